```python
import math
import jax, jax.numpy as jnp
from jax import lax
import numpy as np

D_MODEL = 2048
BATCH = 2
SEQ = 4096
DEPTH = 1
DEC_BATCH = 128
DEC_SEQ = 1
PAST_LEN = 2048
PAGE_SIZE = 128

HEAD_DIM = 64
N_HEADS = 16
D_ATTN = N_HEADS * HEAD_DIM
D_CONV = D_MODEL - D_ATTN
CONV_W = 3
DIL_STEPS = 128
DILATIONS = (1, 4, 16)
WINDOWS = tuple(DIL_STEPS * d for d in DILATIONS)
W_MAX = max(WINDOWS)
N_BUCKETS = 32
MAX_EXACT = N_BUCKETS // 2
MAX_DIST = W_MAX
D_FF = ((8 * D_MODEL + 2) // 3 + 255) // 256 * 256
D_PLE = 256
EPS = 1e-6
SPLITS = [D_ATTN, 2 * D_ATTN, 3 * D_ATTN, 3 * D_ATTN + D_CONV, 3 * D_ATTN + 2 * D_CONV]
D_IN = 3 * D_ATTN + 3 * D_CONV

kernel_name = "hymba_dilated_swa_shortconv_step"


def rms_norm(x, g):
    xf = x.astype(jnp.float32)
    y = xf * lax.rsqrt(jnp.mean(xf * xf, axis=-1, keepdims=True) + EPS)
    return (y * g.astype(jnp.float32)).astype(x.dtype)


def t5_bucket(dist):
    n = np.asarray(dist, np.int32)
    nf = np.maximum(n, 1).astype(np.float32)
    large = MAX_EXACT + (np.log(nf / MAX_EXACT) / np.float32(math.log(MAX_DIST / MAX_EXACT))
                         * (N_BUCKETS - MAX_EXACT)).astype(np.int32)
    large = np.minimum(large, N_BUCKETS - 1)
    return np.where(n < MAX_EXACT, n, large).astype(np.int32)


def merge_dilations(outs, maxes, dens):
    m_all = functools_max(maxes)
    ws = [den * jnp.exp(m - m_all) for m, den in zip(maxes, dens)]
    total = sum(ws)
    return sum(w[..., None] * o for w, o in zip(ws, outs)) / total[..., None]


def functools_max(arrs):
    out = arrs[0]
    for a in arrs[1:]:
        out = jnp.maximum(out, a)
    return out


def dilated_attention_prompt(q, k, v, rel_bias):
    B, S = q.shape[:2]
    span = max(DILATIONS) * DIL_STEPS
    Tp = -(-S // span) * span
    pad = ((0, 0), (0, Tp - S), (0, 0), (0, 0))
    qp, kp, vp = jnp.pad(q, pad), jnp.pad(k, pad), jnp.pad(v, pad)
    scale = HEAD_DIM ** -0.5
    qi = np.arange(DIL_STEPS)[:, None]
    kj = np.arange(2 * DIL_STEPS)[None, :]
    steps = DIL_STEPS + qi - kj
    band = (steps >= 0) & (steps <= DIL_STEPS)
    outs, maxes, dens = [], [], []
    for d in DILATIONS:
        L = Tp // d
        nb = L // DIL_STEPS

        def to_blocks(a):
            return a.reshape(B, L, d, N_HEADS, HEAD_DIM).transpose(0, 2, 1, 3, 4).reshape(
                B, d, nb, DIL_STEPS, N_HEADS, HEAD_DIM)

        def with_prev(a):
            prev = jnp.pad(a[:, :, :-1], ((0, 0), (0, 0), (1, 0), (0, 0), (0, 0), (0, 0)))
            return jnp.concatenate([prev, a], axis=3)

        def from_blocks(a):
            rest = a.shape[4:]
            return a.reshape(B, d, L, *rest).swapaxes(1, 2).reshape(B, Tp, *rest)

        qb = to_blocks(qp)
        kk = with_prev(to_blocks(kp))
        vv = with_prev(to_blocks(vp))
        bias = rel_bias[t5_bucket(d * np.clip(steps, 0, DIL_STEPS))]
        bias = jnp.transpose(bias, (2, 0, 1)).astype(jnp.float32)
        blk = np.arange(nb)[:, None, None]
        valid = band[None] & ((blk > 0) | (kj[None] >= DIL_STEPS))
        s = jnp.einsum('brnqhd,brnkhd->brnhqk', qb, kk,
                       preferred_element_type=jnp.float32) * scale + bias
        s = jnp.where(valid[None, None, :, None], s, -jnp.inf)
        m = jnp.max(s, axis=-1)
        e = jnp.exp(s - m[..., None])
        den = jnp.sum(e, axis=-1)
        o = jnp.einsum('brnhqk,brnkhd->brnqhd', e, vv.astype(jnp.float32))
        o = o / jnp.moveaxis(den, -1, -2)[..., None]
        outs.append(from_blocks(o)[:, :S])
        maxes.append(from_blocks(jnp.moveaxis(m, -1, -2))[:, :S])
        dens.append(from_blocks(jnp.moveaxis(den, -1, -2))[:, :S])
    return merge_dilations(outs, maxes, dens)


def dilated_attention_sample(q, k_new, v_new, cache_k, cache_v, rel_bias):
    S = q.shape[1]
    Lb = cache_k.shape[1]
    scale = HEAD_DIM ** -0.5
    j = np.arange(DIL_STEPS + 1)
    outs, maxes, dens = [], [], []
    for d in DILATIONS:
        idx = Lb + np.arange(S)[:, None] - d * j[None, :]
        valid = idx >= 0
        from_cache = (idx < Lb)[None, :, :, None, None]
        ci = np.clip(idx, 0, Lb - 1)
        ni = np.clip(idx - Lb, 0, S - 1)
        kg = jnp.where(from_cache, cache_k[:, ci], k_new[:, ni])
        vg = jnp.where(from_cache, cache_v[:, ci], v_new[:, ni])
        bias = rel_bias[t5_bucket(d * j)].T.astype(jnp.float32)
        s = jnp.einsum('bshd,bsjhd->bshj', q, kg,
                       preferred_element_type=jnp.float32) * scale + bias
        s = jnp.where(valid[None, :, None, :], s, -jnp.inf)
        m = jnp.max(s, axis=-1)
        e = jnp.exp(s - m[..., None])
        den = jnp.sum(e, axis=-1)
        o = jnp.einsum('bshj,bsjhd->bshd', e, vg.astype(jnp.float32)) / den[..., None]
        outs.append(o)
        maxes.append(m)
        dens.append(den)
    return merge_dilations(outs, maxes, dens)


def short_conv(u, buf, w):
    S = u.shape[1]
    up = jnp.concatenate([buf, u], axis=1)
    y = sum(w[t] * up[:, t:t + S] for t in range(CONV_W))
    return y, up[:, -(CONV_W - 1):]


def mix_inputs(x, g_mix, w_in, q_norm, k_norm):
    n = rms_norm(x, g_mix)
    proj = n @ w_in
    q, k, v, hc, bc, cc = jnp.split(proj, SPLITS, axis=-1)
    hs = x.shape[:2] + (N_HEADS, HEAD_DIM)
    q = rms_norm(q.reshape(hs), q_norm)
    k = rms_norm(k.reshape(hs), k_norm)
    v = v.reshape(hs)
    return q, k, v, cc * hc, bc


def mix_output(attn, conv, g_attn_out, g_conv_out, w_out):
    a = attn.reshape(attn.shape[:2] + (D_ATTN,))
    cat = jnp.concatenate([rms_norm(a, g_attn_out), rms_norm(conv, g_conv_out)], axis=-1)
    return cat @ w_out


def channel_and_ple(h, p, g_ffn, w_gate, w_up, w_down, g_ple, w_ple_gate, w_ple_proj):
    n = rms_norm(h, g_ffn)
    h = h + (jax.nn.silu(n @ w_gate) * (n @ w_up)) @ w_down
    gate = jax.nn.sigmoid(rms_norm(h, g_ple) @ w_ple_gate)
    return h + gate * (p @ w_ple_proj)


def setup_inputs(seed: int = 0) -> dict:
    key = jax.random.key(seed)
    ks = jax.random.split(key, 24)
    f32 = jnp.float32
    nrm = lambda k, shape, s=1.0: jax.random.normal(k, shape, f32) * s
    gain = lambda k, shape: 1.0 + 0.05 * jax.random.normal(k, shape, f32)
    L_buf = min(W_MAX, PAST_LEN)
    return {
        "x_prompt": nrm(ks[0], (BATCH, SEQ, D_MODEL)),
        "x_sample": nrm(ks[1], (DEC_BATCH, DEC_SEQ, D_MODEL)),
        "p_prompt": nrm(ks[2], (DEPTH, BATCH, SEQ, D_PLE)),
        "p_sample": nrm(ks[3], (DEPTH, DEC_BATCH, DEC_SEQ, D_PLE)),
        "cache_k": nrm(ks[4], (DEPTH, DEC_BATCH, L_buf, N_HEADS, HEAD_DIM)),
        "cache_v": nrm(ks[5], (DEPTH, DEC_BATCH, L_buf, N_HEADS, HEAD_DIM)),
        "state_conv": nrm(ks[6], (DEPTH, DEC_BATCH, CONV_W - 1, D_CONV)),
        "rel_bias": nrm(ks[7], (N_BUCKETS, N_HEADS), 0.5),
        "g_mix": gain(ks[8], (DEPTH, D_MODEL)),
        "w_in": nrm(ks[9], (DEPTH, D_MODEL, D_IN), D_MODEL ** -0.5),
        "q_norm": gain(ks[10], (DEPTH, HEAD_DIM)),
        "k_norm": gain(ks[11], (DEPTH, HEAD_DIM)),
        "conv_w": nrm(ks[12], (DEPTH, CONV_W, D_CONV), CONV_W ** -0.5),
        "g_attn_out": gain(ks[13], (DEPTH, D_ATTN)),
        "g_conv_out": gain(ks[14], (DEPTH, D_CONV)),
        "w_out": nrm(ks[15], (DEPTH, D_ATTN + D_CONV, D_MODEL), (D_ATTN + D_CONV) ** -0.5),
        "g_ffn": gain(ks[16], (DEPTH, D_MODEL)),
        "w_gate": nrm(ks[17], (DEPTH, D_MODEL, D_FF), D_MODEL ** -0.5),
        "w_up": nrm(ks[18], (DEPTH, D_MODEL, D_FF), D_MODEL ** -0.5),
        "w_down": nrm(ks[19], (DEPTH, D_FF, D_MODEL), D_FF ** -0.5),
        "g_ple": gain(ks[20], (DEPTH, D_MODEL)),
        "w_ple_gate": nrm(ks[21], (DEPTH, D_MODEL, D_MODEL), D_MODEL ** -0.5),
        "w_ple_proj": nrm(ks[22], (DEPTH, D_PLE, D_MODEL), D_PLE ** -0.5),
    }


def reference(x_prompt, x_sample, p_prompt, p_sample, cache_k, cache_v, state_conv,
              rel_bias, g_mix, w_in, q_norm, k_norm, conv_w, g_attn_out, g_conv_out, w_out,
              g_ffn, w_gate, w_up, w_down, g_ple, w_ple_gate, w_ple_proj):
    hp, hs = x_prompt, x_sample
    S = x_prompt.shape[1]
    kw = min(W_MAX, S)
    kp_l, vp_l, cp_l, ks_l, vs_l, cs_l = [], [], [], [], [], []
    for i in range(DEPTH):
        q, k, v, u, bc = mix_inputs(hp, g_mix[i], w_in[i], q_norm[i], k_norm[i])
        attn = dilated_attention_prompt(q, k, v, rel_bias).astype(hp.dtype)
        buf0 = jnp.zeros((hp.shape[0], CONV_W - 1, D_CONV), u.dtype)
        cy, c_new = short_conv(u, buf0, conv_w[i])
        hp = hp + mix_output(attn, bc * cy, g_attn_out[i], g_conv_out[i], w_out[i])
        hp = channel_and_ple(hp, p_prompt[i], g_ffn[i], w_gate[i], w_up[i], w_down[i],
                             g_ple[i], w_ple_gate[i], w_ple_proj[i])
        kp_l.append(k[:, S - kw:])
        vp_l.append(v[:, S - kw:])
        cp_l.append(c_new)
        q, k, v, u, bc = mix_inputs(hs, g_mix[i], w_in[i], q_norm[i], k_norm[i])
        attn = dilated_attention_sample(q, k, v, cache_k[i], cache_v[i], rel_bias).astype(hs.dtype)
        cy, c_new = short_conv(u, state_conv[i].astype(u.dtype), conv_w[i])
        hs = hs + mix_output(attn, bc * cy, g_attn_out[i], g_conv_out[i], w_out[i])
        hs = channel_and_ple(hs, p_sample[i], g_ffn[i], w_gate[i], w_up[i], w_down[i],
                             g_ple[i], w_ple_gate[i], w_ple_proj[i])
        ks_l.append(k)
        vs_l.append(v)
        cs_l.append(c_new)
    k_prompt = jnp.stack(kp_l)
    v_prompt = jnp.stack(vp_l)
    conv_prompt = jnp.stack(cp_l)
    k_sample = jnp.stack(ks_l)
    v_sample = jnp.stack(vs_l)
    conv_sample = jnp.stack(cs_l)
    return (hp, hs, k_prompt, v_prompt, conv_prompt, k_sample, v_sample, conv_sample)
```

```python
import functools
import math

import numpy as np
import jax
import jax.numpy as jnp
from jax import lax
from jax.experimental import pallas as pl
from jax.experimental.pallas import tpu as pltpu

HEAD_DIM = 64
N_HEADS = 16
D_ATTN = N_HEADS * HEAD_DIM
CONV_W = 3
DIL_STEPS = 128
DILATIONS = (1, 4, 16)
N_BUCKETS = 32
MAX_EXACT = N_BUCKETS // 2
MAX_DIST = DIL_STEPS * max(DILATIONS)
EPS = 1e-6
SCALE = HEAD_DIM ** -0.5

LANES = 128
SUBLANES = 8
HEADS_PER_GROUP = LANES // HEAD_DIM
N_GROUPS = N_HEADS // HEADS_PER_GROUP
SUPER = DIL_STEPS * max(DILATIONS)
N_RES = max(DILATIONS)
MASK_VALUE = -1e30
MASK_BUCKET = N_BUCKETS
TABLE_ROWS = LANES
TABLE_CHUNK = 2048
SAMPLE_POS_CHUNK = 512
VMEM_SLACK_BYTES = 6 * 1024 * 1024

F32 = jnp.float32
BF16 = jnp.bfloat16


def _cparams(sem, vmem_bytes):
    return pltpu.CompilerParams(dimension_semantics=sem,
                                vmem_limit_bytes=int(vmem_bytes + VMEM_SLACK_BYTES))


def _nbytes(shape, dtype):
    return int(np.prod(shape)) * jnp.dtype(dtype).itemsize


def _t5_bucket(dist):
    n = np.asarray(dist, np.int32)
    nf = np.maximum(n, 1).astype(np.float32)
    large = MAX_EXACT + (np.log(nf / MAX_EXACT) / np.float32(math.log(MAX_DIST / MAX_EXACT))
                         * (N_BUCKETS - MAX_EXACT)).astype(np.int32)
    large = np.minimum(large, N_BUCKETS - 1)
    return np.where(n < MAX_EXACT, n, large).astype(np.int32)


def _stored_to_natural(d):
    runs = N_RES // d
    run = DIL_STEPS // runs
    j = np.arange(DIL_STEPS)
    return (j % run) * runs + j // run


def _prompt_bias_index():
    out = np.empty((len(DILATIONS), 2, DIL_STEPS, 2 * DIL_STEPS), np.int32)
    for di, d in enumerate(DILATIONS):
        nat = _stored_to_natural(d)
        qi = nat[:, None]
        kj = np.concatenate([nat, nat + DIL_STEPS])[None, :]
        steps = DIL_STEPS + qi - kj
        band = (steps >= 0) & (steps <= DIL_STEPS)
        bucket = _t5_bucket(d * np.clip(steps, 0, DIL_STEPS))
        out[di, 0] = np.where(band, bucket, MASK_BUCKET)
        out[di, 1] = np.where(band & (kj >= DIL_STEPS), bucket, MASK_BUCKET)
    return out


def _sample_bias_index(lb):
    back = lb - np.arange(lb)
    cached = np.stack([np.where((back % d == 0) & (back // d <= DIL_STEPS), _t5_bucket(back), MASK_BUCKET)
                       for d in DILATIONS])
    new = np.stack([_t5_bucket(d * np.zeros(1, np.int32)) for d in DILATIONS])
    return cached, new


def _bias_table_kernel(tab_ref, idx_ref, o_ref):
    t = tab_ref[...]
    hi = t.astype(BF16)
    r1 = t - hi.astype(F32)
    mid = r1.astype(BF16)
    lo = (r1 - mid.astype(F32)).astype(BF16)
    rows = lax.broadcasted_iota(jnp.int32, (TABLE_ROWS, TABLE_CHUNK), 0)

    def chunk(c, carry):
        sl = pl.ds(pl.multiple_of(c * TABLE_CHUNK, TABLE_CHUNK), TABLE_CHUNK)
        onehot = jnp.where(rows == idx_ref[:, sl], 1.0, 0.0).astype(BF16)
        acc = jnp.dot(hi, onehot, preferred_element_type=F32)
        acc = acc + jnp.dot(mid, onehot, preferred_element_type=F32)
        acc = acc + jnp.dot(lo, onehot, preferred_element_type=F32)
        o_ref[:, sl] = acc
        return carry

    lax.fori_loop(0, o_ref.shape[1] // TABLE_CHUNK, chunk, 0)


def _bias_tables(rel_bias, lb):
    pidx = _prompt_bias_index()
    cached, new = _sample_bias_index(lb)
    flat = np.concatenate([pidx.reshape(-1), cached.reshape(-1), new.reshape(-1)])
    n_pad = -(-(pidx.size + cached.size + LANES) // TABLE_CHUNK) * TABLE_CHUNK
    idx = np.full((1, n_pad), MASK_BUCKET, np.int32)
    idx[0, :flat.size] = flat
    tab = jnp.concatenate(
        [rel_bias.astype(F32).T,
         jnp.full((N_HEADS, 1), MASK_VALUE, F32),
         jnp.zeros((N_HEADS, TABLE_ROWS - N_BUCKETS - 1), F32)], axis=1)
    out = pl.pallas_call(
        _bias_table_kernel,
        out_shape=jax.ShapeDtypeStruct((N_HEADS, n_pad), F32),
        grid=(1,),
        in_specs=[pl.BlockSpec((N_HEADS, TABLE_ROWS), lambda i: (0, 0)),
                  pl.BlockSpec((1, n_pad), lambda i: (0, 0))],
        out_specs=pl.BlockSpec((N_HEADS, n_pad), lambda i: (0, 0)),
        compiler_params=_cparams(("arbitrary",), 2 * (_nbytes((N_HEADS, n_pad), F32)
                                                      + _nbytes((SUBLANES, n_pad), jnp.int32))),
        name="bias_table",
    )(tab, jnp.asarray(idx))
    n_p = pidx.size
    prompt = out[:, :n_p].reshape(N_HEADS, len(DILATIONS), 2, DIL_STEPS, 2 * DIL_STEPS)
    samp = out[:, n_p:n_p + cached.size]
    samp_new = out[:, n_p + cached.size:n_p + cached.size + LANES]
    return prompt, samp, samp_new


def _rms(x, g):
    return x * lax.rsqrt(jnp.mean(x * x, axis=-1, keepdims=True) + EPS) * g


def _head_rms(p, g, gmat):
    ms = jnp.dot((p * p).astype(BF16), gmat, preferred_element_type=F32)
    return p * lax.rsqrt(ms + EPS) * g


def _store_groups(ref, val):
    for g in range(N_GROUPS):
        ref[g] = val[:, g * LANES:(g + 1) * LANES]


def _proj_qkv_kernel(x_ref, g_ref, w_ref, gq_ref, gk_ref, gmat_ref,
                     q_ref, k_ref, v_ref, kt_ref, vt_ref, n_sc):
    j = pl.program_id(1)

    @pl.when(j == 0)
    def _():
        n_sc[...] = _rms(x_ref[...], g_ref[...]).astype(BF16)

    p = jnp.dot(n_sc[...], w_ref[...], preferred_element_type=F32)

    @pl.when(j == 0)
    def _():
        _store_groups(q_ref, _head_rms(p, gq_ref[...], gmat_ref[...]))

    @pl.when(j == 1)
    def _():
        k = _head_rms(p, gk_ref[...], gmat_ref[...])
        kt_ref[...] = k
        _store_groups(k_ref, k)

    @pl.when(j == 2)
    def _():
        vt_ref[...] = p
        _store_groups(v_ref, p)


def _proj_conv_kernel(x_ref, g_ref, w_ref, u_ref, b_ref, n_sc, h_sc):
    j = pl.program_id(1)

    @pl.when(j == 0)
    def _():
        n_sc[...] = _rms(x_ref[...], g_ref[...]).astype(BF16)

    p = jnp.dot(n_sc[...], w_ref[...], preferred_element_type=F32)

    @pl.when(j == 0)
    def _():
        h_sc[...] = p

    @pl.when(j == 1)
    def _():
        b_ref[...] = p

    @pl.when(j == 2)
    def _():
        u_ref[...] = p * h_sc[...]


def _in_proj(x, g_mix, w_in, gq, gk, gmat, tm):
    m, dm = x.shape
    da = D_ATTN
    grid = (m // tm, 3)
    row = lambda i, j: (i, 0)
    fixed = lambda i, j: (0, 0)
    grp = lambda i, j: (0, i, 0)
    grp_shape = jax.ShapeDtypeStruct((N_GROUPS, m, LANES), F32)
    tok_shape = jax.ShapeDtypeStruct((m, da), F32)
    grp_spec = pl.BlockSpec((N_GROUPS, tm, LANES), grp)
    tok_spec = pl.BlockSpec((tm, da), row)
    common = 2 * (_nbytes((tm, dm), F32) + _nbytes((dm, da), BF16)) + _nbytes((tm, dm), BF16)
    q, k, v, kt, vt = pl.pallas_call(
        _proj_qkv_kernel,
        out_shape=(grp_shape, grp_shape, grp_shape, tok_shape, tok_shape),
        grid=grid,
        in_specs=[pl.BlockSpec((tm, dm), row), pl.BlockSpec((1, dm), fixed),
                  pl.BlockSpec((dm, da), lambda i, j: (0, j)),
                  pl.BlockSpec((1, da), fixed), pl.BlockSpec((1, da), fixed),
                  pl.BlockSpec((da, da), fixed)],
        out_specs=(grp_spec, grp_spec, grp_spec, tok_spec, tok_spec),
        scratch_shapes=[pltpu.VMEM((tm, dm), BF16)],
        compiler_params=_cparams(("arbitrary", "arbitrary"),
                                 common + 2 * 5 * _nbytes((tm, da), F32) + 2 * _nbytes((da, da), BF16)),
        name="in_proj_qkv",
    )(x, g_mix, w_in, gq, gk, gmat)
    u, bc = pl.pallas_call(
        _proj_conv_kernel,
        out_shape=(tok_shape, tok_shape),
        grid=grid,
        in_specs=[pl.BlockSpec((tm, dm), row), pl.BlockSpec((1, dm), fixed),
                  pl.BlockSpec((dm, da), lambda i, j: (0, j + 3))],
        out_specs=(tok_spec, tok_spec),
        scratch_shapes=[pltpu.VMEM((tm, dm), BF16), pltpu.VMEM((tm, da), F32)],
        compiler_params=_cparams(("arbitrary", "arbitrary"), common + 5 * _nbytes((tm, da), F32)),
        name="in_proj_conv",
    )(x, g_mix, w_in)
    return q, k, v, kt, vt, u, bc


def _gather_rows(ref, starts, run):
    parts = [ref[pl.ds(s, run), :] for s in starts]
    return parts[0] if len(parts) == 1 else jnp.concatenate(parts, axis=0)


def _scatter_rows(ref, starts, run, val):
    for i, s in enumerate(starts):
        ref[pl.ds(s, run), :] = val[i * run:(i + 1) * run]


def _prompt_attn_kernel(q_ref, k_ref, v_ref, bias_ref, o_ref, qp, kp, vp, acc, m_sc, l_sc):
    sb = pl.program_id(2)
    slot = sb % 2
    pslot = 1 - slot
    kcur, vcur = kp.at[slot], vp.at[slot]
    kprev, vprev = kp.at[pslot], vp.at[pslot]

    for r in range(N_RES):
        rows = pl.ds(r * DIL_STEPS, DIL_STEPS)
        src = pl.ds(r, DIL_STEPS, stride=N_RES)
        qp[rows, :] = q_ref[src, :]
        kcur[rows, :] = k_ref[src, :]
        vcur[rows, :] = v_ref[src, :]

    @pl.when(sb == 0)
    def _():
        kprev[...] = jnp.zeros(kprev.shape, F32)
        vprev[...] = jnp.zeros(vprev.shape, F32)

    head0 = lax.broadcasted_iota(jnp.int32, (DIL_STEPS, LANES), 1) < HEAD_DIM

    def block(di, starts, run, prev_ref_k, prev_ref_v, prev_starts, first, mode, out_rows=None):
        qb = _gather_rows(qp, starts, run)
        kb = jnp.concatenate([_gather_rows(prev_ref_k, prev_starts, run),
                              _gather_rows(kcur, starts, run)], axis=0).astype(BF16)
        vb = jnp.concatenate([_gather_rows(prev_ref_v, prev_starts, run),
                              _gather_rows(vcur, starts, run)], axis=0).astype(BF16)
        ms, ls, os_ = [], [], []
        for h in range(HEADS_PER_GROUP):
            keep = head0 if h == 0 else jnp.logical_not(head0)
            qh = jnp.where(keep, qb, 0.0).astype(BF16)
            s = lax.dot_general(qh, kb, (((1,), (1,)), ((), ())), preferred_element_type=F32)
            s = s + bias_ref[h, di, first]
            mh = jnp.max(s, axis=1, keepdims=True)
            p = jnp.exp(s - mh)
            ls.append(jnp.sum(p, axis=1, keepdims=True))
            os_.append(jnp.dot(p.astype(BF16), vb, preferred_element_type=F32))
            ms.append(mh)
        m_c = jnp.where(head0, ms[0], ms[1])
        l_c = jnp.where(head0, ls[0], ls[1])
        o_c = jnp.where(head0, os_[0], os_[1])
        if mode == "init":
            _scatter_rows(m_sc, starts, run, m_c)
            _scatter_rows(l_sc, starts, run, l_c)
            _scatter_rows(acc, starts, run, o_c)
            return
        m_o = _gather_rows(m_sc, starts, run)
        m_n = jnp.maximum(m_o, m_c)
        a_o = jnp.exp(m_o - m_n)
        a_c = jnp.exp(m_c - m_n)
        l_n = _gather_rows(l_sc, starts, run) * a_o + l_c * a_c
        o_n = _gather_rows(acc, starts, run) * a_o + o_c * a_c
        if mode == "merge":
            _scatter_rows(m_sc, starts, run, m_n)
            _scatter_rows(l_sc, starts, run, l_n)
            _scatter_rows(acc, starts, run, o_n)
        else:
            o_ref[out_rows, :] = o_n / l_n

    def run_dilation(di, mode):
        d = DILATIONS[di]
        runs = N_RES // d
        run = DIL_STEPS // runs
        nblk = SUPER // d // DIL_STEPS

        def body(it, carry):
            c = it // nblk
            n = it % nblk
            starts = [pl.multiple_of((d * b + c) * DIL_STEPS + run * n, SUBLANES) for b in range(runs)]
            pn = jnp.where(n > 0, n - 1, nblk - 1)
            prev_starts = [pl.multiple_of((d * b + c) * DIL_STEPS + run * pn, SUBLANES)
                           for b in range(runs)]
            pslot_n = jnp.where(n > 0, slot, pslot)
            first = jnp.logical_and(sb == 0, n == 0).astype(jnp.int32)
            out_rows = pl.ds(it, DIL_STEPS, stride=N_RES) if mode == "final" else None
            block(di, starts, run, kp.at[pslot_n], vp.at[pslot_n], prev_starts, first, mode, out_rows)
            return carry

        lax.fori_loop(0, d * nblk, body, 0)

    run_dilation(0, "init")
    run_dilation(1, "merge")
    run_dilation(2, "final")


def _prompt_attn(q, k, v, bias, batch, seq):
    nsb = seq // SUPER
    blk = pl.BlockSpec((None, SUPER, LANES), lambda g, b, s: (g, b * nsb + s, 0))
    bias_spec = pl.BlockSpec((HEADS_PER_GROUP,) + bias.shape[1:], lambda g, b, s: (g, 0, 0, 0, 0))
    blk_bytes = _nbytes((SUPER, LANES), F32)
    return pl.pallas_call(
        _prompt_attn_kernel,
        out_shape=jax.ShapeDtypeStruct(q.shape, F32),
        grid=(N_GROUPS, batch, nsb),
        in_specs=[blk, blk, blk, bias_spec],
        out_specs=blk,
        scratch_shapes=[pltpu.VMEM((SUPER, LANES), F32),
                        pltpu.VMEM((2, SUPER, LANES), F32), pltpu.VMEM((2, SUPER, LANES), F32),
                        pltpu.VMEM((SUPER, LANES), F32), pltpu.VMEM((SUPER, LANES), F32),
                        pltpu.VMEM((SUPER, LANES), F32)],
        compiler_params=_cparams(("arbitrary", "arbitrary", "arbitrary"),
                                 (2 * 4 + 8) * blk_bytes
                                 + 2 * _nbytes((HEADS_PER_GROUP,) + bias.shape[1:], F32)),
        name="prompt_attn",
    )(q, k, v, bias)


def _sample_attn_kernel(q_ref, kn_ref, vn_ref, kt_ref, vt_ref, bias_ref, bnew_ref, o_ref):
    lb = kt_ref.shape[1]
    nd = len(DILATIONS)
    row = pl.ds(pl.program_id(0), 1)
    seg = (lax.broadcasted_iota(jnp.int32, (N_HEADS, D_ATTN), 1) // HEAD_DIM
           == lax.broadcasted_iota(jnp.int32, (N_HEADS, D_ATTN), 0))
    qe = jnp.where(seg, q_ref[row, :], 0.0).astype(BF16)
    kn = kn_ref[row, :].astype(BF16).astype(F32)
    s_new = jnp.sum(qe.astype(F32) * kn, axis=1, keepdims=True)
    sn = [s_new + bnew_ref[:, di:di + 1] for di in range(nd)]
    starts = range(0, lb, SAMPLE_POS_CHUNK)
    chunks = [pl.ds(c, SAMPLE_POS_CHUNK) for c in starts]
    sc = []
    for c, ch in zip(starts, chunks):
        s = jnp.dot(qe, kt_ref[:, ch].astype(BF16), preferred_element_type=F32)
        sc.append([s + bias_ref[:, pl.ds(di * lb + c, SAMPLE_POS_CHUNK)] for di in range(nd)])
    m = functools.reduce(jnp.maximum, [jnp.max(x, axis=1, keepdims=True) for row_ in sc for x in row_] + sn)
    den = functools.reduce(jnp.add, [jnp.exp(x - m) for x in sn])
    o = den * vn_ref[row, :]
    for ch, sd in zip(chunks, sc):
        p = functools.reduce(jnp.add, [jnp.exp(x - m) for x in sd])
        den = den + jnp.sum(p, axis=1, keepdims=True)
        o = o + lax.dot_general(p.astype(BF16), vt_ref[:, ch].astype(BF16),
                                (((1,), (1,)), ((), ())), preferred_element_type=F32)
    o_ref[row, :] = jnp.sum(jnp.where(seg, o / den, 0.0), axis=0, keepdims=True)


def _sample_attn(q, kn, vn, cache_k, cache_v, bias, bias_new):
    bd, lb = cache_k.shape[0], cache_k.shape[1]
    da = D_ATTN
    assert lb % SAMPLE_POS_CHUNK == 0

    def feature_major(c):
        return jnp.transpose(c, (0, 2, 3, 1)).reshape(bd, da, lb)

    win = pl.BlockSpec((None, da, lb), lambda i: (i, 0, 0))
    full = pl.BlockSpec((bd, da), lambda i: (0, 0))
    return pl.pallas_call(
        _sample_attn_kernel,
        out_shape=jax.ShapeDtypeStruct((bd, da), F32),
        grid=(bd,),
        in_specs=[full, full, full, win, win,
                  pl.BlockSpec(bias.shape, lambda i: (0, 0)), pl.BlockSpec(bias_new.shape, lambda i: (0, 0))],
        out_specs=full,
        compiler_params=_cparams(("arbitrary",), 2 * (2 * _nbytes((da, lb), F32) + 4 * _nbytes((bd, da), F32)
                                                      + _nbytes(bias.shape, F32))),
        name="sample_attn",
    )(q, kn, vn, feature_major(cache_k), feature_major(cache_v), bias, bias_new)


def _mix_tail(x_ref, a, conv, ga_ref, gc_ref, w_ref, o_ref):
    cat = jnp.concatenate([_rms(a, ga_ref[...]), _rms(conv, gc_ref[...])], axis=1).astype(BF16)
    o_ref[...] = x_ref[...] + jnp.dot(cat, w_ref[...], preferred_element_type=F32)


def _mix_prompt_kernel(x_ref, a_ref, u_ref, halo_ref, b_ref, cw_ref, ga_ref, gc_ref, w_ref, o_ref,
                       *, tiles_per_seq):
    tm = u_ref.shape[0]
    u = u_ref[...]
    seq_start = pl.program_id(0) % tiles_per_seq == 0
    halo = jnp.where(seq_start, 0.0, halo_ref[...])
    rid = lax.broadcasted_iota(jnp.int32, u.shape, 0)
    u1 = jnp.where(rid == 0, halo[7:8], pltpu.roll(u, 1, axis=0))
    u2 = jnp.where(rid == 0, halo[6:7], jnp.where(rid == 1, halo[7:8], pltpu.roll(u, 2, axis=0)))
    cy = cw_ref[0:1] * u2 + cw_ref[1:2] * u1 + cw_ref[2:3] * u
    a = jnp.concatenate([a_ref[g] for g in range(N_GROUPS)], axis=1)
    _mix_tail(x_ref, a, b_ref[...] * cy, ga_ref, gc_ref, w_ref, o_ref)


def _mix_sample_kernel(x_ref, a_ref, u_ref, buf_ref, b_ref, cw_ref, ga_ref, gc_ref, w_ref, o_ref):
    dc = u_ref.shape[1]
    cy = cw_ref[0:1] * buf_ref[:, :dc] + cw_ref[1:2] * buf_ref[:, dc:] + cw_ref[2:3] * u_ref[...]
    _mix_tail(x_ref, a_ref[...], b_ref[...] * cy, ga_ref, gc_ref, w_ref, o_ref)


def _mix_out_prompt(x, attn, u, bc, conv_w, ga, gc, w_out, seq, tm):
    m, dm = x.shape
    dc = u.shape[1]
    row = lambda i: (i, 0)
    fixed = lambda i: (0, 0)
    halo_blocks = tm // SUBLANES
    tile = _nbytes((tm, dc), F32)
    return pl.pallas_call(
        functools.partial(_mix_prompt_kernel, tiles_per_seq=seq // tm),
        out_shape=jax.ShapeDtypeStruct((m, dm), F32),
        grid=(m // tm,),
        in_specs=[pl.BlockSpec((tm, dm), row),
                  pl.BlockSpec((N_GROUPS, tm, LANES), lambda i: (0, i, 0)),
                  pl.BlockSpec((tm, dc), row),
                  pl.BlockSpec((SUBLANES, dc), lambda i: (jnp.maximum(i * halo_blocks - 1, 0), 0)),
                  pl.BlockSpec((tm, dc), row),
                  pl.BlockSpec((CONV_W, dc), fixed), pl.BlockSpec((1, D_ATTN), fixed),
                  pl.BlockSpec((1, dc), fixed), pl.BlockSpec(w_out.shape, fixed)],
        out_specs=pl.BlockSpec((tm, dm), row),
        compiler_params=_cparams(("arbitrary",), 2 * (7 * tile + _nbytes(w_out.shape, BF16))),
        name="mix_out_prompt",
    )(x, attn, u, u, bc, conv_w, ga, gc, w_out)


def _mix_out_sample(x, attn, u, buf, bc, conv_w, ga, gc, w_out):
    m, dm = x.shape
    dc = u.shape[1]
    full = lambda a: pl.BlockSpec(a.shape, lambda i: (0,) * a.ndim)
    args = (x, attn, u, buf, bc, conv_w, ga, gc, w_out)
    return pl.pallas_call(
        _mix_sample_kernel,
        out_shape=jax.ShapeDtypeStruct((m, dm), F32),
        grid=(1,),
        in_specs=[full(a) for a in args],
        out_specs=pl.BlockSpec((m, dm), lambda i: (0, 0)),
        compiler_params=_cparams(("arbitrary",), 2 * (8 * _nbytes((m, dc), F32) + _nbytes(w_out.shape, BF16))),
        name="mix_out_sample",
    )(*args)


def _ffn_kernel(x_ref, g_ref, wg_ref, wu_ref, wd_ref, o_ref, n_sc):
    f = pl.program_id(1)

    @pl.when(f == 0)
    def _():
        x = x_ref[...]
        n_sc[...] = _rms(x, g_ref[...]).astype(BF16)
        o_ref[...] = x

    n = n_sc[...]
    gate = jnp.dot(n, wg_ref[...], preferred_element_type=F32)
    up = jnp.dot(n, wu_ref[...], preferred_element_type=F32)
    act = (gate / (1.0 + jnp.exp(-gate)) * up).astype(BF16)
    o_ref[...] += jnp.dot(act, wd_ref[...], preferred_element_type=F32)


def _ffn(x, g, wg, wu, wd, tm, tf):
    m, dm = x.shape
    dff = wg.shape[1]
    row = lambda i, f: (i, 0)
    return pl.pallas_call(
        _ffn_kernel,
        out_shape=jax.ShapeDtypeStruct((m, dm), F32),
        grid=(m // tm, dff // tf),
        in_specs=[pl.BlockSpec((tm, dm), row), pl.BlockSpec((1, dm), lambda i, f: (0, 0)),
                  pl.BlockSpec((dm, tf), lambda i, f: (0, f)), pl.BlockSpec((dm, tf), lambda i, f: (0, f)),
                  pl.BlockSpec((tf, dm), lambda i, f: (f, 0))],
        out_specs=pl.BlockSpec((tm, dm), row),
        scratch_shapes=[pltpu.VMEM((tm, dm), BF16)],
        compiler_params=_cparams(("arbitrary", "arbitrary"),
                                 4 * _nbytes((tm, dm), F32) + _nbytes((tm, dm), BF16)
                                 + 2 * 3 * _nbytes((dm, tf), BF16) + 3 * _nbytes((tm, tf), F32)),
        name="ffn",
    )(x, g, wg, wu, wd)


def _ple_kernel(x_ref, p_ref, g_ref, wg_ref, wp_ref, o_ref):
    x = x_ref[...]
    z = jnp.dot(_rms(x, g_ref[...]).astype(BF16), wg_ref[...], preferred_element_type=F32)
    e = jnp.dot(p_ref[...].astype(BF16), wp_ref[...], preferred_element_type=F32)
    o_ref[...] = x + e / (1.0 + jnp.exp(-z))


def _ple(x, p, g, wg, wp, tm):
    m, dm = x.shape
    dp = p.shape[1]
    row = lambda i: (i, 0)
    fixed = lambda i: (0, 0)
    return pl.pallas_call(
        _ple_kernel,
        out_shape=jax.ShapeDtypeStruct((m, dm), F32),
        grid=(m // tm,),
        in_specs=[pl.BlockSpec((tm, dm), row), pl.BlockSpec((tm, dp), row), pl.BlockSpec((1, dm), fixed),
                  pl.BlockSpec(wg.shape, fixed), pl.BlockSpec(wp.shape, fixed)],
        out_specs=pl.BlockSpec((tm, dm), row),
        compiler_params=_cparams(("arbitrary",),
                                 2 * (2 * _nbytes((tm, dm), F32) + _nbytes((tm, dp), F32)
                                      + _nbytes(wg.shape, BF16) + _nbytes(wp.shape, BF16))
                                 + 2 * _nbytes((tm, dm), F32)),
        name="ple",
    )(x, p, g, wg, wp)


def _tiles(m):
    return dict(proj=min(m, 512), mix=min(m, 256), ffn=min(m, 512), ffn_cols=512, ple=min(m, 512))


def kernel(x_prompt, x_sample, p_prompt, p_sample, cache_k, cache_v, state_conv, rel_bias, g_mix, w_in,
           q_norm, k_norm, conv_w, g_attn_out, g_conv_out, w_out, g_ffn, w_gate, w_up, w_down, g_ple,
           w_ple_gate, w_ple_proj):
    depth = g_mix.shape[0]
    batch, seq, dm = x_prompt.shape
    bd, dec_seq, _ = x_sample.shape
    dc = conv_w.shape[2]
    assert depth == 1 and dec_seq == 1, "single layer, one new position per sample"
    assert seq % SUPER == 0 and dm == D_ATTN + dc and dc == D_ATTN
    assert w_gate.shape[2] % 512 == 0

    bias_p, bias_s, bias_s_new = _bias_tables(rel_bias, cache_k.shape[2])
    gmat = jnp.asarray(np.kron(np.eye(N_HEADS, dtype=np.float32),
                               np.full((HEAD_DIM, HEAD_DIM), 1.0 / HEAD_DIM, np.float32)), BF16)

    i = 0
    row2 = lambda a: a.reshape(1, -1).astype(F32)
    g_mix_i, g_ffn_i, g_ple_i = row2(g_mix[i]), row2(g_ffn[i]), row2(g_ple[i])
    gq = row2(jnp.tile(q_norm[i], N_HEADS) * SCALE)
    gk = row2(jnp.tile(k_norm[i], N_HEADS))
    ga, gc = row2(g_attn_out[i]), row2(g_conv_out[i])
    cw = conv_w[i].astype(F32)
    w_in_i, w_out_i = w_in[i].astype(BF16), w_out[i].astype(BF16)
    wg_i, wu_i, wd_i = w_gate[i].astype(BF16), w_up[i].astype(BF16), w_down[i].astype(BF16)
    wpg_i, wpp_i = w_ple_gate[i].astype(BF16), w_ple_proj[i].astype(BF16)

    mp = batch * seq
    tp = _tiles(mp)
    xp = x_prompt.reshape(mp, dm)
    q, k, v, kt, vt, u, bc = _in_proj(xp, g_mix_i, w_in_i, gq, gk, gmat, tp["proj"])
    attn = _prompt_attn(q, k, v, bias_p, batch, seq)
    h = _mix_out_prompt(xp, attn, u, bc, cw, ga, gc, w_out_i, seq, tp["mix"])
    h = _ffn(h, g_ffn_i, wg_i, wu_i, wd_i, tp["ffn"], tp["ffn_cols"])
    h = _ple(h, p_prompt[i].reshape(mp, -1), g_ple_i, wpg_i, wpp_i, tp["ple"])
    y_prompt = h.reshape(batch, seq, dm)
    kw = min(SUPER, seq)
    k_prompt = kt.reshape(batch, seq, N_HEADS, HEAD_DIM)[None, :, seq - kw:]
    v_prompt = vt.reshape(batch, seq, N_HEADS, HEAD_DIM)[None, :, seq - kw:]
    conv_prompt = u.reshape(batch, seq, dc)[None, :, seq - (CONV_W - 1):]

    ts = _tiles(bd)
    xs = x_sample.reshape(bd, dm)
    qs, _, _, kts, vts, us, bcs = _in_proj(xs, g_mix_i, w_in_i, gq, gk, gmat, ts["proj"])
    qs = jnp.concatenate([qs[g] for g in range(N_GROUPS)], axis=1)
    attn_s = _sample_attn(qs, kts, vts, cache_k[i], cache_v[i], bias_s, bias_s_new)
    buf = state_conv[i].astype(F32)
    hs = _mix_out_sample(xs, attn_s, us, buf.reshape(bd, (CONV_W - 1) * dc), bcs, cw, ga, gc, w_out_i)
    hs = _ffn(hs, g_ffn_i, wg_i, wu_i, wd_i, ts["ffn"], ts["ffn_cols"])
    hs = _ple(hs, p_sample[i].reshape(bd, -1), g_ple_i, wpg_i, wpp_i, ts["ple"])
    y_sample = hs.reshape(bd, dec_seq, dm)
    k_sample = kts.reshape(1, bd, dec_seq, N_HEADS, HEAD_DIM)
    v_sample = vts.reshape(1, bd, dec_seq, N_HEADS, HEAD_DIM)
    conv_sample = jnp.concatenate([buf[:, 1:], us[:, None, :]], axis=1)[None]

    return (y_prompt, y_sample, k_prompt, v_prompt, conv_prompt, k_sample, v_sample, conv_sample)
```

```python
import functools
import math

import numpy as np
import jax
import jax.numpy as jnp
from jax import lax
from jax.experimental import pallas as pl
from jax.experimental.pallas import tpu as pltpu

HEAD_DIM = 64
N_HEADS = 16
D_ATTN = N_HEADS * HEAD_DIM
CONV_W = 3
DIL_STEPS = 128
DILATIONS = (1, 4, 16)
N_BUCKETS = 32
MAX_EXACT = N_BUCKETS // 2
MAX_DIST = DIL_STEPS * max(DILATIONS)
EPS = 1e-6
SCALE = HEAD_DIM ** -0.5

LANES = 128
SUBLANES = 8
HEADS_PER_GROUP = LANES // HEAD_DIM
N_GROUPS = N_HEADS // HEADS_PER_GROUP
SUPER = DIL_STEPS * max(DILATIONS)
N_RES = max(DILATIONS)
MASK_VALUE = -1e30
MASK_BUCKET = N_BUCKETS
TABLE_ROWS = LANES
TABLE_CHUNK = 2048
SAMPLE_POS_CHUNK = 512
ATTN_UNROLL = 16
VMEM_SLACK_BYTES = 6 * 1024 * 1024

F32 = jnp.float32
BF16 = jnp.bfloat16


def _cparams(sem, vmem_bytes):
    return pltpu.CompilerParams(dimension_semantics=sem,
                                vmem_limit_bytes=int(vmem_bytes + VMEM_SLACK_BYTES))


def _nbytes(shape, dtype):
    return int(np.prod(shape)) * jnp.dtype(dtype).itemsize


def _t5_bucket(dist):
    n = np.asarray(dist, np.int32)
    nf = np.maximum(n, 1).astype(np.float32)
    large = MAX_EXACT + (np.log(nf / MAX_EXACT) / np.float32(math.log(MAX_DIST / MAX_EXACT))
                         * (N_BUCKETS - MAX_EXACT)).astype(np.int32)
    large = np.minimum(large, N_BUCKETS - 1)
    return np.where(n < MAX_EXACT, n, large).astype(np.int32)


def _stored_to_natural(d):
    runs = N_RES // d
    run = DIL_STEPS // runs
    j = np.arange(DIL_STEPS)
    return (j % run) * runs + j // run


def _prompt_bias_index():
    out = np.empty((len(DILATIONS), 2, DIL_STEPS, 2 * DIL_STEPS), np.int32)
    for di, d in enumerate(DILATIONS):
        nat = _stored_to_natural(d)
        qi = nat[:, None]
        kj = np.concatenate([nat, nat + DIL_STEPS])[None, :]
        steps = DIL_STEPS + qi - kj
        band = (steps >= 0) & (steps <= DIL_STEPS)
        bucket = _t5_bucket(d * np.clip(steps, 0, DIL_STEPS))
        out[di, 0] = np.where(band, bucket, MASK_BUCKET)
        out[di, 1] = np.where(band & (kj >= DIL_STEPS), bucket, MASK_BUCKET)
    return out


def _sample_bias_index(lb):
    back = lb - np.arange(lb)
    cached = np.stack([np.where((back % d == 0) & (back // d <= DIL_STEPS), _t5_bucket(back), MASK_BUCKET)
                       for d in DILATIONS])
    new = np.stack([_t5_bucket(d * np.zeros(1, np.int32)) for d in DILATIONS])
    return cached, new


def _bias_table_kernel(tab_ref, idx_ref, o_ref):
    t = tab_ref[...]
    hi = t.astype(BF16)
    r1 = t - hi.astype(F32)
    mid = r1.astype(BF16)
    lo = (r1 - mid.astype(F32)).astype(BF16)
    rows = lax.broadcasted_iota(jnp.int32, (TABLE_ROWS, TABLE_CHUNK), 0)

    def chunk(c, carry):
        sl = pl.ds(pl.multiple_of(c * TABLE_CHUNK, TABLE_CHUNK), TABLE_CHUNK)
        onehot = jnp.where(rows == idx_ref[:, sl], 1.0, 0.0).astype(BF16)
        acc = jnp.dot(hi, onehot, preferred_element_type=F32)
        acc = acc + jnp.dot(mid, onehot, preferred_element_type=F32)
        acc = acc + jnp.dot(lo, onehot, preferred_element_type=F32)
        o_ref[:, sl] = acc
        return carry

    lax.fori_loop(0, o_ref.shape[1] // TABLE_CHUNK, chunk, 0)


def _bias_tables(rel_bias, lb):
    pidx = _prompt_bias_index()
    cached, new = _sample_bias_index(lb)
    flat = np.concatenate([pidx.reshape(-1), cached.reshape(-1), new.reshape(-1)])
    n_pad = -(-(pidx.size + cached.size + LANES) // TABLE_CHUNK) * TABLE_CHUNK
    idx = np.full((1, n_pad), MASK_BUCKET, np.int32)
    idx[0, :flat.size] = flat
    tab = jnp.concatenate(
        [rel_bias.astype(F32).T,
         jnp.full((N_HEADS, 1), MASK_VALUE, F32),
         jnp.zeros((N_HEADS, TABLE_ROWS - N_BUCKETS - 1), F32)], axis=1)
    out = pl.pallas_call(
        _bias_table_kernel,
        out_shape=jax.ShapeDtypeStruct((N_HEADS, n_pad), F32),
        grid=(1,),
        in_specs=[pl.BlockSpec((N_HEADS, TABLE_ROWS), lambda i: (0, 0)),
                  pl.BlockSpec((1, n_pad), lambda i: (0, 0))],
        out_specs=pl.BlockSpec((N_HEADS, n_pad), lambda i: (0, 0)),
        compiler_params=_cparams(("arbitrary",), 2 * (_nbytes((N_HEADS, n_pad), F32)
                                                      + _nbytes((SUBLANES, n_pad), jnp.int32))),
        name="bias_table",
    )(tab, jnp.asarray(idx))
    n_p = pidx.size
    prompt = out[:, :n_p].reshape(N_HEADS, len(DILATIONS), 2, DIL_STEPS, 2 * DIL_STEPS)
    samp = out[:, n_p:n_p + cached.size]
    samp_new = out[:, n_p + cached.size:n_p + cached.size + LANES]
    return prompt, samp, samp_new


def _rms(x, g):
    return x * lax.rsqrt(jnp.mean(x * x, axis=-1, keepdims=True) + EPS) * g


def _head_rms(p, g, gmat):
    ms = jnp.dot((p * p).astype(BF16), gmat, preferred_element_type=F32)
    return p * lax.rsqrt(ms + EPS) * g


def _store_groups(ref, val):
    for g in range(N_GROUPS):
        ref[g] = val[:, g * LANES:(g + 1) * LANES]


def _proj_qkv_kernel(x_ref, g_ref, w_ref, gq_ref, gk_ref, gmat_ref,
                     q_ref, k_ref, v_ref, kt_ref, vt_ref, n_sc):
    j = pl.program_id(1)

    @pl.when(j == 0)
    def _():
        n_sc[...] = _rms(x_ref[...], g_ref[...]).astype(BF16)

    p = jnp.dot(n_sc[...], w_ref[...], preferred_element_type=F32)

    @pl.when(j == 0)
    def _():
        _store_groups(q_ref, _head_rms(p, gq_ref[...], gmat_ref[...]))

    @pl.when(j == 1)
    def _():
        k = _head_rms(p, gk_ref[...], gmat_ref[...])
        kt_ref[...] = k
        _store_groups(k_ref, k)

    @pl.when(j == 2)
    def _():
        vt_ref[...] = p
        _store_groups(v_ref, p)


def _proj_conv_kernel(x_ref, g_ref, w_ref, u_ref, b_ref, n_sc, h_sc):
    j = pl.program_id(1)

    @pl.when(j == 0)
    def _():
        n_sc[...] = _rms(x_ref[...], g_ref[...]).astype(BF16)

    p = jnp.dot(n_sc[...], w_ref[...], preferred_element_type=F32)

    @pl.when(j == 0)
    def _():
        h_sc[...] = p

    @pl.when(j == 1)
    def _():
        b_ref[...] = p

    @pl.when(j == 2)
    def _():
        u_ref[...] = p * h_sc[...]


def _in_proj(x, g_mix, w_in, gq, gk, gmat, tm):
    m, dm = x.shape
    da = D_ATTN
    grid = (m // tm, 3)
    row = lambda i, j: (i, 0)
    fixed = lambda i, j: (0, 0)
    grp = lambda i, j: (0, i, 0)
    grp_shape = jax.ShapeDtypeStruct((N_GROUPS, m, LANES), F32)
    tok_shape = jax.ShapeDtypeStruct((m, da), F32)
    grp_spec = pl.BlockSpec((N_GROUPS, tm, LANES), grp)
    tok_spec = pl.BlockSpec((tm, da), row)
    common = 2 * (_nbytes((tm, dm), F32) + _nbytes((dm, da), BF16)) + _nbytes((tm, dm), BF16)
    q, k, v, kt, vt = pl.pallas_call(
        _proj_qkv_kernel,
        out_shape=(grp_shape, grp_shape, grp_shape, tok_shape, tok_shape),
        grid=grid,
        in_specs=[pl.BlockSpec((tm, dm), row), pl.BlockSpec((1, dm), fixed),
                  pl.BlockSpec((dm, da), lambda i, j: (0, j)),
                  pl.BlockSpec((1, da), fixed), pl.BlockSpec((1, da), fixed),
                  pl.BlockSpec((da, da), fixed)],
        out_specs=(grp_spec, grp_spec, grp_spec, tok_spec, tok_spec),
        scratch_shapes=[pltpu.VMEM((tm, dm), BF16)],
        compiler_params=_cparams(("arbitrary", "arbitrary"),
                                 common + 2 * 5 * _nbytes((tm, da), F32) + 2 * _nbytes((da, da), BF16)),
        name="in_proj_qkv",
    )(x, g_mix, w_in, gq, gk, gmat)
    u, bc = pl.pallas_call(
        _proj_conv_kernel,
        out_shape=(tok_shape, tok_shape),
        grid=grid,
        in_specs=[pl.BlockSpec((tm, dm), row), pl.BlockSpec((1, dm), fixed),
                  pl.BlockSpec((dm, da), lambda i, j: (0, j + 3))],
        out_specs=(tok_spec, tok_spec),
        scratch_shapes=[pltpu.VMEM((tm, dm), BF16), pltpu.VMEM((tm, da), F32)],
        compiler_params=_cparams(("arbitrary", "arbitrary"), common + 5 * _nbytes((tm, da), F32)),
        name="in_proj_conv",
    )(x, g_mix, w_in)
    return q, k, v, kt, vt, u, bc


def _gather_rows(ref, starts, run):
    parts = [ref[pl.ds(s, run), :] for s in starts]
    return parts[0] if len(parts) == 1 else jnp.concatenate(parts, axis=0)


def _scatter_rows(ref, starts, run, val):
    for i, s in enumerate(starts):
        ref[pl.ds(s, run), :] = val[i * run:(i + 1) * run]


def _prompt_attn_kernel(q_ref, k_ref, v_ref, bias_ref, o_ref, qp, kp, vp, acc, m_sc, l_sc):
    sb = pl.program_id(2)
    slot = sb % 2
    pslot = 1 - slot
    kcur, vcur = kp.at[slot], vp.at[slot]
    kprev, vprev = kp.at[pslot], vp.at[pslot]

    for r in range(N_RES):
        rows = pl.ds(r * DIL_STEPS, DIL_STEPS)
        src = pl.ds(r, DIL_STEPS, stride=N_RES)
        qp[rows, :] = q_ref[src, :]
        kcur[rows, :] = k_ref[src, :]
        vcur[rows, :] = v_ref[src, :]

    @pl.when(sb == 0)
    def _():
        kprev[...] = jnp.zeros(kprev.shape, F32)
        vprev[...] = jnp.zeros(vprev.shape, F32)

    head0 = lax.broadcasted_iota(jnp.int32, (DIL_STEPS, LANES), 1) < HEAD_DIM

    def block(di, starts, run, prev_ref_k, prev_ref_v, prev_starts, first, mode, out_rows=None):
        qb = _gather_rows(qp, starts, run)
        kb = jnp.concatenate([_gather_rows(prev_ref_k, prev_starts, run),
                              _gather_rows(kcur, starts, run)], axis=0).astype(BF16)
        vb = jnp.concatenate([_gather_rows(prev_ref_v, prev_starts, run),
                              _gather_rows(vcur, starts, run)], axis=0).astype(BF16)
        ms, ls, os_ = [], [], []
        for h in range(HEADS_PER_GROUP):
            keep = head0 if h == 0 else jnp.logical_not(head0)
            qh = jnp.where(keep, qb, 0.0).astype(BF16)
            s = lax.dot_general(qh, kb, (((1,), (1,)), ((), ())), preferred_element_type=F32)
            s = s + bias_ref[h, di, first]
            mh = jnp.max(s, axis=1, keepdims=True)
            p = jnp.exp(s - mh)
            ls.append(jnp.sum(p, axis=1, keepdims=True))
            os_.append(jnp.dot(p.astype(BF16), vb, preferred_element_type=F32))
            ms.append(mh)
        m_c = jnp.where(head0, ms[0], ms[1])
        l_c = jnp.where(head0, ls[0], ls[1])
        o_c = jnp.where(head0, os_[0], os_[1])
        if mode == "init":
            _scatter_rows(m_sc, starts, run, m_c)
            _scatter_rows(l_sc, starts, run, l_c)
            _scatter_rows(acc, starts, run, o_c)
            return
        m_o = _gather_rows(m_sc, starts, run)
        m_n = jnp.maximum(m_o, m_c)
        a_o = jnp.exp(m_o - m_n)
        a_c = jnp.exp(m_c - m_n)
        l_n = _gather_rows(l_sc, starts, run) * a_o + l_c * a_c
        o_n = _gather_rows(acc, starts, run) * a_o + o_c * a_c
        if mode == "merge":
            _scatter_rows(m_sc, starts, run, m_n)
            _scatter_rows(l_sc, starts, run, l_n)
            _scatter_rows(acc, starts, run, o_n)
        else:
            o_ref[out_rows, :] = o_n / l_n

    def run_dilation(di, mode):
        d = DILATIONS[di]
        runs = N_RES // d
        run = DIL_STEPS // runs
        nblk = SUPER // d // DIL_STEPS

        def body(it, carry):
            c = it // nblk
            n = it % nblk
            starts = [pl.multiple_of((d * b + c) * DIL_STEPS + run * n, SUBLANES) for b in range(runs)]
            pn = jnp.where(n > 0, n - 1, nblk - 1)
            prev_starts = [pl.multiple_of((d * b + c) * DIL_STEPS + run * pn, SUBLANES)
                           for b in range(runs)]
            pslot_n = jnp.where(n > 0, slot, pslot)
            first = jnp.logical_and(sb == 0, n == 0).astype(jnp.int32)
            out_rows = pl.ds(it, DIL_STEPS, stride=N_RES) if mode == "final" else None
            block(di, starts, run, kp.at[pslot_n], vp.at[pslot_n], prev_starts, first, mode, out_rows)
            return carry

        lax.fori_loop(0, d * nblk, body, 0, unroll=ATTN_UNROLL)

    run_dilation(0, "init")
    run_dilation(1, "merge")
    run_dilation(2, "final")


def _prompt_attn(q, k, v, bias, batch, seq):
    nsb = seq // SUPER
    blk = pl.BlockSpec((None, SUPER, LANES), lambda g, b, s: (g, b * nsb + s, 0))
    bias_spec = pl.BlockSpec((HEADS_PER_GROUP,) + bias.shape[1:], lambda g, b, s: (g, 0, 0, 0, 0))
    blk_bytes = _nbytes((SUPER, LANES), F32)
    return pl.pallas_call(
        _prompt_attn_kernel,
        out_shape=jax.ShapeDtypeStruct(q.shape, F32),
        grid=(N_GROUPS, batch, nsb),
        in_specs=[blk, blk, blk, bias_spec],
        out_specs=blk,
        scratch_shapes=[pltpu.VMEM((SUPER, LANES), F32),
                        pltpu.VMEM((2, SUPER, LANES), F32), pltpu.VMEM((2, SUPER, LANES), F32),
                        pltpu.VMEM((SUPER, LANES), F32), pltpu.VMEM((SUPER, LANES), F32),
                        pltpu.VMEM((SUPER, LANES), F32)],
        compiler_params=_cparams(("arbitrary", "arbitrary", "arbitrary"),
                                 (2 * 4 + 8) * blk_bytes
                                 + 2 * _nbytes((HEADS_PER_GROUP,) + bias.shape[1:], F32)),
        name="prompt_attn",
    )(q, k, v, bias)


def _sample_attn_kernel(q_ref, kn_ref, vn_ref, kt_ref, vt_ref, bias_ref, bnew_ref, o_ref):
    lb = kt_ref.shape[1]
    nd = len(DILATIONS)
    row = pl.ds(pl.program_id(0), 1)
    seg = (lax.broadcasted_iota(jnp.int32, (N_HEADS, D_ATTN), 1) // HEAD_DIM
           == lax.broadcasted_iota(jnp.int32, (N_HEADS, D_ATTN), 0))
    qe = jnp.where(seg, q_ref[row, :], 0.0).astype(BF16)
    kn = kn_ref[row, :].astype(BF16).astype(F32)
    s_new = jnp.sum(qe.astype(F32) * kn, axis=1, keepdims=True)
    sn = [s_new + bnew_ref[:, di:di + 1] for di in range(nd)]
    starts = range(0, lb, SAMPLE_POS_CHUNK)
    chunks = [pl.ds(c, SAMPLE_POS_CHUNK) for c in starts]
    sc = []
    for c, ch in zip(starts, chunks):
        s = jnp.dot(qe, kt_ref[:, ch].astype(BF16), preferred_element_type=F32)
        sc.append([s + bias_ref[:, pl.ds(di * lb + c, SAMPLE_POS_CHUNK)] for di in range(nd)])
    m = functools.reduce(jnp.maximum, [jnp.max(x, axis=1, keepdims=True) for row_ in sc for x in row_] + sn)
    den = functools.reduce(jnp.add, [jnp.exp(x - m) for x in sn])
    o = den * vn_ref[row, :]
    for ch, sd in zip(chunks, sc):
        p = functools.reduce(jnp.add, [jnp.exp(x - m) for x in sd])
        den = den + jnp.sum(p, axis=1, keepdims=True)
        o = o + lax.dot_general(p.astype(BF16), vt_ref[:, ch].astype(BF16),
                                (((1,), (1,)), ((), ())), preferred_element_type=F32)
    o_ref[row, :] = jnp.sum(jnp.where(seg, o / den, 0.0), axis=0, keepdims=True)


def _sample_attn(q, kn, vn, cache_k, cache_v, bias, bias_new):
    bd, lb = cache_k.shape[0], cache_k.shape[1]
    da = D_ATTN
    assert lb % SAMPLE_POS_CHUNK == 0

    def feature_major(c):
        return jnp.transpose(c, (0, 2, 3, 1)).reshape(bd, da, lb)

    win = pl.BlockSpec((None, da, lb), lambda i: (i, 0, 0))
    full = pl.BlockSpec((bd, da), lambda i: (0, 0))
    return pl.pallas_call(
        _sample_attn_kernel,
        out_shape=jax.ShapeDtypeStruct((bd, da), F32),
        grid=(bd,),
        in_specs=[full, full, full, win, win,
                  pl.BlockSpec(bias.shape, lambda i: (0, 0)), pl.BlockSpec(bias_new.shape, lambda i: (0, 0))],
        out_specs=full,
        compiler_params=_cparams(("arbitrary",), 2 * (2 * _nbytes((da, lb), F32) + 4 * _nbytes((bd, da), F32)
                                                      + _nbytes(bias.shape, F32))),
        name="sample_attn",
    )(q, kn, vn, feature_major(cache_k), feature_major(cache_v), bias, bias_new)


def _mix_tail(x_ref, a, conv, ga_ref, gc_ref, w_ref, o_ref):
    cat = jnp.concatenate([_rms(a, ga_ref[...]), _rms(conv, gc_ref[...])], axis=1).astype(BF16)
    o_ref[...] = x_ref[...] + jnp.dot(cat, w_ref[...], preferred_element_type=F32)


def _mix_prompt_kernel(x_ref, a_ref, u_ref, halo_ref, b_ref, cw_ref, ga_ref, gc_ref, w_ref, o_ref,
                       *, tiles_per_seq):
    tm = u_ref.shape[0]
    u = u_ref[...]
    seq_start = pl.program_id(0) % tiles_per_seq == 0
    halo = jnp.where(seq_start, 0.0, halo_ref[...])
    rid = lax.broadcasted_iota(jnp.int32, u.shape, 0)
    u1 = jnp.where(rid == 0, halo[7:8], pltpu.roll(u, 1, axis=0))
    u2 = jnp.where(rid == 0, halo[6:7], jnp.where(rid == 1, halo[7:8], pltpu.roll(u, 2, axis=0)))
    cy = cw_ref[0:1] * u2 + cw_ref[1:2] * u1 + cw_ref[2:3] * u
    a = jnp.concatenate([a_ref[g] for g in range(N_GROUPS)], axis=1)
    _mix_tail(x_ref, a, b_ref[...] * cy, ga_ref, gc_ref, w_ref, o_ref)


def _mix_sample_kernel(x_ref, a_ref, u_ref, buf_ref, b_ref, cw_ref, ga_ref, gc_ref, w_ref, o_ref):
    dc = u_ref.shape[1]
    cy = cw_ref[0:1] * buf_ref[:, :dc] + cw_ref[1:2] * buf_ref[:, dc:] + cw_ref[2:3] * u_ref[...]
    _mix_tail(x_ref, a_ref[...], b_ref[...] * cy, ga_ref, gc_ref, w_ref, o_ref)


def _mix_out_prompt(x, attn, u, bc, conv_w, ga, gc, w_out, seq, tm):
    m, dm = x.shape
    dc = u.shape[1]
    row = lambda i: (i, 0)
    fixed = lambda i: (0, 0)
    halo_blocks = tm // SUBLANES
    tile = _nbytes((tm, dc), F32)
    return pl.pallas_call(
        functools.partial(_mix_prompt_kernel, tiles_per_seq=seq // tm),
        out_shape=jax.ShapeDtypeStruct((m, dm), F32),
        grid=(m // tm,),
        in_specs=[pl.BlockSpec((tm, dm), row),
                  pl.BlockSpec((N_GROUPS, tm, LANES), lambda i: (0, i, 0)),
                  pl.BlockSpec((tm, dc), row),
                  pl.BlockSpec((SUBLANES, dc), lambda i: (jnp.maximum(i * halo_blocks - 1, 0), 0)),
                  pl.BlockSpec((tm, dc), row),
                  pl.BlockSpec((CONV_W, dc), fixed), pl.BlockSpec((1, D_ATTN), fixed),
                  pl.BlockSpec((1, dc), fixed), pl.BlockSpec(w_out.shape, fixed)],
        out_specs=pl.BlockSpec((tm, dm), row),
        compiler_params=_cparams(("arbitrary",), 2 * (7 * tile + _nbytes(w_out.shape, BF16))),
        name="mix_out_prompt",
    )(x, attn, u, u, bc, conv_w, ga, gc, w_out)


def _mix_out_sample(x, attn, u, buf, bc, conv_w, ga, gc, w_out):
    m, dm = x.shape
    dc = u.shape[1]
    full = lambda a: pl.BlockSpec(a.shape, lambda i: (0,) * a.ndim)
    args = (x, attn, u, buf, bc, conv_w, ga, gc, w_out)
    return pl.pallas_call(
        _mix_sample_kernel,
        out_shape=jax.ShapeDtypeStruct((m, dm), F32),
        grid=(1,),
        in_specs=[full(a) for a in args],
        out_specs=pl.BlockSpec((m, dm), lambda i: (0, 0)),
        compiler_params=_cparams(("arbitrary",), 2 * (8 * _nbytes((m, dc), F32) + _nbytes(w_out.shape, BF16))),
        name="mix_out_sample",
    )(*args)


def _ffn_kernel(x_ref, g_ref, wg_ref, wu_ref, wd_ref, o_ref, n_sc):
    f = pl.program_id(1)

    @pl.when(f == 0)
    def _():
        x = x_ref[...]
        n_sc[...] = _rms(x, g_ref[...]).astype(BF16)
        o_ref[...] = x

    n = n_sc[...]
    gate = jnp.dot(n, wg_ref[...], preferred_element_type=F32)
    up = jnp.dot(n, wu_ref[...], preferred_element_type=F32)
    act = (gate / (1.0 + jnp.exp(-gate)) * up).astype(BF16)
    o_ref[...] += jnp.dot(act, wd_ref[...], preferred_element_type=F32)


def _ffn(x, g, wg, wu, wd, tm, tf):
    m, dm = x.shape
    dff = wg.shape[1]
    row = lambda i, f: (i, 0)
    return pl.pallas_call(
        _ffn_kernel,
        out_shape=jax.ShapeDtypeStruct((m, dm), F32),
        grid=(m // tm, dff // tf),
        in_specs=[pl.BlockSpec((tm, dm), row), pl.BlockSpec((1, dm), lambda i, f: (0, 0)),
                  pl.BlockSpec((dm, tf), lambda i, f: (0, f)), pl.BlockSpec((dm, tf), lambda i, f: (0, f)),
                  pl.BlockSpec((tf, dm), lambda i, f: (f, 0))],
        out_specs=pl.BlockSpec((tm, dm), row),
        scratch_shapes=[pltpu.VMEM((tm, dm), BF16)],
        compiler_params=_cparams(("arbitrary", "arbitrary"),
                                 4 * _nbytes((tm, dm), F32) + _nbytes((tm, dm), BF16)
                                 + 2 * 3 * _nbytes((dm, tf), BF16) + 3 * _nbytes((tm, tf), F32)),
        name="ffn",
    )(x, g, wg, wu, wd)


def _ple_kernel(x_ref, p_ref, g_ref, wg_ref, wp_ref, o_ref):
    x = x_ref[...]
    z = jnp.dot(_rms(x, g_ref[...]).astype(BF16), wg_ref[...], preferred_element_type=F32)
    e = jnp.dot(p_ref[...].astype(BF16), wp_ref[...], preferred_element_type=F32)
    o_ref[...] = x + e / (1.0 + jnp.exp(-z))


def _ple(x, p, g, wg, wp, tm):
    m, dm = x.shape
    dp = p.shape[1]
    row = lambda i: (i, 0)
    fixed = lambda i: (0, 0)
    return pl.pallas_call(
        _ple_kernel,
        out_shape=jax.ShapeDtypeStruct((m, dm), F32),
        grid=(m // tm,),
        in_specs=[pl.BlockSpec((tm, dm), row), pl.BlockSpec((tm, dp), row), pl.BlockSpec((1, dm), fixed),
                  pl.BlockSpec(wg.shape, fixed), pl.BlockSpec(wp.shape, fixed)],
        out_specs=pl.BlockSpec((tm, dm), row),
        compiler_params=_cparams(("arbitrary",),
                                 2 * (2 * _nbytes((tm, dm), F32) + _nbytes((tm, dp), F32)
                                      + _nbytes(wg.shape, BF16) + _nbytes(wp.shape, BF16))
                                 + 2 * _nbytes((tm, dm), F32)),
        name="ple",
    )(x, p, g, wg, wp)


def _tiles(m):
    return dict(proj=min(m, 512), mix=min(m, 256), ffn=min(m, 512), ffn_cols=512, ple=min(m, 512))


def kernel(x_prompt, x_sample, p_prompt, p_sample, cache_k, cache_v, state_conv, rel_bias, g_mix, w_in,
           q_norm, k_norm, conv_w, g_attn_out, g_conv_out, w_out, g_ffn, w_gate, w_up, w_down, g_ple,
           w_ple_gate, w_ple_proj):
    depth = g_mix.shape[0]
    batch, seq, dm = x_prompt.shape
    bd, dec_seq, _ = x_sample.shape
    dc = conv_w.shape[2]
    assert depth == 1 and dec_seq == 1, "single layer, one new position per sample"
    assert seq % SUPER == 0 and dm == D_ATTN + dc and dc == D_ATTN
    assert w_gate.shape[2] % 512 == 0

    bias_p, bias_s, bias_s_new = _bias_tables(rel_bias, cache_k.shape[2])
    gmat = jnp.asarray(np.kron(np.eye(N_HEADS, dtype=np.float32),
                               np.full((HEAD_DIM, HEAD_DIM), 1.0 / HEAD_DIM, np.float32)), BF16)

    i = 0
    row2 = lambda a: a.reshape(1, -1).astype(F32)
    g_mix_i, g_ffn_i, g_ple_i = row2(g_mix[i]), row2(g_ffn[i]), row2(g_ple[i])
    gq = row2(jnp.tile(q_norm[i], N_HEADS) * SCALE)
    gk = row2(jnp.tile(k_norm[i], N_HEADS))
    ga, gc = row2(g_attn_out[i]), row2(g_conv_out[i])
    cw = conv_w[i].astype(F32)
    w_in_i, w_out_i = w_in[i].astype(BF16), w_out[i].astype(BF16)
    wg_i, wu_i, wd_i = w_gate[i].astype(BF16), w_up[i].astype(BF16), w_down[i].astype(BF16)
    wpg_i, wpp_i = w_ple_gate[i].astype(BF16), w_ple_proj[i].astype(BF16)

    mp = batch * seq
    tp = _tiles(mp)
    xp = x_prompt.reshape(mp, dm)
    q, k, v, kt, vt, u, bc = _in_proj(xp, g_mix_i, w_in_i, gq, gk, gmat, tp["proj"])
    attn = _prompt_attn(q, k, v, bias_p, batch, seq)
    h = _mix_out_prompt(xp, attn, u, bc, cw, ga, gc, w_out_i, seq, tp["mix"])
    h = _ffn(h, g_ffn_i, wg_i, wu_i, wd_i, tp["ffn"], tp["ffn_cols"])
    h = _ple(h, p_prompt[i].reshape(mp, -1), g_ple_i, wpg_i, wpp_i, tp["ple"])
    y_prompt = h.reshape(batch, seq, dm)
    kw = min(SUPER, seq)
    k_prompt = kt.reshape(batch, seq, N_HEADS, HEAD_DIM)[None, :, seq - kw:]
    v_prompt = vt.reshape(batch, seq, N_HEADS, HEAD_DIM)[None, :, seq - kw:]
    conv_prompt = u.reshape(batch, seq, dc)[None, :, seq - (CONV_W - 1):]

    ts = _tiles(bd)
    xs = x_sample.reshape(bd, dm)
    qs, _, _, kts, vts, us, bcs = _in_proj(xs, g_mix_i, w_in_i, gq, gk, gmat, ts["proj"])
    qs = jnp.concatenate([qs[g] for g in range(N_GROUPS)], axis=1)
    attn_s = _sample_attn(qs, kts, vts, cache_k[i], cache_v[i], bias_s, bias_s_new)
    buf = state_conv[i].astype(F32)
    hs = _mix_out_sample(xs, attn_s, us, buf.reshape(bd, (CONV_W - 1) * dc), bcs, cw, ga, gc, w_out_i)
    hs = _ffn(hs, g_ffn_i, wg_i, wu_i, wd_i, ts["ffn"], ts["ffn_cols"])
    hs = _ple(hs, p_sample[i].reshape(bd, -1), g_ple_i, wpg_i, wpp_i, ts["ple"])
    y_sample = hs.reshape(bd, dec_seq, dm)
    k_sample = kts.reshape(1, bd, dec_seq, N_HEADS, HEAD_DIM)
    v_sample = vts.reshape(1, bd, dec_seq, N_HEADS, HEAD_DIM)
    conv_sample = jnp.concatenate([buf[:, 1:], us[:, None, :]], axis=1)[None]

    return (y_prompt, y_sample, k_prompt, v_prompt, conv_prompt, k_sample, v_sample, conv_sample)
```

```python
import functools
import math

import numpy as np
import jax
import jax.numpy as jnp
from jax import lax
from jax.experimental import pallas as pl
from jax.experimental.pallas import tpu as pltpu

HEAD_DIM = 64
N_HEADS = 16
D_ATTN = N_HEADS * HEAD_DIM
CONV_W = 3
DIL_STEPS = 128
DILATIONS = (1, 4, 16)
N_BUCKETS = 32
MAX_EXACT = N_BUCKETS // 2
MAX_DIST = DIL_STEPS * max(DILATIONS)
EPS = 1e-6
SCALE = HEAD_DIM ** -0.5

LANES = 128
SUBLANES = 8
BF16_ROWS = 16
HEADS_PER_GROUP = LANES // HEAD_DIM
N_GROUPS = N_HEADS // HEADS_PER_GROUP
SUPER = DIL_STEPS * max(DILATIONS)
N_RES = max(DILATIONS)
MASK_VALUE = -1e30
MASK_BUCKET = N_BUCKETS
TABLE_ROWS = -(-(N_BUCKETS + 1) // BF16_ROWS) * BF16_ROWS
TABLE_CHUNK = 2048
SAMPLE_POS_CHUNK = 512
ATTN_UNROLL = 16
PROJ_COLS = 512
PROJ_SPLIT = D_ATTN // PROJ_COLS
MIX_SUB_ROWS = 256
PLE_SUB_ROWS = 512
VMEM_SLACK_BYTES = 8 * 1024 * 1024

F32 = jnp.float32
BF16 = jnp.bfloat16


def _cparams(sem, vmem_bytes):
    return pltpu.CompilerParams(dimension_semantics=sem,
                                vmem_limit_bytes=int(vmem_bytes + VMEM_SLACK_BYTES))


def _nbytes(shape, dtype):
    return int(np.prod(shape)) * jnp.dtype(dtype).itemsize


def _resident(shape, index_map):
    return pl.BlockSpec(shape, index_map, pipeline_mode=pl.Buffered(1))


def _t5_bucket(dist):
    n = np.asarray(dist, np.int32)
    nf = np.maximum(n, 1).astype(np.float32)
    large = MAX_EXACT + (np.log(nf / MAX_EXACT) / np.float32(math.log(MAX_DIST / MAX_EXACT))
                         * (N_BUCKETS - MAX_EXACT)).astype(np.int32)
    large = np.minimum(large, N_BUCKETS - 1)
    return np.where(n < MAX_EXACT, n, large).astype(np.int32)


def _stored_to_natural(d):
    runs = N_RES // d
    run = DIL_STEPS // runs
    j = np.arange(DIL_STEPS)
    return (j % run) * runs + j // run


def _prompt_bias_index():
    out = np.empty((len(DILATIONS), 2, DIL_STEPS, 2 * DIL_STEPS), np.int32)
    for di, d in enumerate(DILATIONS):
        nat = _stored_to_natural(d)
        qi = nat[:, None]
        kj = np.concatenate([nat, nat + DIL_STEPS])[None, :]
        steps = DIL_STEPS + qi - kj
        band = (steps >= 0) & (steps <= DIL_STEPS)
        bucket = _t5_bucket(d * np.clip(steps, 0, DIL_STEPS))
        out[di, 0] = np.where(band, bucket, MASK_BUCKET)
        out[di, 1] = np.where(band & (kj >= DIL_STEPS), bucket, MASK_BUCKET)
    return out


def _sample_bias_index(lb):
    back = lb - np.arange(lb)
    cached = np.stack([np.where((back % d == 0) & (back // d <= DIL_STEPS), _t5_bucket(back), MASK_BUCKET)
                       for d in DILATIONS])
    new = np.stack([_t5_bucket(d * np.zeros(1, np.int32)) for d in DILATIONS])
    return cached, new


def _bias_table_kernel(tab_ref, idx_ref, o_ref):
    t = tab_ref[...]
    hi = t.astype(BF16)
    r1 = t - hi.astype(F32)
    mid = r1.astype(BF16)
    lo = (r1 - mid.astype(F32)).astype(BF16)
    rows = lax.broadcasted_iota(jnp.int32, (TABLE_ROWS, TABLE_CHUNK), 0)

    def chunk(c, carry):
        sl = pl.ds(pl.multiple_of(c * TABLE_CHUNK, TABLE_CHUNK), TABLE_CHUNK)
        onehot = jnp.where(rows == idx_ref[:, sl], 1.0, 0.0).astype(BF16)
        acc = jnp.dot(hi, onehot, preferred_element_type=F32)
        acc = acc + jnp.dot(mid, onehot, preferred_element_type=F32)
        acc = acc + jnp.dot(lo, onehot, preferred_element_type=F32)
        o_ref[:, sl] = acc
        return carry

    lax.fori_loop(0, o_ref.shape[1] // TABLE_CHUNK, chunk, 0)


def _bias_tables(rel_bias, lb):
    pidx = _prompt_bias_index()
    cached, new = _sample_bias_index(lb)
    flat = np.concatenate([pidx.reshape(-1), cached.reshape(-1), new.reshape(-1)])
    n_pad = -(-(pidx.size + cached.size + LANES) // TABLE_CHUNK) * TABLE_CHUNK
    idx = np.full((1, n_pad), MASK_BUCKET, np.int32)
    idx[0, :flat.size] = flat
    tab = jnp.concatenate(
        [rel_bias.astype(F32).T,
         jnp.full((N_HEADS, 1), MASK_VALUE, F32),
         jnp.zeros((N_HEADS, TABLE_ROWS - N_BUCKETS - 1), F32)], axis=1)
    out = pl.pallas_call(
        _bias_table_kernel,
        out_shape=jax.ShapeDtypeStruct((N_HEADS, n_pad), F32),
        grid=(1,),
        in_specs=[pl.BlockSpec((N_HEADS, TABLE_ROWS), lambda i: (0, 0)),
                  pl.BlockSpec((1, n_pad), lambda i: (0, 0))],
        out_specs=pl.BlockSpec((N_HEADS, n_pad), lambda i: (0, 0)),
        compiler_params=_cparams(("arbitrary",), 2 * (_nbytes((N_HEADS, n_pad), F32)
                                                      + _nbytes((SUBLANES, n_pad), jnp.int32))),
        name="bias_table",
    )(tab, jnp.asarray(idx))
    n_p = pidx.size
    prompt = out[:, :n_p].reshape(N_HEADS, len(DILATIONS), 2, DIL_STEPS, 2 * DIL_STEPS)
    samp = out[:, n_p:n_p + cached.size]
    samp_new = out[:, n_p + cached.size:n_p + cached.size + LANES]
    return prompt, samp, samp_new


def _rms(x, g):
    return x * lax.rsqrt(jnp.mean(x * x, axis=-1, keepdims=True) + EPS) * g


def _head_rms(p, g, gmat):
    ms = jnp.dot((p * p).astype(BF16), gmat, preferred_element_type=F32)
    return p * lax.rsqrt(ms + EPS) * g


def _cat_groups(ref, rows):
    return jnp.concatenate([ref[g, rows, :] for g in range(N_GROUPS)], axis=1)


def _sub_tiles(tm, sub):
    sub = min(tm, sub)
    return [pl.ds(r, sub) for r in range(0, tm, sub)]


def _proj_kernel(*refs, normed, t_plane, tiles_per_seq, first_kept):
    if normed:
        x_ref, g_ref, w_ref, gain_ref, gmat_ref, o_ref, t_ref, n_sc = refs
    else:
        x_ref, g_ref, w_ref, o_ref, t_ref, n_sc = refs
    i, j = pl.program_id(0), pl.program_id(1)

    @pl.when(j == 0)
    def _():
        n_sc[...] = _rms(x_ref[...], g_ref[...]).astype(BF16)

    y = jnp.dot(n_sc[...], w_ref[...], preferred_element_type=F32)
    if normed:
        y = _head_rms(y, gain_ref[...], gmat_ref[...])
    for g in range(PROJ_COLS // LANES):
        o_ref[g] = y[:, g * LANES:(g + 1) * LANES]

    @pl.when(jnp.logical_and(j // PROJ_SPLIT == t_plane, i % tiles_per_seq >= first_kept))
    def _():
        t_ref[...] = y.T


def _proj(x, g_mix, w_in, plane0, n_planes, t_plane, seq, kw, tm, gains=None, gmat=None):
    m, dm = x.shape
    da = D_ATTN
    assert seq % tm == 0 and kw % tm == 0 and m % seq == 0
    tps, first_kept, batch = seq // tm, (seq - kw) // tm, m // seq
    normed = gains is not None
    grp = PROJ_COLS // LANES
    in_specs = [pl.BlockSpec((tm, dm), lambda i, j: (i, 0)), pl.BlockSpec((1, dm), lambda i, j: (0, 0)),
                pl.BlockSpec((dm, PROJ_COLS), lambda i, j: (0, plane0 * PROJ_SPLIT + j))]
    args = [x, g_mix, w_in]
    if normed:
        in_specs += [pl.BlockSpec((None, 1, PROJ_COLS), lambda i, j: (j, 0, 0)),
                     _resident((PROJ_COLS, PROJ_COLS), lambda i, j: (0, 0))]
        args += [gains.reshape(n_planes * PROJ_SPLIT, 1, PROJ_COLS), gmat]

    def t_index(i, j):
        kept = i % tps >= first_kept
        half = jnp.where(kept, jnp.clip(j - t_plane * PROJ_SPLIT, 0, PROJ_SPLIT - 1), 0)
        return (i // tps, half, jnp.maximum(i % tps - first_kept, 0))

    out_specs = (pl.BlockSpec((None, grp, tm, LANES), lambda i, j: (j // PROJ_SPLIT, j % PROJ_SPLIT, i, 0)),
                 pl.BlockSpec((None, PROJ_COLS, tm), t_index))
    vmem = (2 * (_nbytes((tm, dm), F32) + _nbytes((dm, PROJ_COLS), BF16) + 2 * _nbytes((tm, PROJ_COLS), F32))
            + _nbytes((tm, dm), BF16) + 6 * _nbytes((tm, PROJ_COLS), F32))
    return pl.pallas_call(
        functools.partial(_proj_kernel, normed=normed, t_plane=t_plane, tiles_per_seq=tps,
                          first_kept=first_kept),
        out_shape=(jax.ShapeDtypeStruct((n_planes, N_GROUPS, m, LANES), F32),
                   jax.ShapeDtypeStruct((batch, da, kw), F32)),
        grid=(m // tm, n_planes * PROJ_SPLIT),
        in_specs=in_specs,
        out_specs=out_specs,
        scratch_shapes=[pltpu.VMEM((tm, dm), BF16)],
        compiler_params=_cparams(("arbitrary", "arbitrary"), vmem),
        name="in_proj_qk" if normed else "in_proj_vhbc",
    )(*args)


def _gather_rows(ref, starts, run):
    parts = [ref[pl.ds(s, run), :] for s in starts]
    return parts[0] if len(parts) == 1 else jnp.concatenate(parts, axis=0)


def _scatter_rows(ref, starts, run, val):
    for i, s in enumerate(starts):
        ref[pl.ds(s, run), :] = val[i * run:(i + 1) * run]


def _prompt_attn_kernel(q_ref, k_ref, v_ref, bias_ref, o_ref, qp, kp, vp, acc, m_sc, l_sc):
    sb = pl.program_id(2)
    slot = sb % 2
    pslot = 1 - slot
    kcur, vcur = kp.at[slot], vp.at[slot]
    kprev, vprev = kp.at[pslot], vp.at[pslot]

    for r in range(N_RES):
        rows = pl.ds(r * DIL_STEPS, DIL_STEPS)
        src = pl.ds(r, DIL_STEPS, stride=N_RES)
        qp[rows, :] = q_ref[src, :]
        kcur[rows, :] = k_ref[src, :]
        vcur[rows, :] = v_ref[src, :]

    @pl.when(sb == 0)
    def _():
        kprev[...] = jnp.zeros(kprev.shape, F32)
        vprev[...] = jnp.zeros(vprev.shape, F32)

    head0 = lax.broadcasted_iota(jnp.int32, (DIL_STEPS, LANES), 1) < HEAD_DIM

    def block(di, starts, run, prev_ref_k, prev_ref_v, prev_starts, first, mode, out_rows=None):
        qb = _gather_rows(qp, starts, run)
        kb = jnp.concatenate([_gather_rows(prev_ref_k, prev_starts, run),
                              _gather_rows(kcur, starts, run)], axis=0).astype(BF16)
        vb = jnp.concatenate([_gather_rows(prev_ref_v, prev_starts, run),
                              _gather_rows(vcur, starts, run)], axis=0).astype(BF16)
        ms, ls, os_ = [], [], []
        for h in range(HEADS_PER_GROUP):
            keep = head0 if h == 0 else jnp.logical_not(head0)
            qh = jnp.where(keep, qb, 0.0).astype(BF16)
            s = lax.dot_general(qh, kb, (((1,), (1,)), ((), ())), preferred_element_type=F32)
            s = s + bias_ref[h, di, first]
            mh = jnp.max(s, axis=1, keepdims=True)
            p = jnp.exp(s - mh)
            ls.append(jnp.sum(p, axis=1, keepdims=True))
            os_.append(jnp.dot(p.astype(BF16), vb, preferred_element_type=F32))
            ms.append(mh)
        m_c = jnp.where(head0, ms[0], ms[1])
        l_c = jnp.where(head0, ls[0], ls[1])
        o_c = jnp.where(head0, os_[0], os_[1])
        if mode == "init":
            _scatter_rows(m_sc, starts, run, m_c)
            _scatter_rows(l_sc, starts, run, l_c)
            _scatter_rows(acc, starts, run, o_c)
            return
        m_o = _gather_rows(m_sc, starts, run)
        m_n = jnp.maximum(m_o, m_c)
        a_o = jnp.exp(m_o - m_n)
        a_c = jnp.exp(m_c - m_n)
        l_n = _gather_rows(l_sc, starts, run) * a_o + l_c * a_c
        o_n = _gather_rows(acc, starts, run) * a_o + o_c * a_c
        if mode == "merge":
            _scatter_rows(m_sc, starts, run, m_n)
            _scatter_rows(l_sc, starts, run, l_n)
            _scatter_rows(acc, starts, run, o_n)
        else:
            o_ref[out_rows, :] = o_n / l_n

    def run_dilation(di, mode):
        d = DILATIONS[di]
        runs = N_RES // d
        run = DIL_STEPS // runs
        nblk = SUPER // d // DIL_STEPS

        def body(it, carry):
            c = it // nblk
            n = it % nblk
            starts = [pl.multiple_of((d * b + c) * DIL_STEPS + run * n, SUBLANES) for b in range(runs)]
            pn = jnp.where(n > 0, n - 1, nblk - 1)
            prev_starts = [pl.multiple_of((d * b + c) * DIL_STEPS + run * pn, SUBLANES)
                           for b in range(runs)]
            pslot_n = jnp.where(n > 0, slot, pslot)
            first = jnp.logical_and(sb == 0, n == 0).astype(jnp.int32)
            out_rows = pl.ds(it, DIL_STEPS, stride=N_RES) if mode == "final" else None
            block(di, starts, run, kp.at[pslot_n], vp.at[pslot_n], prev_starts, first, mode, out_rows)
            return carry

        lax.fori_loop(0, d * nblk, body, 0, unroll=ATTN_UNROLL)

    run_dilation(0, "init")
    run_dilation(1, "merge")
    run_dilation(2, "final")


def _prompt_attn(qk, vhbc, bias, batch, seq):
    nsb = seq // SUPER
    m = batch * seq

    def plane(p):
        return pl.BlockSpec((None, None, SUPER, LANES), lambda g, b, s: (p, g, b * nsb + s, 0))

    bias_spec = pl.BlockSpec((HEADS_PER_GROUP,) + bias.shape[1:], lambda g, b, s: (g, 0, 0, 0, 0))
    blk_bytes = _nbytes((SUPER, LANES), F32)
    return pl.pallas_call(
        _prompt_attn_kernel,
        out_shape=jax.ShapeDtypeStruct((N_GROUPS, m, LANES), F32),
        grid=(N_GROUPS, batch, nsb),
        in_specs=[plane(0), plane(1), plane(0), bias_spec],
        out_specs=pl.BlockSpec((None, SUPER, LANES), lambda g, b, s: (g, b * nsb + s, 0)),
        scratch_shapes=[pltpu.VMEM((SUPER, LANES), F32),
                        pltpu.VMEM((2, SUPER, LANES), F32), pltpu.VMEM((2, SUPER, LANES), F32),
                        pltpu.VMEM((SUPER, LANES), F32), pltpu.VMEM((SUPER, LANES), F32),
                        pltpu.VMEM((SUPER, LANES), F32)],
        compiler_params=_cparams(("arbitrary", "arbitrary", "arbitrary"),
                                 (2 * 4 + 8) * blk_bytes
                                 + 2 * _nbytes((HEADS_PER_GROUP,) + bias.shape[1:], F32)),
        name="prompt_attn",
    )(qk, qk, vhbc, bias)


def _sample_attn_kernel(q_ref, kn_ref, vn_ref, kt_ref, vt_ref, bias_ref, bnew_ref, o_ref):
    lb = kt_ref.shape[1]
    nd = len(DILATIONS)
    row = pl.ds(pl.program_id(0), 1)
    seg = (lax.broadcasted_iota(jnp.int32, (N_HEADS, D_ATTN), 1) // HEAD_DIM
           == lax.broadcasted_iota(jnp.int32, (N_HEADS, D_ATTN), 0))
    qe = jnp.where(seg, _cat_groups(q_ref, row), 0.0).astype(BF16)
    kn = _cat_groups(kn_ref, row).astype(BF16).astype(F32)
    s_new = jnp.sum(qe.astype(F32) * kn, axis=1, keepdims=True)
    sn = [s_new + bnew_ref[:, di:di + 1] for di in range(nd)]
    starts = range(0, lb, SAMPLE_POS_CHUNK)
    chunks = [pl.ds(c, SAMPLE_POS_CHUNK) for c in starts]
    sc = []
    for c, ch in zip(starts, chunks):
        s = jnp.dot(qe, kt_ref[:, ch].astype(BF16), preferred_element_type=F32)
        sc.append([s + bias_ref[:, pl.ds(di * lb + c, SAMPLE_POS_CHUNK)] for di in range(nd)])
    m = functools.reduce(jnp.maximum, [jnp.max(x, axis=1, keepdims=True) for row_ in sc for x in row_] + sn)
    den = functools.reduce(jnp.add, [jnp.exp(x - m) for x in sn])
    o = den * _cat_groups(vn_ref, row)
    for ch, sd in zip(chunks, sc):
        p = functools.reduce(jnp.add, [jnp.exp(x - m) for x in sd])
        den = den + jnp.sum(p, axis=1, keepdims=True)
        o = o + lax.dot_general(p.astype(BF16), vt_ref[:, ch].astype(BF16),
                                (((1,), (1,)), ((), ())), preferred_element_type=F32)
    out = jnp.sum(jnp.where(seg, o / den, 0.0), axis=0, keepdims=True)
    for g in range(N_GROUPS):
        o_ref[g, row, :] = out[:, g * LANES:(g + 1) * LANES]


def _sample_attn(qk, vhbc, cache_k, cache_v, bias, bias_new):
    bd, lb = cache_k.shape[0], cache_k.shape[1]
    da = D_ATTN
    assert lb % SAMPLE_POS_CHUNK == 0

    def feature_major(c):
        return jnp.transpose(c, (0, 2, 3, 1)).reshape(bd, da, lb)

    def plane(p):
        return pl.BlockSpec((None, N_GROUPS, bd, LANES), lambda i: (p, 0, 0, 0))

    win = pl.BlockSpec((None, da, lb), lambda i: (i, 0, 0))
    return pl.pallas_call(
        _sample_attn_kernel,
        out_shape=jax.ShapeDtypeStruct((N_GROUPS, bd, LANES), F32),
        grid=(bd,),
        in_specs=[plane(0), plane(1), plane(0), win, win,
                  pl.BlockSpec(bias.shape, lambda i: (0, 0)), pl.BlockSpec(bias_new.shape, lambda i: (0, 0))],
        out_specs=pl.BlockSpec((N_GROUPS, bd, LANES), lambda i: (0, 0, 0)),
        compiler_params=_cparams(("arbitrary",), 2 * (2 * _nbytes((da, lb), F32) + 4 * _nbytes((bd, da), F32)
                                                      + _nbytes(bias.shape, F32))),
        name="sample_attn",
    )(qk, qk, vhbc, feature_major(cache_k), feature_major(cache_v), bias, bias_new)


def _mix_tail(x, a, conv, ga_ref, gc_ref, w_ref):
    cat = jnp.concatenate([_rms(a, ga_ref[...]), _rms(conv, gc_ref[...])], axis=1).astype(BF16)
    return x + jnp.dot(cat, w_ref[...], preferred_element_type=F32)


def _mix_prompt_kernel(x_ref, a_ref, h_ref, b_ref, c_ref, hh_ref, ch_ref, cw_ref, ga_ref, gc_ref, w_ref,
                       o_ref, ut_ref, *, tiles_per_seq):
    tm = x_ref.shape[0]
    seq_start = pl.program_id(0) % tiles_per_seq == 0
    all8 = pl.ds(0, SUBLANES)
    halo = jnp.where(seq_start, 0.0, _cat_groups(hh_ref, all8) * _cat_groups(ch_ref, all8))
    for rows in _sub_tiles(tm, MIX_SUB_ROWS):
        u = _cat_groups(h_ref, rows) * _cat_groups(c_ref, rows)
        rid = lax.broadcasted_iota(jnp.int32, u.shape, 0)
        u1 = jnp.where(rid == 0, halo[7:8], pltpu.roll(u, 1, axis=0))
        u2 = jnp.where(rid == 0, halo[6:7], jnp.where(rid == 1, halo[7:8], pltpu.roll(u, 2, axis=0)))
        cy = cw_ref[0:1] * u2 + cw_ref[1:2] * u1 + cw_ref[2:3] * u
        conv = _cat_groups(b_ref, rows) * cy
        o_ref[rows, :] = _mix_tail(x_ref[rows, :], _cat_groups(a_ref, rows), conv, ga_ref, gc_ref, w_ref)
        halo = u[u.shape[0] - SUBLANES:]
    ut_ref[...] = halo


def _mix_sample_kernel(x_ref, a_ref, h_ref, b_ref, c_ref, buf_ref, cw_ref, ga_ref, gc_ref, w_ref, o_ref, u_ref):
    every = pl.ds(0, x_ref.shape[0])
    u = _cat_groups(h_ref, every) * _cat_groups(c_ref, every)
    dc = u.shape[1]
    cy = cw_ref[0:1] * buf_ref[:, :dc] + cw_ref[1:2] * buf_ref[:, dc:] + cw_ref[2:3] * u
    conv = _cat_groups(b_ref, every) * cy
    o_ref[...] = _mix_tail(x_ref[...], _cat_groups(a_ref, every), conv, ga_ref, gc_ref, w_ref)
    u_ref[...] = u


def _mix_out_prompt(x, attn, vhbc, conv_w, ga, gc, w_out, seq, tm):
    m, dm = x.shape
    dc = conv_w.shape[1]
    fixed = lambda i: (0, 0)
    halo_blocks = tm // SUBLANES

    def plane(p):
        return pl.BlockSpec((None, N_GROUPS, tm, LANES), lambda i: (p, 0, i, 0))

    def halo(p):
        return pl.BlockSpec((None, N_GROUPS, SUBLANES, LANES),
                            lambda i: (p, 0, jnp.maximum(i * halo_blocks - 1, 0), 0))

    tile = _nbytes((tm, dc), F32)
    return pl.pallas_call(
        functools.partial(_mix_prompt_kernel, tiles_per_seq=seq // tm),
        out_shape=(jax.ShapeDtypeStruct((m, dm), F32), jax.ShapeDtypeStruct((m // tm, SUBLANES, dc), F32)),
        grid=(m // tm,),
        in_specs=[pl.BlockSpec((tm, dm), lambda i: (i, 0)),
                  pl.BlockSpec((N_GROUPS, tm, LANES), lambda i: (0, i, 0)),
                  plane(1), plane(2), plane(3), halo(1), halo(3),
                  pl.BlockSpec((CONV_W, dc), fixed), pl.BlockSpec((1, D_ATTN), fixed),
                  pl.BlockSpec((1, dc), fixed), _resident(w_out.shape, fixed)],
        out_specs=(pl.BlockSpec((tm, dm), lambda i: (i, 0)),
                   pl.BlockSpec((None, SUBLANES, dc), lambda i: (i, 0, 0))),
        compiler_params=_cparams(("arbitrary",), 2 * 8 * tile + _nbytes(w_out.shape, BF16)
                                 + 4 * _nbytes((min(tm, MIX_SUB_ROWS), dm), F32)),
        name="mix_out_prompt",
    )(x, attn, vhbc, vhbc, vhbc, vhbc, vhbc, conv_w, ga, gc, w_out)


def _mix_out_sample(x, attn, vhbc, buf, conv_w, ga, gc, w_out):
    m, dm = x.shape
    dc = conv_w.shape[1]
    full = lambda a: pl.BlockSpec(a.shape, lambda i: (0,) * a.ndim)

    def plane(p):
        return pl.BlockSpec((None, N_GROUPS, m, LANES), lambda i: (p, 0, 0, 0))

    return pl.pallas_call(
        _mix_sample_kernel,
        out_shape=(jax.ShapeDtypeStruct((m, dm), F32), jax.ShapeDtypeStruct((m, dc), F32)),
        grid=(1,),
        in_specs=[full(x), full(attn), plane(1), plane(2), plane(3), full(buf), full(conv_w), full(ga),
                  full(gc), full(w_out)],
        out_specs=(pl.BlockSpec((m, dm), lambda i: (0, 0)), pl.BlockSpec((m, dc), lambda i: (0, 0))),
        compiler_params=_cparams(("arbitrary",), 2 * (10 * _nbytes((m, dc), F32) + _nbytes(w_out.shape, BF16))),
        name="mix_out_sample",
    )(x, attn, vhbc, vhbc, vhbc, buf, conv_w, ga, gc, w_out)


def _ffn_kernel(x_ref, g_ref, wg_ref, wu_ref, wd_ref, o_ref, n_sc):
    f = pl.program_id(1)

    @pl.when(f == 0)
    def _():
        x = x_ref[...]
        n_sc[...] = _rms(x, g_ref[...]).astype(BF16)
        o_ref[...] = x

    n = n_sc[...]
    gate = jnp.dot(n, wg_ref[...], preferred_element_type=F32)
    up = jnp.dot(n, wu_ref[...], preferred_element_type=F32)
    act = (gate / (1.0 + jnp.exp(-gate)) * up).astype(BF16)
    o_ref[...] += jnp.dot(act, wd_ref[...], preferred_element_type=F32)


def _ffn(x, g, wg, wu, wd, tm, tf):
    m, dm = x.shape
    dff = wg.shape[1]
    row = lambda i, f: (i, 0)
    return pl.pallas_call(
        _ffn_kernel,
        out_shape=jax.ShapeDtypeStruct((m, dm), F32),
        grid=(m // tm, dff // tf),
        in_specs=[pl.BlockSpec((tm, dm), row), pl.BlockSpec((1, dm), lambda i, f: (0, 0)),
                  pl.BlockSpec((dm, tf), lambda i, f: (0, f)), pl.BlockSpec((dm, tf), lambda i, f: (0, f)),
                  pl.BlockSpec((tf, dm), lambda i, f: (f, 0))],
        out_specs=pl.BlockSpec((tm, dm), row),
        scratch_shapes=[pltpu.VMEM((tm, dm), BF16)],
        compiler_params=_cparams(("arbitrary", "arbitrary"),
                                 4 * _nbytes((tm, dm), F32) + _nbytes((tm, dm), BF16)
                                 + 2 * 3 * _nbytes((dm, tf), BF16) + 3 * _nbytes((tm, tf), F32)),
        name="ffn",
    )(x, g, wg, wu, wd)


def _ple_kernel(x_ref, p_ref, g_ref, wg_ref, wp_ref, o_ref):
    for rows in _sub_tiles(x_ref.shape[0], PLE_SUB_ROWS):
        x = x_ref[rows, :]
        z = jnp.dot(_rms(x, g_ref[...]).astype(BF16), wg_ref[...], preferred_element_type=F32)
        e = jnp.dot(p_ref[rows, :].astype(BF16), wp_ref[...], preferred_element_type=F32)
        o_ref[rows, :] = x + e / (1.0 + jnp.exp(-z))


def _ple(x, p, g, wg, wp, tm):
    m, dm = x.shape
    dp = p.shape[1]
    row = lambda i: (i, 0)
    fixed = lambda i: (0, 0)
    return pl.pallas_call(
        _ple_kernel,
        out_shape=jax.ShapeDtypeStruct((m, dm), F32),
        grid=(m // tm,),
        in_specs=[pl.BlockSpec((tm, dm), row), pl.BlockSpec((tm, dp), row), pl.BlockSpec((1, dm), fixed),
                  _resident(wg.shape, fixed), _resident(wp.shape, fixed)],
        out_specs=pl.BlockSpec((tm, dm), row),
        compiler_params=_cparams(("arbitrary",),
                                 2 * (2 * _nbytes((tm, dm), F32) + _nbytes((tm, dp), F32))
                                 + _nbytes(wg.shape, BF16) + _nbytes(wp.shape, BF16)
                                 + 6 * _nbytes((min(tm, PLE_SUB_ROWS), dm), F32)),
        name="ple",
    )(x, p, g, wg, wp)


def _tiles(m):
    return dict(proj=min(m, 1024), mix=min(m, 512), ffn=min(m, 512), ffn_cols=512, ple=min(m, 1024))


def _window_rows(t, batch, kw):
    return jnp.transpose(t.reshape(batch, N_HEADS, HEAD_DIM, kw), (0, 3, 1, 2))


def kernel(x_prompt, x_sample, p_prompt, p_sample, cache_k, cache_v, state_conv, rel_bias, g_mix, w_in,
           q_norm, k_norm, conv_w, g_attn_out, g_conv_out, w_out, g_ffn, w_gate, w_up, w_down, g_ple,
           w_ple_gate, w_ple_proj):
    depth = g_mix.shape[0]
    batch, seq, dm = x_prompt.shape
    bd, dec_seq, _ = x_sample.shape
    dc = conv_w.shape[2]
    assert depth == 1 and dec_seq == 1, "single layer, one new position per sample"
    assert seq % SUPER == 0 and dm == D_ATTN + dc and dc == D_ATTN
    assert w_gate.shape[2] % 512 == 0

    bias_p, bias_s, bias_s_new = _bias_tables(rel_bias, cache_k.shape[2])
    gmat = jnp.asarray(np.kron(np.eye(PROJ_COLS // HEAD_DIM, dtype=np.float32),
                               np.full((HEAD_DIM, HEAD_DIM), 1.0 / HEAD_DIM, np.float32)), BF16)

    i = 0
    row2 = lambda a: a.reshape(1, -1).astype(F32)
    g_mix_i, g_ffn_i, g_ple_i = row2(g_mix[i]), row2(g_ffn[i]), row2(g_ple[i])
    qk_gains = jnp.stack([row2(jnp.tile(q_norm[i], N_HEADS) * SCALE), row2(jnp.tile(k_norm[i], N_HEADS))])
    ga, gc = row2(g_attn_out[i]), row2(g_conv_out[i])
    cw = conv_w[i].astype(F32)
    w_in_i, w_out_i = w_in[i].astype(BF16), w_out[i].astype(BF16)
    wg_i, wu_i, wd_i = w_gate[i].astype(BF16), w_up[i].astype(BF16), w_down[i].astype(BF16)
    wpg_i, wpp_i = w_ple_gate[i].astype(BF16), w_ple_proj[i].astype(BF16)

    mp = batch * seq
    tp = _tiles(mp)
    kw = min(SUPER, seq)
    xp = x_prompt.reshape(mp, dm)
    qk, kt = _proj(xp, g_mix_i, w_in_i, 0, 2, 1, seq, kw, tp["proj"], qk_gains, gmat)
    vhbc, vt = _proj(xp, g_mix_i, w_in_i, 2, 4, 0, seq, kw, tp["proj"])
    attn = _prompt_attn(qk, vhbc, bias_p, batch, seq)
    h, u_tail = _mix_out_prompt(xp, attn, vhbc, cw, ga, gc, w_out_i, seq, tp["mix"])
    h = _ffn(h, g_ffn_i, wg_i, wu_i, wd_i, tp["ffn"], tp["ffn_cols"])
    h = _ple(h, p_prompt[i].reshape(mp, -1), g_ple_i, wpg_i, wpp_i, tp["ple"])
    y_prompt = h.reshape(batch, seq, dm)
    k_prompt = _window_rows(kt, batch, kw)[None]
    v_prompt = _window_rows(vt, batch, kw)[None]
    tiles_per_seq = seq // tp["mix"]
    conv_prompt = u_tail.reshape(batch, tiles_per_seq, SUBLANES, dc)[None, :, -1, SUBLANES - (CONV_W - 1):]

    ts = _tiles(bd)
    xs = x_sample.reshape(bd, dm)
    qks, kts = _proj(xs, g_mix_i, w_in_i, 0, 2, 1, bd, bd, ts["proj"], qk_gains, gmat)
    vhbcs, vts = _proj(xs, g_mix_i, w_in_i, 2, 4, 0, bd, bd, ts["proj"])
    attn_s = _sample_attn(qks, vhbcs, cache_k[i], cache_v[i], bias_s, bias_s_new)
    buf = state_conv[i].astype(F32)
    hs, us = _mix_out_sample(xs, attn_s, vhbcs, buf.reshape(bd, (CONV_W - 1) * dc), cw, ga, gc, w_out_i)
    hs = _ffn(hs, g_ffn_i, wg_i, wu_i, wd_i, ts["ffn"], ts["ffn_cols"])
    hs = _ple(hs, p_sample[i].reshape(bd, -1), g_ple_i, wpg_i, wpp_i, ts["ple"])
    y_sample = hs.reshape(bd, dec_seq, dm)
    k_sample = _window_rows(kts, 1, bd).reshape(1, bd, dec_seq, N_HEADS, HEAD_DIM)
    v_sample = _window_rows(vts, 1, bd).reshape(1, bd, dec_seq, N_HEADS, HEAD_DIM)
    conv_sample = jnp.concatenate([buf[:, 1:], us[:, None, :]], axis=1)[None]

    return (y_prompt, y_sample, k_prompt, v_prompt, conv_prompt, k_sample, v_sample, conv_sample)
```

```python
import functools
import math

import numpy as np
import jax
import jax.numpy as jnp
from jax import lax
from jax.experimental import pallas as pl
from jax.experimental.pallas import tpu as pltpu

HEAD_DIM = 64
N_HEADS = 16
D_ATTN = N_HEADS * HEAD_DIM
CONV_W = 3
DIL_STEPS = 128
DILATIONS = (1, 4, 16)
N_BUCKETS = 32
MAX_EXACT = N_BUCKETS // 2
MAX_DIST = DIL_STEPS * max(DILATIONS)
EPS = 1e-6
SCALE = HEAD_DIM ** -0.5

LANES = 128
SUBLANES = 8
BF16_ROWS = 16
HEADS_PER_GROUP = LANES // HEAD_DIM
N_GROUPS = N_HEADS // HEADS_PER_GROUP
SUPER = DIL_STEPS * max(DILATIONS)
N_RES = max(DILATIONS)
MASK_VALUE = -1e30
MASK_BUCKET = N_BUCKETS
TABLE_ROWS = -(-(N_BUCKETS + 1) // BF16_ROWS) * BF16_ROWS
TABLE_CHUNK = 2048
SAMPLE_POS_CHUNK = 512
ATTN_UNROLL = 16
PROJ_COLS = 512
PROJ_SPLIT = D_ATTN // PROJ_COLS
MIX_SUB_ROWS = 256
PLE_SUB_ROWS = 512
VMEM_SLACK_BYTES = 8 * 1024 * 1024

F32 = jnp.float32
BF16 = jnp.bfloat16


def _cparams(sem, vmem_bytes):
    return pltpu.CompilerParams(dimension_semantics=sem,
                                vmem_limit_bytes=int(vmem_bytes + VMEM_SLACK_BYTES))


def _nbytes(shape, dtype):
    return int(np.prod(shape)) * jnp.dtype(dtype).itemsize


def _resident(shape, index_map):
    return pl.BlockSpec(shape, index_map, pipeline_mode=pl.Buffered(1))


def _t5_bucket(dist):
    n = np.asarray(dist, np.int32)
    nf = np.maximum(n, 1).astype(np.float32)
    large = MAX_EXACT + (np.log(nf / MAX_EXACT) / np.float32(math.log(MAX_DIST / MAX_EXACT))
                         * (N_BUCKETS - MAX_EXACT)).astype(np.int32)
    large = np.minimum(large, N_BUCKETS - 1)
    return np.where(n < MAX_EXACT, n, large).astype(np.int32)


def _stored_to_natural(d):
    runs = N_RES // d
    run = DIL_STEPS // runs
    j = np.arange(DIL_STEPS)
    return (j % run) * runs + j // run


def _prompt_bias_index():
    out = np.empty((len(DILATIONS), 2, DIL_STEPS, 2 * DIL_STEPS), np.int32)
    for di, d in enumerate(DILATIONS):
        nat = _stored_to_natural(d)
        qi = nat[:, None]
        kj = np.concatenate([nat, nat + DIL_STEPS])[None, :]
        steps = DIL_STEPS + qi - kj
        band = (steps >= 0) & (steps <= DIL_STEPS)
        bucket = _t5_bucket(d * np.clip(steps, 0, DIL_STEPS))
        out[di, 0] = np.where(band, bucket, MASK_BUCKET)
        out[di, 1] = np.where(band & (kj >= DIL_STEPS), bucket, MASK_BUCKET)
    return out


def _sample_bias_index(lb):
    back = lb - np.arange(lb)
    cached = np.stack([np.where((back % d == 0) & (back // d <= DIL_STEPS), _t5_bucket(back), MASK_BUCKET)
                       for d in DILATIONS])
    new = np.stack([_t5_bucket(d * np.zeros(1, np.int32)) for d in DILATIONS])
    return cached, new


def _bias_table_kernel(tab_ref, idx_ref, o_ref):
    t = tab_ref[...]
    hi = t.astype(BF16)
    r1 = t - hi.astype(F32)
    mid = r1.astype(BF16)
    lo = (r1 - mid.astype(F32)).astype(BF16)
    rows = lax.broadcasted_iota(jnp.int32, (TABLE_ROWS, TABLE_CHUNK), 0)

    def chunk(c, carry):
        sl = pl.ds(pl.multiple_of(c * TABLE_CHUNK, TABLE_CHUNK), TABLE_CHUNK)
        onehot = jnp.where(rows == idx_ref[:, sl], 1.0, 0.0).astype(BF16)
        acc = jnp.dot(hi, onehot, preferred_element_type=F32)
        acc = acc + jnp.dot(mid, onehot, preferred_element_type=F32)
        acc = acc + jnp.dot(lo, onehot, preferred_element_type=F32)
        o_ref[:, sl] = acc
        return carry

    lax.fori_loop(0, o_ref.shape[1] // TABLE_CHUNK, chunk, 0)


def _bias_tables(rel_bias, lb):
    pidx = _prompt_bias_index()
    cached, new = _sample_bias_index(lb)
    flat = np.concatenate([pidx.reshape(-1), cached.reshape(-1), new.reshape(-1)])
    n_pad = -(-(pidx.size + cached.size + LANES) // TABLE_CHUNK) * TABLE_CHUNK
    idx = np.full((1, n_pad), MASK_BUCKET, np.int32)
    idx[0, :flat.size] = flat
    tab = jnp.concatenate(
        [rel_bias.astype(F32).T,
         jnp.full((N_HEADS, 1), MASK_VALUE, F32),
         jnp.zeros((N_HEADS, TABLE_ROWS - N_BUCKETS - 1), F32)], axis=1)
    out = pl.pallas_call(
        _bias_table_kernel,
        out_shape=jax.ShapeDtypeStruct((N_HEADS, n_pad), F32),
        grid=(1,),
        in_specs=[pl.BlockSpec((N_HEADS, TABLE_ROWS), lambda i: (0, 0)),
                  pl.BlockSpec((1, n_pad), lambda i: (0, 0))],
        out_specs=pl.BlockSpec((N_HEADS, n_pad), lambda i: (0, 0)),
        compiler_params=_cparams(("arbitrary",), 2 * (_nbytes((N_HEADS, n_pad), F32)
                                                      + _nbytes((SUBLANES, n_pad), jnp.int32))),
        name="bias_table",
    )(tab, jnp.asarray(idx))
    n_p = pidx.size
    prompt = out[:, :n_p].reshape(N_HEADS, len(DILATIONS), 2, DIL_STEPS, 2 * DIL_STEPS)
    samp = out[:, n_p:n_p + cached.size]
    samp_new = out[:, n_p + cached.size:n_p + cached.size + LANES]
    return prompt, samp, samp_new


def _rms(x, g):
    return x * lax.rsqrt(jnp.mean(x * x, axis=-1, keepdims=True) + EPS) * g


def _head_rms(p, g, gmat):
    ms = jnp.dot((p * p).astype(BF16), gmat, preferred_element_type=F32)
    return p * lax.rsqrt(ms + EPS) * g


def _cat_groups(ref, rows):
    return jnp.concatenate([ref[g, rows, :] for g in range(N_GROUPS)], axis=1)


def _sub_tiles(tm, sub):
    sub = min(tm, sub)
    return [pl.ds(r, sub) for r in range(0, tm, sub)]


def _proj_kernel(*refs, normed, t_plane, tiles_per_seq, first_kept):
    if normed:
        x_ref, g_ref, w_ref, gain_ref, gmat_ref, o_ref, t_ref, n_sc = refs
    else:
        x_ref, g_ref, w_ref, o_ref, t_ref, n_sc = refs
    i, j = pl.program_id(0), pl.program_id(1)

    @pl.when(j == 0)
    def _():
        n_sc[...] = _rms(x_ref[...], g_ref[...]).astype(BF16)

    y = jnp.dot(n_sc[...], w_ref[...], preferred_element_type=F32)
    if normed:
        y = _head_rms(y, gain_ref[...], gmat_ref[...])
    for g in range(PROJ_COLS // LANES):
        o_ref[g] = y[:, g * LANES:(g + 1) * LANES]

    @pl.when(jnp.logical_and(j // PROJ_SPLIT == t_plane, i % tiles_per_seq >= first_kept))
    def _():
        t_ref[...] = y.T


def _proj(x, g_mix, w_in, plane0, n_planes, t_plane, seq, kw, tm, gains=None, gmat=None):
    m, dm = x.shape
    da = D_ATTN
    assert seq % tm == 0 and kw % tm == 0 and m % seq == 0
    tps, first_kept, batch = seq // tm, (seq - kw) // tm, m // seq
    normed = gains is not None
    grp = PROJ_COLS // LANES
    in_specs = [pl.BlockSpec((tm, dm), lambda i, j: (i, 0)), pl.BlockSpec((1, dm), lambda i, j: (0, 0)),
                pl.BlockSpec((dm, PROJ_COLS), lambda i, j: (0, plane0 * PROJ_SPLIT + j))]
    args = [x, g_mix, w_in]
    if normed:
        in_specs += [pl.BlockSpec((None, 1, PROJ_COLS), lambda i, j: (j, 0, 0)),
                     _resident((PROJ_COLS, PROJ_COLS), lambda i, j: (0, 0))]
        args += [gains.reshape(n_planes * PROJ_SPLIT, 1, PROJ_COLS), gmat]

    def t_index(i, j):
        kept = i % tps >= first_kept
        half = jnp.where(kept, jnp.clip(j - t_plane * PROJ_SPLIT, 0, PROJ_SPLIT - 1), 0)
        return (i // tps, half, jnp.maximum(i % tps - first_kept, 0))

    out_specs = (pl.BlockSpec((None, grp, tm, LANES), lambda i, j: (j // PROJ_SPLIT, j % PROJ_SPLIT, i, 0)),
                 pl.BlockSpec((None, PROJ_COLS, tm), t_index))
    vmem = (2 * (_nbytes((tm, dm), F32) + _nbytes((dm, PROJ_COLS), BF16) + 2 * _nbytes((tm, PROJ_COLS), F32))
            + _nbytes((tm, dm), BF16) + 6 * _nbytes((tm, PROJ_COLS), F32))
    return pl.pallas_call(
        functools.partial(_proj_kernel, normed=normed, t_plane=t_plane, tiles_per_seq=tps,
                          first_kept=first_kept),
        out_shape=(jax.ShapeDtypeStruct((n_planes, N_GROUPS, m, LANES), F32),
                   jax.ShapeDtypeStruct((batch, da, kw), F32)),
        grid=(m // tm, n_planes * PROJ_SPLIT),
        in_specs=in_specs,
        out_specs=out_specs,
        scratch_shapes=[pltpu.VMEM((tm, dm), BF16)],
        compiler_params=_cparams(("arbitrary", "arbitrary"), vmem),
        name="in_proj_qk" if normed else "in_proj_vhbc",
    )(*args)


def _gather_rows(ref, starts, run):
    parts = [ref[pl.ds(s, run), :] for s in starts]
    return parts[0] if len(parts) == 1 else jnp.concatenate(parts, axis=0)


def _scatter_rows(ref, starts, run, val):
    for i, s in enumerate(starts):
        ref[pl.ds(s, run), :] = val[i * run:(i + 1) * run]


def _sample_scores(row, q_ref, kn_ref, kt_ref, bias_ref, bnew_ref):
    lb = kt_ref.shape[1]
    nd = len(DILATIONS)
    seg = (lax.broadcasted_iota(jnp.int32, (N_HEADS, D_ATTN), 1) // HEAD_DIM
           == lax.broadcasted_iota(jnp.int32, (N_HEADS, D_ATTN), 0))
    qe = jnp.where(seg, _cat_groups(q_ref, row), 0.0).astype(BF16)
    kn = _cat_groups(kn_ref, row).astype(BF16).astype(F32)
    s_new = jnp.sum(qe.astype(F32) * kn, axis=1, keepdims=True)
    sn = [s_new + bnew_ref[:, di:di + 1] for di in range(nd)]
    starts = range(0, lb, SAMPLE_POS_CHUNK)
    chunks = [pl.ds(c, SAMPLE_POS_CHUNK) for c in starts]
    sc = []
    for c, ch in zip(starts, chunks):
        s = jnp.dot(qe, kt_ref[:, ch].astype(BF16), preferred_element_type=F32)
        sc.append([s + bias_ref[:, pl.ds(di * lb + c, SAMPLE_POS_CHUNK)] for di in range(nd)])
    m = functools.reduce(jnp.maximum, [jnp.max(x, axis=1, keepdims=True) for row_ in sc for x in row_] + sn)
    pn = functools.reduce(jnp.add, [jnp.exp(x - m) for x in sn])
    ps = [functools.reduce(jnp.add, [jnp.exp(x - m) for x in sd]) for sd in sc]
    return seg, chunks, ps, pn


def _sample_output(row, state, vn_ref, vt_ref, o_ref):
    seg, chunks, ps, pn = state
    den = pn
    o = pn * _cat_groups(vn_ref, row)
    for ch, p in zip(chunks, ps):
        den = den + jnp.sum(p, axis=1, keepdims=True)
        o = o + lax.dot_general(p.astype(BF16), vt_ref[:, ch].astype(BF16),
                                (((1,), (1,)), ((), ())), preferred_element_type=F32)
    out = jnp.sum(jnp.where(seg, o / den, 0.0), axis=0, keepdims=True)
    for g in range(N_GROUPS):
        o_ref[g, row, :] = out[:, g * LANES:(g + 1) * LANES]


def _attn_kernel(q_ref, k_ref, v_ref, bias_ref, sq_ref, skn_ref, svn_ref, ck_hbm, cv_hbm, sbias_ref,
                 sbnew_ref, o_ref, so_ref, qp, kp, vp, acc, m_sc, l_sc, kbuf, vbuf, sem, *, samples_per_step):
    sb = pl.program_id(2)
    step = (pl.program_id(0) * pl.num_programs(1) + pl.program_id(1)) * pl.num_programs(2) + sb
    n_samples = ck_hbm.shape[0]
    slot = sb % 2
    pslot = 1 - slot
    kcur, vcur = kp.at[slot], vp.at[slot]
    kprev, vprev = kp.at[pslot], vp.at[pslot]

    def k_copy(g):
        return pltpu.make_async_copy(ck_hbm.at[g], kbuf, sem.at[0])

    def v_copy(g):
        return pltpu.make_async_copy(cv_hbm.at[g], vbuf, sem.at[1])

    @pl.when(step == 0)
    def _():
        k_copy(0).start()
        v_copy(0).start()

    def sample(t):
        g = step * samples_per_step + t
        row = pl.ds(g, 1)
        k_copy(g).wait()
        state = _sample_scores(row, sq_ref, skn_ref, kbuf, sbias_ref, sbnew_ref)

        @pl.when(g + 1 < n_samples)
        def _():
            k_copy(g + 1).start()

        v_copy(g).wait()
        _sample_output(row, state, svn_ref, vbuf, so_ref)

        @pl.when(g + 1 < n_samples)
        def _():
            v_copy(g + 1).start()

    for r in range(N_RES):
        rows = pl.ds(r * DIL_STEPS, DIL_STEPS)
        src = pl.ds(r, DIL_STEPS, stride=N_RES)
        qp[rows, :] = q_ref[src, :]
        kcur[rows, :] = k_ref[src, :]
        vcur[rows, :] = v_ref[src, :]

    @pl.when(sb == 0)
    def _():
        kprev[...] = jnp.zeros(kprev.shape, F32)
        vprev[...] = jnp.zeros(vprev.shape, F32)

    head0 = lax.broadcasted_iota(jnp.int32, (DIL_STEPS, LANES), 1) < HEAD_DIM

    def block(di, starts, run, prev_ref_k, prev_ref_v, prev_starts, first, mode, out_rows=None):
        qb = _gather_rows(qp, starts, run)
        kb = jnp.concatenate([_gather_rows(prev_ref_k, prev_starts, run),
                              _gather_rows(kcur, starts, run)], axis=0).astype(BF16)
        vb = jnp.concatenate([_gather_rows(prev_ref_v, prev_starts, run),
                              _gather_rows(vcur, starts, run)], axis=0).astype(BF16)
        ms, ls, os_ = [], [], []
        for h in range(HEADS_PER_GROUP):
            keep = head0 if h == 0 else jnp.logical_not(head0)
            qh = jnp.where(keep, qb, 0.0).astype(BF16)
            s = lax.dot_general(qh, kb, (((1,), (1,)), ((), ())), preferred_element_type=F32)
            s = s + bias_ref[h, di, first]
            mh = jnp.max(s, axis=1, keepdims=True)
            p = jnp.exp(s - mh)
            ls.append(jnp.sum(p, axis=1, keepdims=True))
            os_.append(jnp.dot(p.astype(BF16), vb, preferred_element_type=F32))
            ms.append(mh)
        m_c = jnp.where(head0, ms[0], ms[1])
        l_c = jnp.where(head0, ls[0], ls[1])
        o_c = jnp.where(head0, os_[0], os_[1])
        if mode == "init":
            _scatter_rows(m_sc, starts, run, m_c)
            _scatter_rows(l_sc, starts, run, l_c)
            _scatter_rows(acc, starts, run, o_c)
            return
        m_o = _gather_rows(m_sc, starts, run)
        m_n = jnp.maximum(m_o, m_c)
        a_o = jnp.exp(m_o - m_n)
        a_c = jnp.exp(m_c - m_n)
        l_n = _gather_rows(l_sc, starts, run) * a_o + l_c * a_c
        o_n = _gather_rows(acc, starts, run) * a_o + o_c * a_c
        if mode == "merge":
            _scatter_rows(m_sc, starts, run, m_n)
            _scatter_rows(l_sc, starts, run, l_n)
            _scatter_rows(acc, starts, run, o_n)
        else:
            o_ref[out_rows, :] = o_n / l_n

    def run_dilation(di, mode):
        d = DILATIONS[di]
        runs = N_RES // d
        run = DIL_STEPS // runs
        nblk = SUPER // d // DIL_STEPS

        def body(it, carry):
            c = it // nblk
            n = it % nblk
            starts = [pl.multiple_of((d * b + c) * DIL_STEPS + run * n, SUBLANES) for b in range(runs)]
            pn = jnp.where(n > 0, n - 1, nblk - 1)
            prev_starts = [pl.multiple_of((d * b + c) * DIL_STEPS + run * pn, SUBLANES)
                           for b in range(runs)]
            pslot_n = jnp.where(n > 0, slot, pslot)
            first = jnp.logical_and(sb == 0, n == 0).astype(jnp.int32)
            out_rows = pl.ds(it, DIL_STEPS, stride=N_RES) if mode == "final" else None
            block(di, starts, run, kp.at[pslot_n], vp.at[pslot_n], prev_starts, first, mode, out_rows)
            return carry

        lax.fori_loop(0, d * nblk, body, 0, unroll=ATTN_UNROLL)

    phases = ("init", "merge", "final")
    for t in range(max(samples_per_step, len(phases))):
        if t < samples_per_step:
            sample(t)
        if t < len(phases):
            run_dilation(t, phases[t])


def _attn(qk, vhbc, bias, batch, seq, sqk, svhbc, cache_k, cache_v, sbias, sbias_new):
    nsb = seq // SUPER
    m = batch * seq
    bd, lb = cache_k.shape[0], cache_k.shape[1]
    da = D_ATTN
    n_steps = N_GROUPS * batch * nsb
    assert bd % n_steps == 0 and lb % SAMPLE_POS_CHUNK == 0

    def feature_major(c):
        return jnp.transpose(c, (0, 2, 3, 1)).reshape(bd, da, lb)

    def plane(p):
        return pl.BlockSpec((None, None, SUPER, LANES), lambda g, b, s: (p, g, b * nsb + s, 0))

    def splane(p):
        return pl.BlockSpec((None, N_GROUPS, bd, LANES), lambda g, b, s: (p, 0, 0, 0))

    fixed2 = lambda g, b, s: (0, 0)
    bias_spec = pl.BlockSpec((HEADS_PER_GROUP,) + bias.shape[1:], lambda g, b, s: (g, 0, 0, 0, 0))
    hbm = pl.BlockSpec(memory_space=pl.ANY)
    blk_bytes = _nbytes((SUPER, LANES), F32)
    return pl.pallas_call(
        functools.partial(_attn_kernel, samples_per_step=bd // n_steps),
        out_shape=(jax.ShapeDtypeStruct((N_GROUPS, m, LANES), F32),
                   jax.ShapeDtypeStruct((N_GROUPS, bd, LANES), F32)),
        grid=(N_GROUPS, batch, nsb),
        in_specs=[plane(0), plane(1), plane(0), bias_spec,
                  splane(0), splane(1), splane(0), hbm, hbm,
                  pl.BlockSpec(sbias.shape, fixed2), pl.BlockSpec(sbias_new.shape, fixed2)],
        out_specs=(pl.BlockSpec((None, SUPER, LANES), lambda g, b, s: (g, b * nsb + s, 0)),
                   pl.BlockSpec((N_GROUPS, bd, LANES), lambda g, b, s: (0, 0, 0))),
        scratch_shapes=[pltpu.VMEM((SUPER, LANES), F32),
                        pltpu.VMEM((2, SUPER, LANES), F32), pltpu.VMEM((2, SUPER, LANES), F32),
                        pltpu.VMEM((SUPER, LANES), F32), pltpu.VMEM((SUPER, LANES), F32),
                        pltpu.VMEM((SUPER, LANES), F32),
                        pltpu.VMEM((da, lb), F32), pltpu.VMEM((da, lb), F32),
                        pltpu.SemaphoreType.DMA((2,))],
        compiler_params=_cparams(("arbitrary", "arbitrary", "arbitrary"),
                                 (2 * 4 + 8) * blk_bytes
                                 + 2 * _nbytes((HEADS_PER_GROUP,) + bias.shape[1:], F32)
                                 + 2 * _nbytes((da, lb), F32) + 2 * 4 * _nbytes((bd, da), F32)
                                 + 2 * _nbytes(sbias.shape, F32)),
        name="attn",
    )(qk, qk, vhbc, bias, sqk, sqk, svhbc, feature_major(cache_k), feature_major(cache_v), sbias, sbias_new)


def _mix_tail(x, a, conv, ga_ref, gc_ref, w_ref):
    cat = jnp.concatenate([_rms(a, ga_ref[...]), _rms(conv, gc_ref[...])], axis=1).astype(BF16)
    return x + jnp.dot(cat, w_ref[...], preferred_element_type=F32)


def _mix_prompt_kernel(x_ref, a_ref, h_ref, b_ref, c_ref, hh_ref, ch_ref, cw_ref, ga_ref, gc_ref, w_ref,
                       o_ref, ut_ref, *, tiles_per_seq):
    tm = x_ref.shape[0]
    seq_start = pl.program_id(0) % tiles_per_seq == 0
    all8 = pl.ds(0, SUBLANES)
    halo = jnp.where(seq_start, 0.0, _cat_groups(hh_ref, all8) * _cat_groups(ch_ref, all8))
    for rows in _sub_tiles(tm, MIX_SUB_ROWS):
        u = _cat_groups(h_ref, rows) * _cat_groups(c_ref, rows)
        rid = lax.broadcasted_iota(jnp.int32, u.shape, 0)
        u1 = jnp.where(rid == 0, halo[7:8], pltpu.roll(u, 1, axis=0))
        u2 = jnp.where(rid == 0, halo[6:7], jnp.where(rid == 1, halo[7:8], pltpu.roll(u, 2, axis=0)))
        cy = cw_ref[0:1] * u2 + cw_ref[1:2] * u1 + cw_ref[2:3] * u
        conv = _cat_groups(b_ref, rows) * cy
        o_ref[rows, :] = _mix_tail(x_ref[rows, :], _cat_groups(a_ref, rows), conv, ga_ref, gc_ref, w_ref)
        halo = u[u.shape[0] - SUBLANES:]
    ut_ref[...] = halo


def _mix_sample_kernel(x_ref, a_ref, h_ref, b_ref, c_ref, buf_ref, cw_ref, ga_ref, gc_ref, w_ref, o_ref, u_ref):
    every = pl.ds(0, x_ref.shape[0])
    u = _cat_groups(h_ref, every) * _cat_groups(c_ref, every)
    dc = u.shape[1]
    cy = cw_ref[0:1] * buf_ref[:, :dc] + cw_ref[1:2] * buf_ref[:, dc:] + cw_ref[2:3] * u
    conv = _cat_groups(b_ref, every) * cy
    o_ref[...] = _mix_tail(x_ref[...], _cat_groups(a_ref, every), conv, ga_ref, gc_ref, w_ref)
    u_ref[...] = u


def _mix_out_prompt(x, attn, vhbc, conv_w, ga, gc, w_out, seq, tm):
    m, dm = x.shape
    dc = conv_w.shape[1]
    fixed = lambda i: (0, 0)
    halo_blocks = tm // SUBLANES

    def plane(p):
        return pl.BlockSpec((None, N_GROUPS, tm, LANES), lambda i: (p, 0, i, 0))

    def halo(p):
        return pl.BlockSpec((None, N_GROUPS, SUBLANES, LANES),
                            lambda i: (p, 0, jnp.maximum(i * halo_blocks - 1, 0), 0))

    tile = _nbytes((tm, dc), F32)
    return pl.pallas_call(
        functools.partial(_mix_prompt_kernel, tiles_per_seq=seq // tm),
        out_shape=(jax.ShapeDtypeStruct((m, dm), F32), jax.ShapeDtypeStruct((m // tm, SUBLANES, dc), F32)),
        grid=(m // tm,),
        in_specs=[pl.BlockSpec((tm, dm), lambda i: (i, 0)),
                  pl.BlockSpec((N_GROUPS, tm, LANES), lambda i: (0, i, 0)),
                  plane(1), plane(2), plane(3), halo(1), halo(3),
                  pl.BlockSpec((CONV_W, dc), fixed), pl.BlockSpec((1, D_ATTN), fixed),
                  pl.BlockSpec((1, dc), fixed), _resident(w_out.shape, fixed)],
        out_specs=(pl.BlockSpec((tm, dm), lambda i: (i, 0)),
                   pl.BlockSpec((None, SUBLANES, dc), lambda i: (i, 0, 0))),
        compiler_params=_cparams(("arbitrary",), 2 * 8 * tile + _nbytes(w_out.shape, BF16)
                                 + 4 * _nbytes((min(tm, MIX_SUB_ROWS), dm), F32)),
        name="mix_out_prompt",
    )(x, attn, vhbc, vhbc, vhbc, vhbc, vhbc, conv_w, ga, gc, w_out)


def _mix_out_sample(x, attn, vhbc, buf, conv_w, ga, gc, w_out):
    m, dm = x.shape
    dc = conv_w.shape[1]
    full = lambda a: pl.BlockSpec(a.shape, lambda i: (0,) * a.ndim)

    def plane(p):
        return pl.BlockSpec((None, N_GROUPS, m, LANES), lambda i: (p, 0, 0, 0))

    return pl.pallas_call(
        _mix_sample_kernel,
        out_shape=(jax.ShapeDtypeStruct((m, dm), F32), jax.ShapeDtypeStruct((m, dc), F32)),
        grid=(1,),
        in_specs=[full(x), full(attn), plane(1), plane(2), plane(3), full(buf), full(conv_w), full(ga),
                  full(gc), full(w_out)],
        out_specs=(pl.BlockSpec((m, dm), lambda i: (0, 0)), pl.BlockSpec((m, dc), lambda i: (0, 0))),
        compiler_params=_cparams(("arbitrary",), 2 * (10 * _nbytes((m, dc), F32) + _nbytes(w_out.shape, BF16))),
        name="mix_out_sample",
    )(x, attn, vhbc, vhbc, vhbc, buf, conv_w, ga, gc, w_out)


def _ffn_kernel(x_ref, g_ref, wg_ref, wu_ref, wd_ref, o_ref, n_sc):
    f = pl.program_id(1)

    @pl.when(f == 0)
    def _():
        x = x_ref[...]
        n_sc[...] = _rms(x, g_ref[...]).astype(BF16)
        o_ref[...] = x

    n = n_sc[...]
    gate = jnp.dot(n, wg_ref[...], preferred_element_type=F32)
    up = jnp.dot(n, wu_ref[...], preferred_element_type=F32)
    act = (gate / (1.0 + jnp.exp(-gate)) * up).astype(BF16)
    o_ref[...] += jnp.dot(act, wd_ref[...], preferred_element_type=F32)


def _ffn(x, g, wg, wu, wd, tm, tf):
    m, dm = x.shape
    dff = wg.shape[1]
    row = lambda i, f: (i, 0)
    return pl.pallas_call(
        _ffn_kernel,
        out_shape=jax.ShapeDtypeStruct((m, dm), F32),
        grid=(m // tm, dff // tf),
        in_specs=[pl.BlockSpec((tm, dm), row), pl.BlockSpec((1, dm), lambda i, f: (0, 0)),
                  pl.BlockSpec((dm, tf), lambda i, f: (0, f)), pl.BlockSpec((dm, tf), lambda i, f: (0, f)),
                  pl.BlockSpec((tf, dm), lambda i, f: (f, 0))],
        out_specs=pl.BlockSpec((tm, dm), row),
        scratch_shapes=[pltpu.VMEM((tm, dm), BF16)],
        compiler_params=_cparams(("arbitrary", "arbitrary"),
                                 4 * _nbytes((tm, dm), F32) + _nbytes((tm, dm), BF16)
                                 + 2 * 3 * _nbytes((dm, tf), BF16) + 3 * _nbytes((tm, tf), F32)),
        name="ffn",
    )(x, g, wg, wu, wd)


def _ple_kernel(x_ref, p_ref, g_ref, wg_ref, wp_ref, o_ref):
    for rows in _sub_tiles(x_ref.shape[0], PLE_SUB_ROWS):
        x = x_ref[rows, :]
        z = jnp.dot(_rms(x, g_ref[...]).astype(BF16), wg_ref[...], preferred_element_type=F32)
        e = jnp.dot(p_ref[rows, :].astype(BF16), wp_ref[...], preferred_element_type=F32)
        o_ref[rows, :] = x + e / (1.0 + jnp.exp(-z))


def _ple(x, p, g, wg, wp, tm):
    m, dm = x.shape
    dp = p.shape[1]
    row = lambda i: (i, 0)
    fixed = lambda i: (0, 0)
    return pl.pallas_call(
        _ple_kernel,
        out_shape=jax.ShapeDtypeStruct((m, dm), F32),
        grid=(m // tm,),
        in_specs=[pl.BlockSpec((tm, dm), row), pl.BlockSpec((tm, dp), row), pl.BlockSpec((1, dm), fixed),
                  _resident(wg.shape, fixed), _resident(wp.shape, fixed)],
        out_specs=pl.BlockSpec((tm, dm), row),
        compiler_params=_cparams(("arbitrary",),
                                 2 * (2 * _nbytes((tm, dm), F32) + _nbytes((tm, dp), F32))
                                 + _nbytes(wg.shape, BF16) + _nbytes(wp.shape, BF16)
                                 + 6 * _nbytes((min(tm, PLE_SUB_ROWS), dm), F32)),
        name="ple",
    )(x, p, g, wg, wp)


def _tiles(m):
    return dict(proj=min(m, 1024), mix=min(m, 512), ffn=min(m, 512), ffn_cols=512, ple=min(m, 1024))


def _window_rows(t, batch, kw):
    return jnp.transpose(t.reshape(batch, N_HEADS, HEAD_DIM, kw), (0, 3, 1, 2))


def kernel(x_prompt, x_sample, p_prompt, p_sample, cache_k, cache_v, state_conv, rel_bias, g_mix, w_in,
           q_norm, k_norm, conv_w, g_attn_out, g_conv_out, w_out, g_ffn, w_gate, w_up, w_down, g_ple,
           w_ple_gate, w_ple_proj):
    depth = g_mix.shape[0]
    batch, seq, dm = x_prompt.shape
    bd, dec_seq, _ = x_sample.shape
    dc = conv_w.shape[2]
    assert depth == 1 and dec_seq == 1, "single layer, one new position per sample"
    assert seq % SUPER == 0 and dm == D_ATTN + dc and dc == D_ATTN
    assert w_gate.shape[2] % 512 == 0

    bias_p, bias_s, bias_s_new = _bias_tables(rel_bias, cache_k.shape[2])
    gmat = jnp.asarray(np.kron(np.eye(PROJ_COLS // HEAD_DIM, dtype=np.float32),
                               np.full((HEAD_DIM, HEAD_DIM), 1.0 / HEAD_DIM, np.float32)), BF16)

    i = 0
    row2 = lambda a: a.reshape(1, -1).astype(F32)
    g_mix_i, g_ffn_i, g_ple_i = row2(g_mix[i]), row2(g_ffn[i]), row2(g_ple[i])
    qk_gains = jnp.stack([row2(jnp.tile(q_norm[i], N_HEADS) * SCALE), row2(jnp.tile(k_norm[i], N_HEADS))])
    ga, gc = row2(g_attn_out[i]), row2(g_conv_out[i])
    cw = conv_w[i].astype(F32)
    w_in_i, w_out_i = w_in[i].astype(BF16), w_out[i].astype(BF16)
    wg_i, wu_i, wd_i = w_gate[i].astype(BF16), w_up[i].astype(BF16), w_down[i].astype(BF16)
    wpg_i, wpp_i = w_ple_gate[i].astype(BF16), w_ple_proj[i].astype(BF16)

    mp = batch * seq
    tp, ts = _tiles(mp), _tiles(bd)
    kw = min(SUPER, seq)
    xp = x_prompt.reshape(mp, dm)
    xs = x_sample.reshape(bd, dm)
    qks, kts = _proj(xs, g_mix_i, w_in_i, 0, 2, 1, bd, bd, ts["proj"], qk_gains, gmat)
    vhbcs, vts = _proj(xs, g_mix_i, w_in_i, 2, 4, 0, bd, bd, ts["proj"])
    qk, kt = _proj(xp, g_mix_i, w_in_i, 0, 2, 1, seq, kw, tp["proj"], qk_gains, gmat)
    vhbc, vt = _proj(xp, g_mix_i, w_in_i, 2, 4, 0, seq, kw, tp["proj"])
    attn, attn_s = _attn(qk, vhbc, bias_p, batch, seq, qks, vhbcs, cache_k[i], cache_v[i], bias_s, bias_s_new)

    h, u_tail = _mix_out_prompt(xp, attn, vhbc, cw, ga, gc, w_out_i, seq, tp["mix"])
    h = _ffn(h, g_ffn_i, wg_i, wu_i, wd_i, tp["ffn"], tp["ffn_cols"])
    h = _ple(h, p_prompt[i].reshape(mp, -1), g_ple_i, wpg_i, wpp_i, tp["ple"])
    y_prompt = h.reshape(batch, seq, dm)
    k_prompt = _window_rows(kt, batch, kw)[None]
    v_prompt = _window_rows(vt, batch, kw)[None]
    tiles_per_seq = seq // tp["mix"]
    conv_prompt = u_tail.reshape(batch, tiles_per_seq, SUBLANES, dc)[None, :, -1, SUBLANES - (CONV_W - 1):]

    buf = state_conv[i].astype(F32)
    hs, us = _mix_out_sample(xs, attn_s, vhbcs, buf.reshape(bd, (CONV_W - 1) * dc), cw, ga, gc, w_out_i)
    hs = _ffn(hs, g_ffn_i, wg_i, wu_i, wd_i, ts["ffn"], ts["ffn_cols"])
    hs = _ple(hs, p_sample[i].reshape(bd, -1), g_ple_i, wpg_i, wpp_i, ts["ple"])
    y_sample = hs.reshape(bd, dec_seq, dm)
    k_sample = _window_rows(kts, 1, bd).reshape(1, bd, dec_seq, N_HEADS, HEAD_DIM)
    v_sample = _window_rows(vts, 1, bd).reshape(1, bd, dec_seq, N_HEADS, HEAD_DIM)
    conv_sample = jnp.concatenate([buf[:, 1:], us[:, None, :]], axis=1)[None]

    return (y_prompt, y_sample, k_prompt, v_prompt, conv_prompt, k_sample, v_sample, conv_sample)
```

```python
import functools
import math

import numpy as np
import jax
import jax.numpy as jnp
from jax import lax
from jax.experimental import pallas as pl
from jax.experimental.pallas import tpu as pltpu

HEAD_DIM = 64
N_HEADS = 16
D_ATTN = N_HEADS * HEAD_DIM
CONV_W = 3
DIL_STEPS = 128
DILATIONS = (1, 4, 16)
N_BUCKETS = 32
MAX_EXACT = N_BUCKETS // 2
MAX_DIST = DIL_STEPS * max(DILATIONS)
EPS = 1e-6
SCALE = HEAD_DIM ** -0.5

LANES = 128
SUBLANES = 8
BF16_ROWS = 16
HEADS_PER_GROUP = LANES // HEAD_DIM
N_GROUPS = N_HEADS // HEADS_PER_GROUP
SUPER = DIL_STEPS * max(DILATIONS)
N_RES = max(DILATIONS)
MASK_VALUE = -1e30
MASK_BUCKET = N_BUCKETS
TABLE_ROWS = -(-(N_BUCKETS + 1) // BF16_ROWS) * BF16_ROWS
TABLE_CHUNK = 2048
SAMPLE_POS_CHUNK = 512
ATTN_UNROLL = 16
PROJ_COLS = 512
PROJ_SPLIT = D_ATTN // PROJ_COLS
MIX_SUB_ROWS = 256
PLE_SUB_ROWS = 512
VMEM_SLACK_BYTES = 8 * 1024 * 1024

F32 = jnp.float32
BF16 = jnp.bfloat16


def _cparams(sem, vmem_bytes):
    return pltpu.CompilerParams(dimension_semantics=sem,
                                vmem_limit_bytes=int(vmem_bytes + VMEM_SLACK_BYTES))


def _nbytes(shape, dtype):
    return int(np.prod(shape)) * jnp.dtype(dtype).itemsize


def _resident(shape, index_map):
    return pl.BlockSpec(shape, index_map, pipeline_mode=pl.Buffered(1))


def _t5_bucket(dist):
    n = np.asarray(dist, np.int32)
    nf = np.maximum(n, 1).astype(np.float32)
    large = MAX_EXACT + (np.log(nf / MAX_EXACT) / np.float32(math.log(MAX_DIST / MAX_EXACT))
                         * (N_BUCKETS - MAX_EXACT)).astype(np.int32)
    large = np.minimum(large, N_BUCKETS - 1)
    return np.where(n < MAX_EXACT, n, large).astype(np.int32)


def _stored_to_natural(d):
    runs = N_RES // d
    run = DIL_STEPS // runs
    j = np.arange(DIL_STEPS)
    return (j % run) * runs + j // run


def _prompt_bias_index():
    out = np.empty((len(DILATIONS), 2, DIL_STEPS, 2 * DIL_STEPS), np.int32)
    for di, d in enumerate(DILATIONS):
        nat = _stored_to_natural(d)
        qi = nat[:, None]
        kj = np.concatenate([nat, nat + DIL_STEPS])[None, :]
        steps = DIL_STEPS + qi - kj
        band = (steps >= 0) & (steps <= DIL_STEPS)
        bucket = _t5_bucket(d * np.clip(steps, 0, DIL_STEPS))
        out[di, 0] = np.where(band, bucket, MASK_BUCKET)
        out[di, 1] = np.where(band & (kj >= DIL_STEPS), bucket, MASK_BUCKET)
    return out


def _sample_bias_index(lb):
    back = lb - np.arange(lb)
    cached = np.stack([np.where((back % d == 0) & (back // d <= DIL_STEPS), _t5_bucket(back), MASK_BUCKET)
                       for d in DILATIONS])
    new = np.stack([_t5_bucket(d * np.zeros(1, np.int32)) for d in DILATIONS])
    return cached, new


def _bias_table_kernel(tab_ref, idx_ref, o_ref):
    t = tab_ref[...]
    hi = t.astype(BF16)
    r1 = t - hi.astype(F32)
    mid = r1.astype(BF16)
    lo = (r1 - mid.astype(F32)).astype(BF16)
    rows = lax.broadcasted_iota(jnp.int32, (TABLE_ROWS, TABLE_CHUNK), 0)

    def chunk(c, carry):
        sl = pl.ds(pl.multiple_of(c * TABLE_CHUNK, TABLE_CHUNK), TABLE_CHUNK)
        onehot = jnp.where(rows == idx_ref[:, sl], 1.0, 0.0).astype(BF16)
        acc = jnp.dot(hi, onehot, preferred_element_type=F32)
        acc = acc + jnp.dot(mid, onehot, preferred_element_type=F32)
        acc = acc + jnp.dot(lo, onehot, preferred_element_type=F32)
        o_ref[:, sl] = acc
        return carry

    lax.fori_loop(0, o_ref.shape[1] // TABLE_CHUNK, chunk, 0)


def _bias_tables(rel_bias, lb):
    pidx = _prompt_bias_index()
    cached, new = _sample_bias_index(lb)
    flat = np.concatenate([pidx.reshape(-1), cached.reshape(-1), new.reshape(-1)])
    n_pad = -(-(pidx.size + cached.size + LANES) // TABLE_CHUNK) * TABLE_CHUNK
    idx = np.full((1, n_pad), MASK_BUCKET, np.int32)
    idx[0, :flat.size] = flat
    tab = jnp.concatenate(
        [rel_bias.astype(F32).T,
         jnp.full((N_HEADS, 1), MASK_VALUE, F32),
         jnp.zeros((N_HEADS, TABLE_ROWS - N_BUCKETS - 1), F32)], axis=1)
    out = pl.pallas_call(
        _bias_table_kernel,
        out_shape=jax.ShapeDtypeStruct((N_HEADS, n_pad), F32),
        grid=(1,),
        in_specs=[pl.BlockSpec((N_HEADS, TABLE_ROWS), lambda i: (0, 0)),
                  pl.BlockSpec((1, n_pad), lambda i: (0, 0))],
        out_specs=pl.BlockSpec((N_HEADS, n_pad), lambda i: (0, 0)),
        compiler_params=_cparams(("arbitrary",), 2 * (_nbytes((N_HEADS, n_pad), F32)
                                                      + _nbytes((SUBLANES, n_pad), jnp.int32))),
        name="bias_table",
    )(tab, jnp.asarray(idx))
    n_p = pidx.size
    prompt = out[:, :n_p].reshape(N_HEADS, len(DILATIONS), 2, DIL_STEPS, 2 * DIL_STEPS)
    samp = out[:, n_p:n_p + cached.size]
    samp_new = out[:, n_p + cached.size:n_p + cached.size + LANES]
    return prompt, samp, samp_new


def _rms(x, g):
    return x * lax.rsqrt(jnp.mean(x * x, axis=-1, keepdims=True) + EPS) * g


def _head_rms(p, g, gmat):
    ms = jnp.dot((p * p).astype(BF16), gmat, preferred_element_type=F32)
    return p * lax.rsqrt(ms + EPS) * g


def _cat_groups(ref, rows):
    return jnp.concatenate([ref[g, rows, :] for g in range(N_GROUPS)], axis=1)


def _sub_tiles(tm, sub):
    sub = min(tm, sub)
    return [pl.ds(r, sub) for r in range(0, tm, sub)]


def _proj_kernel(*refs, normed, t_plane, tiles_per_seq, first_kept):
    if normed:
        x_ref, g_ref, w_ref, gain_ref, gmat_ref, o_ref, t_ref, n_sc = refs
    else:
        x_ref, g_ref, w_ref, o_ref, t_ref, n_sc = refs
    i, j = pl.program_id(0), pl.program_id(1)

    @pl.when(j == 0)
    def _():
        n_sc[...] = _rms(x_ref[...], g_ref[...]).astype(BF16)

    y = jnp.dot(n_sc[...], w_ref[...].astype(BF16), preferred_element_type=F32)
    if normed:
        y = _head_rms(y, gain_ref[...], gmat_ref[...])
    for g in range(PROJ_COLS // LANES):
        o_ref[g] = y[:, g * LANES:(g + 1) * LANES]

    @pl.when(jnp.logical_and(j // PROJ_SPLIT == t_plane, i % tiles_per_seq >= first_kept))
    def _():
        t_ref[...] = y.T


def _proj(x, g_mix, w_in, plane0, n_planes, t_plane, seq, kw, tm, gains=None, gmat=None):
    m, dm = x.shape
    da = D_ATTN
    assert seq % tm == 0 and kw % tm == 0 and m % seq == 0
    tps, first_kept, batch = seq // tm, (seq - kw) // tm, m // seq
    normed = gains is not None
    grp = PROJ_COLS // LANES
    in_specs = [pl.BlockSpec((tm, dm), lambda i, j: (i, 0)), pl.BlockSpec((1, dm), lambda i, j: (0, 0)),
                pl.BlockSpec((dm, PROJ_COLS), lambda i, j: (0, plane0 * PROJ_SPLIT + j))]
    args = [x, g_mix, w_in]
    if normed:
        in_specs += [pl.BlockSpec((None, 1, PROJ_COLS), lambda i, j: (j, 0, 0)),
                     _resident((PROJ_COLS, PROJ_COLS), lambda i, j: (0, 0))]
        args += [gains.reshape(n_planes * PROJ_SPLIT, 1, PROJ_COLS), gmat]

    def t_index(i, j):
        kept = i % tps >= first_kept
        half = jnp.where(kept, jnp.clip(j - t_plane * PROJ_SPLIT, 0, PROJ_SPLIT - 1), 0)
        return (i // tps, half, jnp.maximum(i % tps - first_kept, 0))

    out_specs = (pl.BlockSpec((None, grp, tm, LANES), lambda i, j: (j // PROJ_SPLIT, j % PROJ_SPLIT, i, 0)),
                 pl.BlockSpec((None, PROJ_COLS, tm), t_index))
    vmem = (2 * (_nbytes((tm, dm), F32) + _nbytes((dm, PROJ_COLS), w_in.dtype) + 2 * _nbytes((tm, PROJ_COLS), F32))
            + _nbytes((tm, dm), BF16) + 6 * _nbytes((tm, PROJ_COLS), F32))
    return pl.pallas_call(
        functools.partial(_proj_kernel, normed=normed, t_plane=t_plane, tiles_per_seq=tps,
                          first_kept=first_kept),
        out_shape=(jax.ShapeDtypeStruct((n_planes, N_GROUPS, m, LANES), F32),
                   jax.ShapeDtypeStruct((batch, da, kw), F32)),
        grid=(m // tm, n_planes * PROJ_SPLIT),
        in_specs=in_specs,
        out_specs=out_specs,
        scratch_shapes=[pltpu.VMEM((tm, dm), BF16)],
        compiler_params=_cparams(("arbitrary", "arbitrary"), vmem),
        name="in_proj_qk" if normed else "in_proj_vhbc",
    )(*args)


def _gather_rows(ref, starts, run):
    parts = [ref[pl.ds(s, run), :] for s in starts]
    return parts[0] if len(parts) == 1 else jnp.concatenate(parts, axis=0)


def _scatter_rows(ref, starts, run, val):
    for i, s in enumerate(starts):
        ref[pl.ds(s, run), :] = val[i * run:(i + 1) * run]


def _sample_scores(row, q_ref, kn_ref, kt_ref, bias_ref, bnew_ref):
    lb = kt_ref.shape[1]
    nd = len(DILATIONS)
    seg = (lax.broadcasted_iota(jnp.int32, (N_HEADS, D_ATTN), 1) // HEAD_DIM
           == lax.broadcasted_iota(jnp.int32, (N_HEADS, D_ATTN), 0))
    qe = jnp.where(seg, _cat_groups(q_ref, row), 0.0).astype(BF16)
    kn = _cat_groups(kn_ref, row).astype(BF16).astype(F32)
    s_new = jnp.sum(qe.astype(F32) * kn, axis=1, keepdims=True)
    sn = [s_new + bnew_ref[:, di:di + 1] for di in range(nd)]
    starts = range(0, lb, SAMPLE_POS_CHUNK)
    chunks = [pl.ds(c, SAMPLE_POS_CHUNK) for c in starts]
    sc = []
    for c, ch in zip(starts, chunks):
        s = jnp.dot(qe, kt_ref[:, ch].astype(BF16), preferred_element_type=F32)
        sc.append([s + bias_ref[:, pl.ds(di * lb + c, SAMPLE_POS_CHUNK)] for di in range(nd)])
    m = functools.reduce(jnp.maximum, [jnp.max(x, axis=1, keepdims=True) for row_ in sc for x in row_] + sn)
    pn = functools.reduce(jnp.add, [jnp.exp(x - m) for x in sn])
    ps = [functools.reduce(jnp.add, [jnp.exp(x - m) for x in sd]) for sd in sc]
    return seg, chunks, ps, pn


def _sample_output(row, state, vn_ref, vt_ref, o_ref):
    seg, chunks, ps, pn = state
    den = pn
    o = pn * _cat_groups(vn_ref, row)
    for ch, p in zip(chunks, ps):
        den = den + jnp.sum(p, axis=1, keepdims=True)
        o = o + lax.dot_general(p.astype(BF16), vt_ref[:, ch].astype(BF16),
                                (((1,), (1,)), ((), ())), preferred_element_type=F32)
    out = jnp.sum(jnp.where(seg, o / den, 0.0), axis=0, keepdims=True)
    for g in range(N_GROUPS):
        o_ref[g, row, :] = out[:, g * LANES:(g + 1) * LANES]


def _attn_kernel(q_ref, k_ref, v_ref, bias_ref, sq_ref, skn_ref, svn_ref, ck_hbm, cv_hbm, sbias_ref,
                 sbnew_ref, o_ref, so_ref, qp, kp, vp, acc, m_sc, l_sc, kbuf, vbuf, sem, *, samples_per_step):
    sb = pl.program_id(2)
    step = (pl.program_id(0) * pl.num_programs(1) + pl.program_id(1)) * pl.num_programs(2) + sb
    n_samples = ck_hbm.shape[0]
    slot = sb % 2
    pslot = 1 - slot
    kcur, vcur = kp.at[slot], vp.at[slot]
    kprev, vprev = kp.at[pslot], vp.at[pslot]

    def copies(g, slot):
        return (pltpu.make_async_copy(ck_hbm.at[g], kbuf.at[slot], sem.at[0, slot]),
                pltpu.make_async_copy(cv_hbm.at[g], vbuf.at[slot], sem.at[1, slot]))

    @pl.when(step == 0)
    def _():
        for c in copies(0, 0):
            c.start()

    def sample(t):
        g = step * samples_per_step + t
        slot = t % 2
        row = pl.ds(g, 1)

        @pl.when(g + 1 < n_samples)
        def _():
            for c in copies(g + 1, 1 - slot):
                c.start()

        k_in, v_in = copies(g, slot)
        k_in.wait()
        state = _sample_scores(row, sq_ref, skn_ref, kbuf.at[slot], sbias_ref, sbnew_ref)
        v_in.wait()
        _sample_output(row, state, svn_ref, vbuf.at[slot], so_ref)

    for r in range(N_RES):
        rows = pl.ds(r * DIL_STEPS, DIL_STEPS)
        src = pl.ds(r, DIL_STEPS, stride=N_RES)
        qp[rows, :] = q_ref[src, :]
        kcur[rows, :] = k_ref[src, :]
        vcur[rows, :] = v_ref[src, :]

    @pl.when(sb == 0)
    def _():
        kprev[...] = jnp.zeros(kprev.shape, F32)
        vprev[...] = jnp.zeros(vprev.shape, F32)

    head0 = lax.broadcasted_iota(jnp.int32, (DIL_STEPS, LANES), 1) < HEAD_DIM

    def block(di, starts, run, prev_ref_k, prev_ref_v, prev_starts, first, mode, out_rows=None):
        qb = _gather_rows(qp, starts, run)
        kb = jnp.concatenate([_gather_rows(prev_ref_k, prev_starts, run),
                              _gather_rows(kcur, starts, run)], axis=0).astype(BF16)
        vb = jnp.concatenate([_gather_rows(prev_ref_v, prev_starts, run),
                              _gather_rows(vcur, starts, run)], axis=0).astype(BF16)
        ms, ls, os_ = [], [], []
        for h in range(HEADS_PER_GROUP):
            keep = head0 if h == 0 else jnp.logical_not(head0)
            qh = jnp.where(keep, qb, 0.0).astype(BF16)
            s = lax.dot_general(qh, kb, (((1,), (1,)), ((), ())), preferred_element_type=F32)
            s = s + bias_ref[h, di, first]
            mh = jnp.max(s, axis=1, keepdims=True)
            p = jnp.exp(s - mh)
            ls.append(jnp.sum(p, axis=1, keepdims=True))
            os_.append(jnp.dot(p.astype(BF16), vb, preferred_element_type=F32))
            ms.append(mh)
        m_c = jnp.where(head0, ms[0], ms[1])
        l_c = jnp.where(head0, ls[0], ls[1])
        o_c = jnp.where(head0, os_[0], os_[1])
        if mode == "init":
            _scatter_rows(m_sc, starts, run, m_c)
            _scatter_rows(l_sc, starts, run, l_c)
            _scatter_rows(acc, starts, run, o_c)
            return
        m_o = _gather_rows(m_sc, starts, run)
        m_n = jnp.maximum(m_o, m_c)
        a_o = jnp.exp(m_o - m_n)
        a_c = jnp.exp(m_c - m_n)
        l_n = _gather_rows(l_sc, starts, run) * a_o + l_c * a_c
        o_n = _gather_rows(acc, starts, run) * a_o + o_c * a_c
        if mode == "merge":
            _scatter_rows(m_sc, starts, run, m_n)
            _scatter_rows(l_sc, starts, run, l_n)
            _scatter_rows(acc, starts, run, o_n)
        else:
            o_ref[out_rows, :] = o_n / l_n

    def run_dilation(di, mode):
        d = DILATIONS[di]
        runs = N_RES // d
        run = DIL_STEPS // runs
        nblk = SUPER // d // DIL_STEPS

        def body(it, carry):
            c = it // nblk
            n = it % nblk
            starts = [pl.multiple_of((d * b + c) * DIL_STEPS + run * n, SUBLANES) for b in range(runs)]
            pn = jnp.where(n > 0, n - 1, nblk - 1)
            prev_starts = [pl.multiple_of((d * b + c) * DIL_STEPS + run * pn, SUBLANES)
                           for b in range(runs)]
            pslot_n = jnp.where(n > 0, slot, pslot)
            first = jnp.logical_and(sb == 0, n == 0).astype(jnp.int32)
            out_rows = pl.ds(it, DIL_STEPS, stride=N_RES) if mode == "final" else None
            block(di, starts, run, kp.at[pslot_n], vp.at[pslot_n], prev_starts, first, mode, out_rows)
            return carry

        lax.fori_loop(0, d * nblk, body, 0, unroll=ATTN_UNROLL)

    phases = ("init", "merge", "final")
    for t in range(max(samples_per_step, len(phases))):
        if t < samples_per_step:
            sample(t)
        if t < len(phases):
            run_dilation(t, phases[t])


def _attn(qk, vhbc, bias, batch, seq, sqk, svhbc, cache_k, cache_v, sbias, sbias_new):
    nsb = seq // SUPER
    m = batch * seq
    bd, lb = cache_k.shape[0], cache_k.shape[1]
    da = D_ATTN
    n_steps = N_GROUPS * batch * nsb
    assert bd % (2 * n_steps) == 0 and lb % SAMPLE_POS_CHUNK == 0

    def feature_major(c):
        return jnp.transpose(c, (0, 2, 3, 1)).reshape(bd, da, lb)

    def plane(p):
        return pl.BlockSpec((None, None, SUPER, LANES), lambda g, b, s: (p, g, b * nsb + s, 0))

    def splane(p):
        return pl.BlockSpec((None, N_GROUPS, bd, LANES), lambda g, b, s: (p, 0, 0, 0))

    fixed2 = lambda g, b, s: (0, 0)
    bias_spec = pl.BlockSpec((HEADS_PER_GROUP,) + bias.shape[1:], lambda g, b, s: (g, 0, 0, 0, 0),
                             pipeline_mode=pl.Buffered(1))
    hbm = pl.BlockSpec(memory_space=pl.ANY)
    blk_bytes = _nbytes((SUPER, LANES), F32)
    return pl.pallas_call(
        functools.partial(_attn_kernel, samples_per_step=bd // n_steps),
        out_shape=(jax.ShapeDtypeStruct((N_GROUPS, m, LANES), F32),
                   jax.ShapeDtypeStruct((N_GROUPS, bd, LANES), F32)),
        grid=(N_GROUPS, batch, nsb),
        in_specs=[plane(0), plane(1), plane(0), bias_spec,
                  splane(0), splane(1), splane(0), hbm, hbm,
                  pl.BlockSpec(sbias.shape, fixed2), pl.BlockSpec(sbias_new.shape, fixed2)],
        out_specs=(pl.BlockSpec((None, SUPER, LANES), lambda g, b, s: (g, b * nsb + s, 0)),
                   pl.BlockSpec((N_GROUPS, bd, LANES), lambda g, b, s: (0, 0, 0))),
        scratch_shapes=[pltpu.VMEM((SUPER, LANES), F32),
                        pltpu.VMEM((2, SUPER, LANES), F32), pltpu.VMEM((2, SUPER, LANES), F32),
                        pltpu.VMEM((SUPER, LANES), F32), pltpu.VMEM((SUPER, LANES), F32),
                        pltpu.VMEM((SUPER, LANES), F32),
                        pltpu.VMEM((2, da, lb), F32), pltpu.VMEM((2, da, lb), F32),
                        pltpu.SemaphoreType.DMA((2, 2))],
        compiler_params=_cparams(("arbitrary", "arbitrary", "arbitrary"),
                                 (2 * 4 + 8) * blk_bytes
                                 + _nbytes((HEADS_PER_GROUP,) + bias.shape[1:], F32)
                                 + 4 * _nbytes((da, lb), F32) + 2 * 4 * _nbytes((bd, da), F32)
                                 + 2 * _nbytes(sbias.shape, F32)),
        name="attn",
    )(qk, qk, vhbc, bias, sqk, sqk, svhbc, feature_major(cache_k), feature_major(cache_v), sbias, sbias_new)


def _mix_tail(x, a, conv, ga_ref, gc_ref, w_ref):
    cat = jnp.concatenate([_rms(a, ga_ref[...]), _rms(conv, gc_ref[...])], axis=1).astype(BF16)
    return x + jnp.dot(cat, w_ref[...], preferred_element_type=F32)


def _mix_prompt_kernel(x_ref, a_ref, h_ref, b_ref, c_ref, hh_ref, ch_ref, cw_ref, ga_ref, gc_ref, w_ref,
                       o_ref, ut_ref, *, tiles_per_seq):
    tm = x_ref.shape[0]
    seq_start = pl.program_id(0) % tiles_per_seq == 0
    all8 = pl.ds(0, SUBLANES)
    halo = jnp.where(seq_start, 0.0, _cat_groups(hh_ref, all8) * _cat_groups(ch_ref, all8))
    for rows in _sub_tiles(tm, MIX_SUB_ROWS):
        u = _cat_groups(h_ref, rows) * _cat_groups(c_ref, rows)
        rid = lax.broadcasted_iota(jnp.int32, u.shape, 0)
        u1 = jnp.where(rid == 0, halo[7:8], pltpu.roll(u, 1, axis=0))
        u2 = jnp.where(rid == 0, halo[6:7], jnp.where(rid == 1, halo[7:8], pltpu.roll(u, 2, axis=0)))
        cy = cw_ref[0:1] * u2 + cw_ref[1:2] * u1 + cw_ref[2:3] * u
        conv = _cat_groups(b_ref, rows) * cy
        o_ref[rows, :] = _mix_tail(x_ref[rows, :], _cat_groups(a_ref, rows), conv, ga_ref, gc_ref, w_ref)
        halo = u[u.shape[0] - SUBLANES:]
    ut_ref[...] = halo


def _mix_sample_kernel(x_ref, a_ref, h_ref, b_ref, c_ref, buf_ref, cw_ref, ga_ref, gc_ref, w_ref, o_ref, u_ref):
    every = pl.ds(0, x_ref.shape[0])
    u = _cat_groups(h_ref, every) * _cat_groups(c_ref, every)
    dc = u.shape[1]
    cy = cw_ref[0:1] * buf_ref[:, :dc] + cw_ref[1:2] * buf_ref[:, dc:] + cw_ref[2:3] * u
    conv = _cat_groups(b_ref, every) * cy
    o_ref[...] = _mix_tail(x_ref[...], _cat_groups(a_ref, every), conv, ga_ref, gc_ref, w_ref)
    u_ref[...] = u


def _mix_out_prompt(x, attn, vhbc, conv_w, ga, gc, w_out, seq, tm):
    m, dm = x.shape
    dc = conv_w.shape[1]
    fixed = lambda i: (0, 0)
    halo_blocks = tm // SUBLANES

    def plane(p):
        return pl.BlockSpec((None, N_GROUPS, tm, LANES), lambda i: (p, 0, i, 0))

    def halo(p):
        return pl.BlockSpec((None, N_GROUPS, SUBLANES, LANES),
                            lambda i: (p, 0, jnp.maximum(i * halo_blocks - 1, 0), 0))

    tile = _nbytes((tm, dc), F32)
    return pl.pallas_call(
        functools.partial(_mix_prompt_kernel, tiles_per_seq=seq // tm),
        out_shape=(jax.ShapeDtypeStruct((m, dm), F32), jax.ShapeDtypeStruct((m // tm, SUBLANES, dc), F32)),
        grid=(m // tm,),
        in_specs=[pl.BlockSpec((tm, dm), lambda i: (i, 0)),
                  pl.BlockSpec((N_GROUPS, tm, LANES), lambda i: (0, i, 0)),
                  plane(1), plane(2), plane(3), halo(1), halo(3),
                  pl.BlockSpec((CONV_W, dc), fixed), pl.BlockSpec((1, D_ATTN), fixed),
                  pl.BlockSpec((1, dc), fixed), _resident(w_out.shape, fixed)],
        out_specs=(pl.BlockSpec((tm, dm), lambda i: (i, 0)),
                   pl.BlockSpec((None, SUBLANES, dc), lambda i: (i, 0, 0))),
        compiler_params=_cparams(("arbitrary",), 2 * 8 * tile + _nbytes(w_out.shape, BF16)
                                 + 4 * _nbytes((min(tm, MIX_SUB_ROWS), dm), F32)),
        name="mix_out_prompt",
    )(x, attn, vhbc, vhbc, vhbc, vhbc, vhbc, conv_w, ga, gc, w_out)


def _mix_out_sample(x, attn, vhbc, buf, conv_w, ga, gc, w_out):
    m, dm = x.shape
    dc = conv_w.shape[1]
    full = lambda a: pl.BlockSpec(a.shape, lambda i: (0,) * a.ndim)

    def plane(p):
        return pl.BlockSpec((None, N_GROUPS, m, LANES), lambda i: (p, 0, 0, 0))

    return pl.pallas_call(
        _mix_sample_kernel,
        out_shape=(jax.ShapeDtypeStruct((m, dm), F32), jax.ShapeDtypeStruct((m, dc), F32)),
        grid=(1,),
        in_specs=[full(x), full(attn), plane(1), plane(2), plane(3), full(buf), full(conv_w), full(ga),
                  full(gc), full(w_out)],
        out_specs=(pl.BlockSpec((m, dm), lambda i: (0, 0)), pl.BlockSpec((m, dc), lambda i: (0, 0))),
        compiler_params=_cparams(("arbitrary",), 2 * (10 * _nbytes((m, dc), F32) + _nbytes(w_out.shape, BF16))),
        name="mix_out_sample",
    )(x, attn, vhbc, vhbc, vhbc, buf, conv_w, ga, gc, w_out)


def _ffn_kernel(x_ref, g_ref, wg_ref, wu_ref, wd_ref, o_ref, *rest):
    n_sc = rest[-1]
    f = pl.program_id(1)

    @pl.when(f == 0)
    def _():
        x = x_ref[...]
        n_sc[...] = _rms(x, g_ref[...]).astype(BF16)
        o_ref[...] = x

    wg, wu, wd = (w[...].astype(BF16) for w in (wg_ref, wu_ref, wd_ref))
    for dst, w in zip(rest[:-1], (wg, wu, wd)):
        dst[...] = w
    n = n_sc[...]
    gate = jnp.dot(n, wg, preferred_element_type=F32)
    up = jnp.dot(n, wu, preferred_element_type=F32)
    act = (gate / (1.0 + jnp.exp(-gate)) * up).astype(BF16)
    o_ref[...] += jnp.dot(act, wd, preferred_element_type=F32)


def _ffn(x, g, wg, wu, wd, tm, tf, emit_bf16_weights=False):
    m, dm = x.shape
    dff = wg.shape[1]
    row = lambda i, f: (i, 0)
    w_specs = [pl.BlockSpec((dm, tf), lambda i, f: (0, f)), pl.BlockSpec((dm, tf), lambda i, f: (0, f)),
               pl.BlockSpec((tf, dm), lambda i, f: (f, 0))]
    out_shape = [jax.ShapeDtypeStruct((m, dm), F32)]
    out_specs = [pl.BlockSpec((tm, dm), row)]
    if emit_bf16_weights:
        assert m == tm, "weight copies are written once: one row tile"
        out_shape += [jax.ShapeDtypeStruct(w.shape, BF16) for w in (wg, wu, wd)]
        out_specs += w_specs
    out = pl.pallas_call(
        _ffn_kernel,
        out_shape=tuple(out_shape),
        grid=(m // tm, dff // tf),
        in_specs=[pl.BlockSpec((tm, dm), row), pl.BlockSpec((1, dm), lambda i, f: (0, 0))] + w_specs,
        out_specs=tuple(out_specs),
        scratch_shapes=[pltpu.VMEM((tm, dm), BF16)],
        compiler_params=_cparams(("arbitrary", "arbitrary"),
                                 4 * _nbytes((tm, dm), F32) + _nbytes((tm, dm), BF16)
                                 + 2 * 3 * (_nbytes((dm, tf), wg.dtype) + _nbytes((dm, tf), BF16))
                                 + 3 * _nbytes((tm, tf), F32)),
        name="ffn",
    )(x, g, wg, wu, wd)
    return out if emit_bf16_weights else out[0]


def _ple_kernel(x_ref, p_ref, g_ref, wg_ref, wp_ref, o_ref):
    for rows in _sub_tiles(x_ref.shape[0], PLE_SUB_ROWS):
        x = x_ref[rows, :]
        z = jnp.dot(_rms(x, g_ref[...]).astype(BF16), wg_ref[...], preferred_element_type=F32)
        e = jnp.dot(p_ref[rows, :].astype(BF16), wp_ref[...], preferred_element_type=F32)
        o_ref[rows, :] = x + e / (1.0 + jnp.exp(-z))


def _ple(x, p, g, wg, wp, tm):
    m, dm = x.shape
    dp = p.shape[1]
    row = lambda i: (i, 0)
    fixed = lambda i: (0, 0)
    return pl.pallas_call(
        _ple_kernel,
        out_shape=jax.ShapeDtypeStruct((m, dm), F32),
        grid=(m // tm,),
        in_specs=[pl.BlockSpec((tm, dm), row), pl.BlockSpec((tm, dp), row), pl.BlockSpec((1, dm), fixed),
                  _resident(wg.shape, fixed), _resident(wp.shape, fixed)],
        out_specs=pl.BlockSpec((tm, dm), row),
        compiler_params=_cparams(("arbitrary",),
                                 2 * (2 * _nbytes((tm, dm), F32) + _nbytes((tm, dp), F32))
                                 + _nbytes(wg.shape, BF16) + _nbytes(wp.shape, BF16)
                                 + 6 * _nbytes((min(tm, PLE_SUB_ROWS), dm), F32)),
        name="ple",
    )(x, p, g, wg, wp)


def _tiles(m):
    return dict(proj=min(m, 1024), mix=min(m, 512), ffn=min(m, 512), ffn_cols=512, ple=min(m, 1024))


def _window_rows(t, batch, kw):
    return jnp.transpose(t.reshape(batch, N_HEADS, HEAD_DIM, kw), (0, 3, 1, 2))


def kernel(x_prompt, x_sample, p_prompt, p_sample, cache_k, cache_v, state_conv, rel_bias, g_mix, w_in,
           q_norm, k_norm, conv_w, g_attn_out, g_conv_out, w_out, g_ffn, w_gate, w_up, w_down, g_ple,
           w_ple_gate, w_ple_proj):
    depth = g_mix.shape[0]
    batch, seq, dm = x_prompt.shape
    bd, dec_seq, _ = x_sample.shape
    dc = conv_w.shape[2]
    assert depth == 1 and dec_seq == 1, "single layer, one new position per sample"
    assert seq % SUPER == 0 and dm == D_ATTN + dc and dc == D_ATTN
    assert w_gate.shape[2] % 512 == 0

    bias_p, bias_s, bias_s_new = _bias_tables(rel_bias, cache_k.shape[2])
    gmat = jnp.asarray(np.kron(np.eye(PROJ_COLS // HEAD_DIM, dtype=np.float32),
                               np.full((HEAD_DIM, HEAD_DIM), 1.0 / HEAD_DIM, np.float32)), BF16)

    i = 0
    row2 = lambda a: a.reshape(1, -1).astype(F32)
    g_mix_i, g_ffn_i, g_ple_i = row2(g_mix[i]), row2(g_ffn[i]), row2(g_ple[i])
    qk_gains = jnp.stack([row2(jnp.tile(q_norm[i], N_HEADS) * SCALE), row2(jnp.tile(k_norm[i], N_HEADS))])
    ga, gc = row2(g_attn_out[i]), row2(g_conv_out[i])
    cw = conv_w[i].astype(F32)
    w_in_i, w_out_i = w_in[i], w_out[i].astype(BF16)
    wpg_i, wpp_i = w_ple_gate[i].astype(BF16), w_ple_proj[i].astype(BF16)

    mp = batch * seq
    tp, ts = _tiles(mp), _tiles(bd)
    kw = min(SUPER, seq)
    xp = x_prompt.reshape(mp, dm)
    xs = x_sample.reshape(bd, dm)
    qks, kts = _proj(xs, g_mix_i, w_in_i, 0, 2, 1, bd, bd, ts["proj"], qk_gains, gmat)
    vhbcs, vts = _proj(xs, g_mix_i, w_in_i, 2, 4, 0, bd, bd, ts["proj"])
    qk, kt = _proj(xp, g_mix_i, w_in_i, 0, 2, 1, seq, kw, tp["proj"], qk_gains, gmat)
    vhbc, vt = _proj(xp, g_mix_i, w_in_i, 2, 4, 0, seq, kw, tp["proj"])
    attn, attn_s = _attn(qk, vhbc, bias_p, batch, seq, qks, vhbcs, cache_k[i], cache_v[i], bias_s, bias_s_new)

    buf = state_conv[i].astype(F32)
    hs, us = _mix_out_sample(xs, attn_s, vhbcs, buf.reshape(bd, (CONV_W - 1) * dc), cw, ga, gc, w_out_i)
    hs, wg_i, wu_i, wd_i = _ffn(hs, g_ffn_i, w_gate[i], w_up[i], w_down[i], ts["ffn"], ts["ffn_cols"],
                                emit_bf16_weights=True)
    hs = _ple(hs, p_sample[i].reshape(bd, -1), g_ple_i, wpg_i, wpp_i, ts["ple"])

    h, u_tail = _mix_out_prompt(xp, attn, vhbc, cw, ga, gc, w_out_i, seq, tp["mix"])
    h = _ffn(h, g_ffn_i, wg_i, wu_i, wd_i, tp["ffn"], tp["ffn_cols"])
    h = _ple(h, p_prompt[i].reshape(mp, -1), g_ple_i, wpg_i, wpp_i, tp["ple"])
    y_prompt = h.reshape(batch, seq, dm)
    k_prompt = _window_rows(kt, batch, kw)[None]
    v_prompt = _window_rows(vt, batch, kw)[None]
    tiles_per_seq = seq // tp["mix"]
    conv_prompt = u_tail.reshape(batch, tiles_per_seq, SUBLANES, dc)[None, :, -1, SUBLANES - (CONV_W - 1):]

    y_sample = hs.reshape(bd, dec_seq, dm)
    k_sample = _window_rows(kts, 1, bd).reshape(1, bd, dec_seq, N_HEADS, HEAD_DIM)
    v_sample = _window_rows(vts, 1, bd).reshape(1, bd, dec_seq, N_HEADS, HEAD_DIM)
    conv_sample = jnp.concatenate([buf[:, 1:], us[:, None, :]], axis=1)[None]

    return (y_prompt, y_sample, k_prompt, v_prompt, conv_prompt, k_sample, v_sample, conv_sample)
```

```python
import functools
import math

import numpy as np
import jax
import jax.numpy as jnp
from jax import lax
from jax.experimental import pallas as pl
from jax.experimental.pallas import tpu as pltpu

HEAD_DIM = 64
N_HEADS = 16
D_ATTN = N_HEADS * HEAD_DIM
CONV_W = 3
DIL_STEPS = 128
DILATIONS = (1, 4, 16)
N_BUCKETS = 32
MAX_EXACT = N_BUCKETS // 2
MAX_DIST = DIL_STEPS * max(DILATIONS)
EPS = 1e-6
SCALE = HEAD_DIM ** -0.5

LANES = 128
SUBLANES = 8
BF16_ROWS = 16
HEADS_PER_GROUP = LANES // HEAD_DIM
N_GROUPS = N_HEADS // HEADS_PER_GROUP
SUPER = DIL_STEPS * max(DILATIONS)
N_RES = max(DILATIONS)
MASK_VALUE = -1e30
MASK_BUCKET = N_BUCKETS
TABLE_ROWS = -(-(N_BUCKETS + 1) // BF16_ROWS) * BF16_ROWS
TABLE_CHUNK = 2048
SAMPLE_POS_CHUNK = 512
ATTN_UNROLL = 16
PROJ_COLS = 512
PROJ_SPLIT = D_ATTN // PROJ_COLS
MIX_SUB_ROWS = 256
PLE_SUB_ROWS = 512
FFN_SUB_COLS = 256
VMEM_SLACK_BYTES = 8 * 1024 * 1024

F32 = jnp.float32
BF16 = jnp.bfloat16


def _cparams(sem, vmem_bytes):
    return pltpu.CompilerParams(dimension_semantics=sem,
                                vmem_limit_bytes=int(vmem_bytes + VMEM_SLACK_BYTES))


def _nbytes(shape, dtype):
    return int(np.prod(shape)) * jnp.dtype(dtype).itemsize


def _resident(shape, index_map):
    return pl.BlockSpec(shape, index_map, pipeline_mode=pl.Buffered(1))


def _t5_bucket(dist):
    n = np.asarray(dist, np.int32)
    nf = np.maximum(n, 1).astype(np.float32)
    large = MAX_EXACT + (np.log(nf / MAX_EXACT) / np.float32(math.log(MAX_DIST / MAX_EXACT))
                         * (N_BUCKETS - MAX_EXACT)).astype(np.int32)
    large = np.minimum(large, N_BUCKETS - 1)
    return np.where(n < MAX_EXACT, n, large).astype(np.int32)


def _stored_to_natural(d):
    runs = N_RES // d
    run = DIL_STEPS // runs
    j = np.arange(DIL_STEPS)
    return (j % run) * runs + j // run


def _prompt_bias_index():
    out = np.empty((len(DILATIONS), 2, DIL_STEPS, 2 * DIL_STEPS), np.int32)
    for di, d in enumerate(DILATIONS):
        nat = _stored_to_natural(d)
        qi = nat[:, None]
        kj = np.concatenate([nat, nat + DIL_STEPS])[None, :]
        steps = DIL_STEPS + qi - kj
        band = (steps >= 0) & (steps <= DIL_STEPS)
        bucket = _t5_bucket(d * np.clip(steps, 0, DIL_STEPS))
        out[di, 0] = np.where(band, bucket, MASK_BUCKET)
        out[di, 1] = np.where(band & (kj >= DIL_STEPS), bucket, MASK_BUCKET)
    return out


def _sample_bias_index(lb):
    back = lb - np.arange(lb)
    cached = np.stack([np.where((back % d == 0) & (back // d <= DIL_STEPS), _t5_bucket(back), MASK_BUCKET)
                       for d in DILATIONS])
    new = np.stack([_t5_bucket(d * np.zeros(1, np.int32)) for d in DILATIONS])
    return cached, new


def _bias_table_kernel(tab_ref, idx_ref, o_ref):
    t = tab_ref[...]
    hi = t.astype(BF16)
    r1 = t - hi.astype(F32)
    mid = r1.astype(BF16)
    lo = (r1 - mid.astype(F32)).astype(BF16)
    rows = lax.broadcasted_iota(jnp.int32, (TABLE_ROWS, TABLE_CHUNK), 0)

    def chunk(c, carry):
        sl = pl.ds(pl.multiple_of(c * TABLE_CHUNK, TABLE_CHUNK), TABLE_CHUNK)
        onehot = jnp.where(rows == idx_ref[:, sl], 1.0, 0.0).astype(BF16)
        acc = jnp.dot(hi, onehot, preferred_element_type=F32)
        acc = acc + jnp.dot(mid, onehot, preferred_element_type=F32)
        acc = acc + jnp.dot(lo, onehot, preferred_element_type=F32)
        o_ref[:, sl] = acc
        return carry

    lax.fori_loop(0, o_ref.shape[1] // TABLE_CHUNK, chunk, 0)


def _bias_tables(rel_bias, lb):
    pidx = _prompt_bias_index()
    cached, new = _sample_bias_index(lb)
    flat = np.concatenate([pidx.reshape(-1), cached.reshape(-1), new.reshape(-1)])
    n_pad = -(-(pidx.size + cached.size + LANES) // TABLE_CHUNK) * TABLE_CHUNK
    idx = np.full((1, n_pad), MASK_BUCKET, np.int32)
    idx[0, :flat.size] = flat
    tab = jnp.concatenate(
        [rel_bias.astype(F32).T,
         jnp.full((N_HEADS, 1), MASK_VALUE, F32),
         jnp.zeros((N_HEADS, TABLE_ROWS - N_BUCKETS - 1), F32)], axis=1)
    out = pl.pallas_call(
        _bias_table_kernel,
        out_shape=jax.ShapeDtypeStruct((N_HEADS, n_pad), F32),
        grid=(1,),
        in_specs=[pl.BlockSpec((N_HEADS, TABLE_ROWS), lambda i: (0, 0)),
                  pl.BlockSpec((1, n_pad), lambda i: (0, 0))],
        out_specs=pl.BlockSpec((N_HEADS, n_pad), lambda i: (0, 0)),
        compiler_params=_cparams(("arbitrary",), 2 * (_nbytes((N_HEADS, n_pad), F32)
                                                      + _nbytes((SUBLANES, n_pad), jnp.int32))),
        name="bias_table",
    )(tab, jnp.asarray(idx))
    n_p = pidx.size
    prompt = out[:, :n_p].reshape(N_HEADS, len(DILATIONS), 2, DIL_STEPS, 2 * DIL_STEPS)
    samp = out[:, n_p:n_p + cached.size]
    samp_new = out[:, n_p + cached.size:n_p + cached.size + LANES]
    return prompt, samp, samp_new


def _rms(x, g):
    return x * lax.rsqrt(jnp.mean(x * x, axis=-1, keepdims=True) + EPS) * g


def _head_rms(p, g, gmat):
    ms = jnp.dot((p * p).astype(BF16), gmat, preferred_element_type=F32)
    return p * lax.rsqrt(ms + EPS) * g


def _cat_groups(ref, rows):
    return jnp.concatenate([ref[g, rows, :] for g in range(N_GROUPS)], axis=1)


def _sub_tiles(tm, sub):
    sub = min(tm, sub)
    return [pl.ds(r, sub) for r in range(0, tm, sub)]


def _proj_kernel(*refs, normed, emit_w, t_plane, tiles_per_seq, first_kept):
    refs = list(refs)
    n_sc = refs.pop()
    wcopy_ref = refs.pop() if emit_w else None
    if normed:
        x_ref, g_ref, w_ref, gain_ref, gmat_ref, o_ref, t_ref = refs
    else:
        x_ref, g_ref, w_ref, o_ref, t_ref = refs
    i, j = pl.program_id(0), pl.program_id(1)

    @pl.when(j == 0)
    def _():
        n_sc[...] = _rms(x_ref[...], g_ref[...]).astype(BF16)

    w = w_ref[...].astype(BF16)
    if emit_w:
        wcopy_ref[...] = w
    y = jnp.dot(n_sc[...], w, preferred_element_type=F32)
    if normed:
        y = _head_rms(y, gain_ref[...], gmat_ref[...])
    for g in range(PROJ_COLS // LANES):
        o_ref[g] = y[:, g * LANES:(g + 1) * LANES]

    @pl.when(jnp.logical_and(j // PROJ_SPLIT == t_plane, i % tiles_per_seq >= first_kept))
    def _():
        t_ref[...] = y.T


def _proj(x, g_mix, w_in, plane0, n_planes, t_plane, seq, kw, tm, gains=None, gmat=None, emit_bf16_weights=False):
    m, dm = x.shape
    da = D_ATTN
    assert seq % tm == 0 and kw % tm == 0 and m % seq == 0
    assert not emit_bf16_weights or m == tm, "weight copies are written once: one row tile"
    tps, first_kept, batch = seq // tm, (seq - kw) // tm, m // seq
    normed = gains is not None
    grp = PROJ_COLS // LANES
    in_specs = [pl.BlockSpec((tm, dm), lambda i, j: (i, 0)), pl.BlockSpec((1, dm), lambda i, j: (0, 0)),
                pl.BlockSpec((dm, PROJ_COLS), lambda i, j: (0, plane0 * PROJ_SPLIT + j))]
    args = [x, g_mix, w_in]
    if normed:
        in_specs += [pl.BlockSpec((None, 1, PROJ_COLS), lambda i, j: (j, 0, 0)),
                     _resident((PROJ_COLS, PROJ_COLS), lambda i, j: (0, 0))]
        args += [gains.reshape(n_planes * PROJ_SPLIT, 1, PROJ_COLS), gmat]

    def t_index(i, j):
        kept = i % tps >= first_kept
        half = jnp.where(kept, jnp.clip(j - t_plane * PROJ_SPLIT, 0, PROJ_SPLIT - 1), 0)
        return (i // tps, half, jnp.maximum(i % tps - first_kept, 0))

    out_specs = [pl.BlockSpec((None, grp, tm, LANES), lambda i, j: (j // PROJ_SPLIT, j % PROJ_SPLIT, i, 0)),
                 pl.BlockSpec((None, PROJ_COLS, tm), t_index)]
    out_shape = [jax.ShapeDtypeStruct((n_planes, N_GROUPS, m, LANES), F32),
                 jax.ShapeDtypeStruct((batch, da, kw), F32)]
    if emit_bf16_weights:
        out_specs.append(pl.BlockSpec((dm, PROJ_COLS), lambda i, j: (0, j)))
        out_shape.append(jax.ShapeDtypeStruct((dm, n_planes * da), BF16))
    vmem = (2 * (_nbytes((tm, dm), F32) + _nbytes((dm, PROJ_COLS), w_in.dtype) + _nbytes((dm, PROJ_COLS), BF16)
                 + 2 * _nbytes((tm, PROJ_COLS), F32))
            + _nbytes((tm, dm), BF16) + 6 * _nbytes((tm, PROJ_COLS), F32))
    return pl.pallas_call(
        functools.partial(_proj_kernel, normed=normed, emit_w=emit_bf16_weights, t_plane=t_plane,
                          tiles_per_seq=tps, first_kept=first_kept),
        out_shape=tuple(out_shape),
        grid=(m // tm, n_planes * PROJ_SPLIT),
        in_specs=in_specs,
        out_specs=tuple(out_specs),
        scratch_shapes=[pltpu.VMEM((tm, dm), BF16)],
        compiler_params=_cparams(("arbitrary", "arbitrary"), vmem),
        name="in_proj_qk" if normed else "in_proj_vhbc",
    )(*args)


def _gather_rows(ref, starts, run):
    parts = [ref[pl.ds(s, run), :] for s in starts]
    return parts[0] if len(parts) == 1 else jnp.concatenate(parts, axis=0)


def _scatter_rows(ref, starts, run, val):
    for i, s in enumerate(starts):
        ref[pl.ds(s, run), :] = val[i * run:(i + 1) * run]


def _sample_scores(row, q_ref, kn_ref, kt_ref, bias_ref, bnew_ref):
    lb = kt_ref.shape[1]
    nd = len(DILATIONS)
    seg = (lax.broadcasted_iota(jnp.int32, (N_HEADS, D_ATTN), 1) // HEAD_DIM
           == lax.broadcasted_iota(jnp.int32, (N_HEADS, D_ATTN), 0))
    qe = jnp.where(seg, _cat_groups(q_ref, row), 0.0).astype(BF16)
    kn = _cat_groups(kn_ref, row).astype(BF16).astype(F32)
    s_new = jnp.sum(qe.astype(F32) * kn, axis=1, keepdims=True)
    sn = [s_new + bnew_ref[:, di:di + 1] for di in range(nd)]
    starts = range(0, lb, SAMPLE_POS_CHUNK)
    chunks = [pl.ds(c, SAMPLE_POS_CHUNK) for c in starts]
    sc = []
    for c, ch in zip(starts, chunks):
        s = jnp.dot(qe, kt_ref[:, ch].astype(BF16), preferred_element_type=F32)
        sc.append([s + bias_ref[:, pl.ds(di * lb + c, SAMPLE_POS_CHUNK)] for di in range(nd)])
    m = functools.reduce(jnp.maximum, [jnp.max(x, axis=1, keepdims=True) for row_ in sc for x in row_] + sn)
    pn = functools.reduce(jnp.add, [jnp.exp(x - m) for x in sn])
    ps = [functools.reduce(jnp.add, [jnp.exp(x - m) for x in sd]) for sd in sc]
    return seg, chunks, ps, pn


def _sample_output(row, state, vn_ref, vt_ref, o_ref):
    seg, chunks, ps, pn = state
    den = pn
    o = pn * _cat_groups(vn_ref, row)
    for ch, p in zip(chunks, ps):
        den = den + jnp.sum(p, axis=1, keepdims=True)
        o = o + lax.dot_general(p.astype(BF16), vt_ref[:, ch].astype(BF16),
                                (((1,), (1,)), ((), ())), preferred_element_type=F32)
    out = jnp.sum(jnp.where(seg, o / den, 0.0), axis=0, keepdims=True)
    for g in range(N_GROUPS):
        o_ref[g, row, :] = out[:, g * LANES:(g + 1) * LANES]


def _attn_kernel(q_ref, k_ref, v_ref, bias_ref, sq_ref, skn_ref, svn_ref, ck_hbm, cv_hbm, sbias_ref,
                 sbnew_ref, o_ref, so_ref, qp, kp, vp, acc, m_sc, l_sc, kbuf, vbuf, sem, *, samples_per_step):
    sb = pl.program_id(2)
    step = (pl.program_id(0) * pl.num_programs(1) + pl.program_id(1)) * pl.num_programs(2) + sb
    n_samples = ck_hbm.shape[0]
    slot = sb % 2
    pslot = 1 - slot
    kcur, vcur = kp.at[slot], vp.at[slot]
    kprev, vprev = kp.at[pslot], vp.at[pslot]

    def copies(g, slot):
        return (pltpu.make_async_copy(ck_hbm.at[g], kbuf.at[slot], sem.at[0, slot]),
                pltpu.make_async_copy(cv_hbm.at[g], vbuf.at[slot], sem.at[1, slot]))

    @pl.when(step == 0)
    def _():
        for c in copies(0, 0):
            c.start()

    def sample(t):
        g = step * samples_per_step + t
        slot = t % 2
        row = pl.ds(g, 1)

        @pl.when(g + 1 < n_samples)
        def _():
            for c in copies(g + 1, 1 - slot):
                c.start()

        k_in, v_in = copies(g, slot)
        k_in.wait()
        state = _sample_scores(row, sq_ref, skn_ref, kbuf.at[slot], sbias_ref, sbnew_ref)
        v_in.wait()
        _sample_output(row, state, svn_ref, vbuf.at[slot], so_ref)

    for r in range(N_RES):
        rows = pl.ds(r * DIL_STEPS, DIL_STEPS)
        src = pl.ds(r, DIL_STEPS, stride=N_RES)
        qp[rows, :] = q_ref[src, :]
        kcur[rows, :] = k_ref[src, :]
        vcur[rows, :] = v_ref[src, :]

    @pl.when(sb == 0)
    def _():
        kprev[...] = jnp.zeros(kprev.shape, F32)
        vprev[...] = jnp.zeros(vprev.shape, F32)

    head0 = lax.broadcasted_iota(jnp.int32, (DIL_STEPS, LANES), 1) < HEAD_DIM

    def block(di, starts, run, prev_ref_k, prev_ref_v, prev_starts, first, mode, out_rows=None):
        qb = _gather_rows(qp, starts, run)
        kb = jnp.concatenate([_gather_rows(prev_ref_k, prev_starts, run),
                              _gather_rows(kcur, starts, run)], axis=0).astype(BF16)
        vb = jnp.concatenate([_gather_rows(prev_ref_v, prev_starts, run),
                              _gather_rows(vcur, starts, run)], axis=0).astype(BF16)
        ms, ls, os_ = [], [], []
        for h in range(HEADS_PER_GROUP):
            keep = head0 if h == 0 else jnp.logical_not(head0)
            qh = jnp.where(keep, qb, 0.0).astype(BF16)
            s = lax.dot_general(qh, kb, (((1,), (1,)), ((), ())), preferred_element_type=F32)
            s = s + bias_ref[h, di, first]
            mh = jnp.max(s, axis=1, keepdims=True)
            p = jnp.exp(s - mh)
            ls.append(jnp.sum(p, axis=1, keepdims=True))
            os_.append(jnp.dot(p.astype(BF16), vb, preferred_element_type=F32))
            ms.append(mh)
        m_c = jnp.where(head0, ms[0], ms[1])
        l_c = jnp.where(head0, ls[0], ls[1])
        o_c = jnp.where(head0, os_[0], os_[1])
        if mode == "init":
            _scatter_rows(m_sc, starts, run, m_c)
            _scatter_rows(l_sc, starts, run, l_c)
            _scatter_rows(acc, starts, run, o_c)
            return
        m_o = _gather_rows(m_sc, starts, run)
        m_n = jnp.maximum(m_o, m_c)
        a_o = jnp.exp(m_o - m_n)
        a_c = jnp.exp(m_c - m_n)
        l_n = _gather_rows(l_sc, starts, run) * a_o + l_c * a_c
        o_n = _gather_rows(acc, starts, run) * a_o + o_c * a_c
        if mode == "merge":
            _scatter_rows(m_sc, starts, run, m_n)
            _scatter_rows(l_sc, starts, run, l_n)
            _scatter_rows(acc, starts, run, o_n)
        else:
            o_ref[out_rows, :] = o_n / l_n

    def run_dilation(di, mode):
        d = DILATIONS[di]
        runs = N_RES // d
        run = DIL_STEPS // runs
        nblk = SUPER // d // DIL_STEPS

        def body(it, carry):
            c = it // nblk
            n = it % nblk
            starts = [pl.multiple_of((d * b + c) * DIL_STEPS + run * n, SUBLANES) for b in range(runs)]
            pn = jnp.where(n > 0, n - 1, nblk - 1)
            prev_starts = [pl.multiple_of((d * b + c) * DIL_STEPS + run * pn, SUBLANES)
                           for b in range(runs)]
            pslot_n = jnp.where(n > 0, slot, pslot)
            first = jnp.logical_and(sb == 0, n == 0).astype(jnp.int32)
            out_rows = pl.ds(it, DIL_STEPS, stride=N_RES) if mode == "final" else None
            block(di, starts, run, kp.at[pslot_n], vp.at[pslot_n], prev_starts, first, mode, out_rows)
            return carry

        lax.fori_loop(0, d * nblk, body, 0, unroll=ATTN_UNROLL)

    phases = ("init", "merge", "final")
    for t in range(max(samples_per_step, len(phases))):
        if t < samples_per_step:
            sample(t)
        if t < len(phases):
            run_dilation(t, phases[t])


def _attn(qk, vhbc, bias, batch, seq, sqk, svhbc, cache_k, cache_v, sbias, sbias_new):
    nsb = seq // SUPER
    m = batch * seq
    bd, lb = cache_k.shape[0], cache_k.shape[1]
    da = D_ATTN
    n_steps = N_GROUPS * batch * nsb
    assert bd % (2 * n_steps) == 0 and lb % SAMPLE_POS_CHUNK == 0

    def feature_major(c):
        return jnp.transpose(c, (0, 2, 3, 1)).reshape(bd, da, lb)

    def plane(p):
        return pl.BlockSpec((None, None, SUPER, LANES), lambda g, b, s: (p, g, b * nsb + s, 0))

    def splane(p):
        return pl.BlockSpec((None, N_GROUPS, bd, LANES), lambda g, b, s: (p, 0, 0, 0))

    fixed2 = lambda g, b, s: (0, 0)
    bias_spec = pl.BlockSpec((HEADS_PER_GROUP,) + bias.shape[1:], lambda g, b, s: (g, 0, 0, 0, 0),
                             pipeline_mode=pl.Buffered(1))
    hbm = pl.BlockSpec(memory_space=pl.ANY)
    blk_bytes = _nbytes((SUPER, LANES), F32)
    return pl.pallas_call(
        functools.partial(_attn_kernel, samples_per_step=bd // n_steps),
        out_shape=(jax.ShapeDtypeStruct((N_GROUPS, m, LANES), F32),
                   jax.ShapeDtypeStruct((N_GROUPS, bd, LANES), F32)),
        grid=(N_GROUPS, batch, nsb),
        in_specs=[plane(0), plane(1), plane(0), bias_spec,
                  splane(0), splane(1), splane(0), hbm, hbm,
                  pl.BlockSpec(sbias.shape, fixed2), pl.BlockSpec(sbias_new.shape, fixed2)],
        out_specs=(pl.BlockSpec((None, SUPER, LANES), lambda g, b, s: (g, b * nsb + s, 0)),
                   pl.BlockSpec((N_GROUPS, bd, LANES), lambda g, b, s: (0, 0, 0))),
        scratch_shapes=[pltpu.VMEM((SUPER, LANES), F32),
                        pltpu.VMEM((2, SUPER, LANES), F32), pltpu.VMEM((2, SUPER, LANES), F32),
                        pltpu.VMEM((SUPER, LANES), F32), pltpu.VMEM((SUPER, LANES), F32),
                        pltpu.VMEM((SUPER, LANES), F32),
                        pltpu.VMEM((2, da, lb), F32), pltpu.VMEM((2, da, lb), F32),
                        pltpu.SemaphoreType.DMA((2, 2))],
        compiler_params=_cparams(("arbitrary", "arbitrary", "arbitrary"),
                                 (2 * 4 + 8) * blk_bytes
                                 + _nbytes((HEADS_PER_GROUP,) + bias.shape[1:], F32)
                                 + 4 * _nbytes((da, lb), F32) + 2 * 4 * _nbytes((bd, da), F32)
                                 + 2 * _nbytes(sbias.shape, F32)),
        name="attn",
    )(qk, qk, vhbc, bias, sqk, sqk, svhbc, feature_major(cache_k), feature_major(cache_v), sbias, sbias_new)


def _mix_tail(x, a, conv, ga_ref, gc_ref, w_ref):
    cat = jnp.concatenate([_rms(a, ga_ref[...]), _rms(conv, gc_ref[...])], axis=1).astype(BF16)
    return x + jnp.dot(cat, w_ref[...], preferred_element_type=F32)


def _mix_prompt_kernel(x_ref, a_ref, h_ref, b_ref, c_ref, hh_ref, ch_ref, cw_ref, ga_ref, gc_ref, gf_ref, w_ref,
                       o_ref, n_ref, ut_ref, *, tiles_per_seq):
    tm = x_ref.shape[0]
    seq_start = pl.program_id(0) % tiles_per_seq == 0
    all8 = pl.ds(0, SUBLANES)
    halo = jnp.where(seq_start, 0.0, _cat_groups(hh_ref, all8) * _cat_groups(ch_ref, all8))
    for rows in _sub_tiles(tm, MIX_SUB_ROWS):
        u = _cat_groups(h_ref, rows) * _cat_groups(c_ref, rows)
        rid = lax.broadcasted_iota(jnp.int32, u.shape, 0)
        u1 = jnp.where(rid == 0, halo[7:8], pltpu.roll(u, 1, axis=0))
        u2 = jnp.where(rid == 0, halo[6:7], jnp.where(rid == 1, halo[7:8], pltpu.roll(u, 2, axis=0)))
        cy = cw_ref[0:1] * u2 + cw_ref[1:2] * u1 + cw_ref[2:3] * u
        conv = _cat_groups(b_ref, rows) * cy
        h1 = _mix_tail(x_ref[rows, :], _cat_groups(a_ref, rows), conv, ga_ref, gc_ref, w_ref)
        o_ref[rows, :] = h1
        n_ref[rows, :] = _rms(h1, gf_ref[...]).astype(BF16)
        halo = u[u.shape[0] - SUBLANES:]
    ut_ref[...] = halo


def _mix_sample_kernel(x_ref, a_ref, h_ref, b_ref, c_ref, buf_ref, cw_ref, ga_ref, gc_ref, gf_ref, w_ref,
                       o_ref, n_ref, u_ref):
    every = pl.ds(0, x_ref.shape[0])
    u = _cat_groups(h_ref, every) * _cat_groups(c_ref, every)
    dc = u.shape[1]
    cy = cw_ref[0:1] * buf_ref[:, :dc] + cw_ref[1:2] * buf_ref[:, dc:] + cw_ref[2:3] * u
    conv = _cat_groups(b_ref, every) * cy
    h1 = _mix_tail(x_ref[...], _cat_groups(a_ref, every), conv, ga_ref, gc_ref, w_ref)
    o_ref[...] = h1
    n_ref[...] = _rms(h1, gf_ref[...]).astype(BF16)
    u_ref[...] = u


def _mix_out_prompt(x, attn, vhbc, conv_w, ga, gc, gf, w_out, seq, tm):
    m, dm = x.shape
    dc = conv_w.shape[1]
    fixed = lambda i: (0, 0)
    halo_blocks = tm // SUBLANES

    def plane(p):
        return pl.BlockSpec((None, N_GROUPS, tm, LANES), lambda i: (p, 0, i, 0))

    def halo(p):
        return pl.BlockSpec((None, N_GROUPS, SUBLANES, LANES),
                            lambda i: (p, 0, jnp.maximum(i * halo_blocks - 1, 0), 0))

    tile = _nbytes((tm, dc), F32)
    return pl.pallas_call(
        functools.partial(_mix_prompt_kernel, tiles_per_seq=seq // tm),
        out_shape=(jax.ShapeDtypeStruct((m, dm), F32), jax.ShapeDtypeStruct((m, dm), BF16),
                   jax.ShapeDtypeStruct((m // tm, SUBLANES, dc), F32)),
        grid=(m // tm,),
        in_specs=[pl.BlockSpec((tm, dm), lambda i: (i, 0)),
                  pl.BlockSpec((N_GROUPS, tm, LANES), lambda i: (0, i, 0)),
                  plane(1), plane(2), plane(3), halo(1), halo(3),
                  pl.BlockSpec((CONV_W, dc), fixed), pl.BlockSpec((1, D_ATTN), fixed),
                  pl.BlockSpec((1, dc), fixed), pl.BlockSpec((1, dm), fixed), _resident(w_out.shape, fixed)],
        out_specs=(pl.BlockSpec((tm, dm), lambda i: (i, 0)), pl.BlockSpec((tm, dm), lambda i: (i, 0)),
                   pl.BlockSpec((None, SUBLANES, dc), lambda i: (i, 0, 0))),
        compiler_params=_cparams(("arbitrary",), 2 * 9 * tile + _nbytes(w_out.shape, BF16)
                                 + 4 * _nbytes((min(tm, MIX_SUB_ROWS), dm), F32)),
        name="mix_out_prompt",
    )(x, attn, vhbc, vhbc, vhbc, vhbc, vhbc, conv_w, ga, gc, gf, w_out)


def _mix_out_sample(x, attn, vhbc, buf, conv_w, ga, gc, gf, w_out):
    m, dm = x.shape
    dc = conv_w.shape[1]
    full = lambda a: pl.BlockSpec(a.shape, lambda i: (0,) * a.ndim)

    def plane(p):
        return pl.BlockSpec((None, N_GROUPS, m, LANES), lambda i: (p, 0, 0, 0))

    return pl.pallas_call(
        _mix_sample_kernel,
        out_shape=(jax.ShapeDtypeStruct((m, dm), F32), jax.ShapeDtypeStruct((m, dm), BF16),
                   jax.ShapeDtypeStruct((m, dc), F32)),
        grid=(1,),
        in_specs=[full(x), full(attn), plane(1), plane(2), plane(3), full(buf), full(conv_w), full(ga),
                  full(gc), full(gf), full(w_out)],
        out_specs=(pl.BlockSpec((m, dm), lambda i: (0, 0)), pl.BlockSpec((m, dm), lambda i: (0, 0)),
                   pl.BlockSpec((m, dc), lambda i: (0, 0))),
        compiler_params=_cparams(("arbitrary",), 2 * (11 * _nbytes((m, dc), F32) + _nbytes(w_out.shape, BF16))),
        name="mix_out_sample",
    )(x, attn, vhbc, vhbc, vhbc, buf, conv_w, ga, gc, gf, w_out)


def _ffn_kernel(n_ref, wg_ref, wu_ref, wd_ref, o_ref, *wcopy_refs):
    @pl.when(pl.program_id(1) == 0)
    def _():
        o_ref[...] = jnp.zeros(o_ref.shape, F32)

    n = n_ref[...]
    acc = None
    for c in range(0, wg_ref.shape[1], FFN_SUB_COLS):
        cols = pl.ds(c, FFN_SUB_COLS)
        wg, wu, wd = wg_ref[:, cols].astype(BF16), wu_ref[:, cols].astype(BF16), wd_ref[cols, :].astype(BF16)
        if wcopy_refs:
            wcopy_refs[0][:, cols], wcopy_refs[1][:, cols], wcopy_refs[2][cols, :] = wg, wu, wd
        gate = jnp.dot(n, wg, preferred_element_type=F32)
        up = jnp.dot(n, wu, preferred_element_type=F32)
        act = (gate / (1.0 + jnp.exp(-gate)) * up).astype(BF16)
        part = jnp.dot(act, wd, preferred_element_type=F32)
        acc = part if acc is None else acc + part
    o_ref[...] += acc


def _ffn(n, wg, wu, wd, tm, tf, emit_bf16_weights=False):
    m, dm = n.shape
    dff = wg.shape[1]
    row = lambda i, f: (i, 0)
    w_specs = [pl.BlockSpec((dm, tf), lambda i, f: (0, f)), pl.BlockSpec((dm, tf), lambda i, f: (0, f)),
               pl.BlockSpec((tf, dm), lambda i, f: (f, 0))]
    out_shape = [jax.ShapeDtypeStruct((m, dm), F32)]
    out_specs = [pl.BlockSpec((tm, dm), row)]
    if emit_bf16_weights:
        assert m == tm, "weight copies are written once: one row tile"
        out_shape += [jax.ShapeDtypeStruct(w.shape, BF16) for w in (wg, wu, wd)]
        out_specs += w_specs
    out = pl.pallas_call(
        _ffn_kernel,
        out_shape=tuple(out_shape),
        grid=(m // tm, dff // tf),
        in_specs=[pl.BlockSpec((tm, dm), row)] + w_specs,
        out_specs=tuple(out_specs),
        compiler_params=_cparams(("arbitrary", "arbitrary"),
                                 2 * (_nbytes((tm, dm), F32) + _nbytes((tm, dm), BF16))
                                 + 2 * 3 * (_nbytes((dm, tf), wg.dtype)
                                            + (_nbytes((dm, tf), BF16) if emit_bf16_weights else 0))
                                 + 4 * _nbytes((tm, tf), F32)),
        name="ffn",
    )(n, wg, wu, wd)
    return out if emit_bf16_weights else out[0]


def _ple_kernel(h_ref, d_ref, p_ref, g_ref, wg_ref, wp_ref, o_ref):
    for rows in _sub_tiles(h_ref.shape[0], PLE_SUB_ROWS):
        x = h_ref[rows, :] + d_ref[rows, :]
        z = jnp.dot(_rms(x, g_ref[...]).astype(BF16), wg_ref[...], preferred_element_type=F32)
        e = jnp.dot(p_ref[rows, :].astype(BF16), wp_ref[...], preferred_element_type=F32)
        o_ref[rows, :] = x + e / (1.0 + jnp.exp(-z))


def _ple(h, d, p, g, wg, wp, tm):
    m, dm = h.shape
    dp = p.shape[1]
    row = lambda i: (i, 0)
    fixed = lambda i: (0, 0)
    return pl.pallas_call(
        _ple_kernel,
        out_shape=jax.ShapeDtypeStruct((m, dm), F32),
        grid=(m // tm,),
        in_specs=[pl.BlockSpec((tm, dm), row), pl.BlockSpec((tm, dm), row), pl.BlockSpec((tm, dp), row),
                  pl.BlockSpec((1, dm), fixed), _resident(wg.shape, fixed), _resident(wp.shape, fixed)],
        out_specs=pl.BlockSpec((tm, dm), row),
        compiler_params=_cparams(("arbitrary",),
                                 2 * (3 * _nbytes((tm, dm), F32) + _nbytes((tm, dp), F32))
                                 + _nbytes(wg.shape, BF16) + _nbytes(wp.shape, BF16)
                                 + 6 * _nbytes((min(tm, PLE_SUB_ROWS), dm), F32)),
        name="ple",
    )(h, d, p, g, wg, wp)


def _tiles(m):
    return dict(proj=min(m, 1024), mix=min(m, 512), ffn=min(m, 1024), ffn_cols=512, ple=min(m, 512))


def _window_rows(t, batch, kw):
    return jnp.transpose(t.reshape(batch, N_HEADS, HEAD_DIM, kw), (0, 3, 1, 2))


def kernel(x_prompt, x_sample, p_prompt, p_sample, cache_k, cache_v, state_conv, rel_bias, g_mix, w_in,
           q_norm, k_norm, conv_w, g_attn_out, g_conv_out, w_out, g_ffn, w_gate, w_up, w_down, g_ple,
           w_ple_gate, w_ple_proj):
    depth = g_mix.shape[0]
    batch, seq, dm = x_prompt.shape
    bd, dec_seq, _ = x_sample.shape
    dc = conv_w.shape[2]
    assert depth == 1 and dec_seq == 1, "single layer, one new position per sample"
    assert seq % SUPER == 0 and dm == D_ATTN + dc and dc == D_ATTN
    assert w_gate.shape[2] % 512 == 0

    bias_p, bias_s, bias_s_new = _bias_tables(rel_bias, cache_k.shape[2])
    gmat = jnp.asarray(np.kron(np.eye(PROJ_COLS // HEAD_DIM, dtype=np.float32),
                               np.full((HEAD_DIM, HEAD_DIM), 1.0 / HEAD_DIM, np.float32)), BF16)

    i = 0
    row2 = lambda a: a.reshape(1, -1).astype(F32)
    g_mix_i, g_ffn_i, g_ple_i = row2(g_mix[i]), row2(g_ffn[i]), row2(g_ple[i])
    qk_gains = jnp.stack([row2(jnp.tile(q_norm[i], N_HEADS) * SCALE), row2(jnp.tile(k_norm[i], N_HEADS))])
    ga, gc = row2(g_attn_out[i]), row2(g_conv_out[i])
    cw = conv_w[i].astype(F32)
    w_out_i = w_out[i].astype(BF16)
    wpg_i, wpp_i = w_ple_gate[i].astype(BF16), w_ple_proj[i].astype(BF16)

    mp = batch * seq
    tp, ts = _tiles(mp), _tiles(bd)
    kw = min(SUPER, seq)
    xp = x_prompt.reshape(mp, dm)
    xs = x_sample.reshape(bd, dm)
    qks, kts, w_qk = _proj(xs, g_mix_i, w_in[i], 0, 2, 1, bd, bd, ts["proj"], qk_gains, gmat,
                           emit_bf16_weights=True)
    vhbcs, vts, w_vhbc = _proj(xs, g_mix_i, w_in[i], 2, 4, 0, bd, bd, ts["proj"], emit_bf16_weights=True)
    qk, kt = _proj(xp, g_mix_i, w_qk, 0, 2, 1, seq, kw, tp["proj"], qk_gains, gmat)
    vhbc, vt = _proj(xp, g_mix_i, w_vhbc, 0, 4, 0, seq, kw, tp["proj"])
    attn, attn_s = _attn(qk, vhbc, bias_p, batch, seq, qks, vhbcs, cache_k[i], cache_v[i], bias_s, bias_s_new)

    buf = state_conv[i].astype(F32)
    hs, ns, us = _mix_out_sample(xs, attn_s, vhbcs, buf.reshape(bd, (CONV_W - 1) * dc), cw, ga, gc, g_ffn_i,
                                 w_out_i)
    ds, wg_i, wu_i, wd_i = _ffn(ns, w_gate[i], w_up[i], w_down[i], ts["ffn"], ts["ffn_cols"],
                                emit_bf16_weights=True)
    hs = _ple(hs, ds, p_sample[i].reshape(bd, -1), g_ple_i, wpg_i, wpp_i, ts["ple"])

    h, n, u_tail = _mix_out_prompt(xp, attn, vhbc, cw, ga, gc, g_ffn_i, w_out_i, seq, tp["mix"])
    d = _ffn(n, wg_i, wu_i, wd_i, tp["ffn"], tp["ffn_cols"])
    h = _ple(h, d, p_prompt[i].reshape(mp, -1), g_ple_i, wpg_i, wpp_i, tp["ple"])
    y_prompt = h.reshape(batch, seq, dm)
    k_prompt = _window_rows(kt, batch, kw)[None]
    v_prompt = _window_rows(vt, batch, kw)[None]
    tiles_per_seq = seq // tp["mix"]
    conv_prompt = u_tail.reshape(batch, tiles_per_seq, SUBLANES, dc)[None, :, -1, SUBLANES - (CONV_W - 1):]

    y_sample = hs.reshape(bd, dec_seq, dm)
    k_sample = _window_rows(kts, 1, bd).reshape(1, bd, dec_seq, N_HEADS, HEAD_DIM)
    v_sample = _window_rows(vts, 1, bd).reshape(1, bd, dec_seq, N_HEADS, HEAD_DIM)
    conv_sample = jnp.concatenate([buf[:, 1:], us[:, None, :]], axis=1)[None]

    return (y_prompt, y_sample, k_prompt, v_prompt, conv_prompt, k_sample, v_sample, conv_sample)
```

```python
import functools
import math

import numpy as np
import jax
import jax.numpy as jnp
from jax import lax
from jax.experimental import pallas as pl
from jax.experimental.pallas import tpu as pltpu

HEAD_DIM = 64
N_HEADS = 16
D_ATTN = N_HEADS * HEAD_DIM
CONV_W = 3
DIL_STEPS = 128
DILATIONS = (1, 4, 16)
N_BUCKETS = 32
MAX_EXACT = N_BUCKETS // 2
MAX_DIST = DIL_STEPS * max(DILATIONS)
EPS = 1e-6
SCALE = HEAD_DIM ** -0.5

LANES = 128
SUBLANES = 8
BF16_ROWS = 16
HEADS_PER_GROUP = LANES // HEAD_DIM
N_GROUPS = N_HEADS // HEADS_PER_GROUP
SUPER = DIL_STEPS * max(DILATIONS)
N_RES = max(DILATIONS)
MASK_VALUE = -1e30
MASK_BUCKET = N_BUCKETS
TABLE_ROWS = -(-(N_BUCKETS + 1) // BF16_ROWS) * BF16_ROWS
TABLE_CHUNK = 2048
SAMPLE_POS_CHUNK = 512
ATTN_UNROLL = 16
PROJ_COLS = 512
PROJ_SPLIT = D_ATTN // PROJ_COLS
MIX_SUB_ROWS = 256
PLE_SUB_ROWS = 512
FFN_SUB_COLS = 256
VMEM_SLACK_BYTES = 8 * 1024 * 1024

F32 = jnp.float32
BF16 = jnp.bfloat16


def _cparams(sem, vmem_bytes):
    return pltpu.CompilerParams(dimension_semantics=sem,
                                vmem_limit_bytes=int(vmem_bytes + VMEM_SLACK_BYTES))


def _nbytes(shape, dtype):
    return int(np.prod(shape)) * jnp.dtype(dtype).itemsize


def _resident(shape, index_map):
    return pl.BlockSpec(shape, index_map, pipeline_mode=pl.Buffered(1))


def _t5_bucket(dist):
    n = np.asarray(dist, np.int32)
    nf = np.maximum(n, 1).astype(np.float32)
    large = MAX_EXACT + (np.log(nf / MAX_EXACT) / np.float32(math.log(MAX_DIST / MAX_EXACT))
                         * (N_BUCKETS - MAX_EXACT)).astype(np.int32)
    large = np.minimum(large, N_BUCKETS - 1)
    return np.where(n < MAX_EXACT, n, large).astype(np.int32)


def _stored_to_natural(d):
    runs = N_RES // d
    run = DIL_STEPS // runs
    j = np.arange(DIL_STEPS)
    return (j % run) * runs + j // run


def _prompt_bias_index():
    out = np.empty((len(DILATIONS), 2, DIL_STEPS, 2 * DIL_STEPS), np.int32)
    for di, d in enumerate(DILATIONS):
        nat = _stored_to_natural(d)
        qi = nat[:, None]
        kj = np.concatenate([nat, nat + DIL_STEPS])[None, :]
        steps = DIL_STEPS + qi - kj
        band = (steps >= 0) & (steps <= DIL_STEPS)
        bucket = _t5_bucket(d * np.clip(steps, 0, DIL_STEPS))
        out[di, 0] = np.where(band, bucket, MASK_BUCKET)
        out[di, 1] = np.where(band & (kj >= DIL_STEPS), bucket, MASK_BUCKET)
    return out


def _sample_bias_index(lb):
    back = lb - np.arange(lb)
    cached = np.stack([np.where((back % d == 0) & (back // d <= DIL_STEPS), _t5_bucket(back), MASK_BUCKET)
                       for d in DILATIONS])
    new = np.stack([_t5_bucket(d * np.zeros(1, np.int32)) for d in DILATIONS])
    return cached, new


def _bias_table_kernel(tab_ref, idx_ref, o_ref):
    t = tab_ref[...]
    hi = t.astype(BF16)
    r1 = t - hi.astype(F32)
    mid = r1.astype(BF16)
    lo = (r1 - mid.astype(F32)).astype(BF16)
    rows = lax.broadcasted_iota(jnp.int32, (TABLE_ROWS, TABLE_CHUNK), 0)

    def chunk(c, carry):
        sl = pl.ds(pl.multiple_of(c * TABLE_CHUNK, TABLE_CHUNK), TABLE_CHUNK)
        onehot = jnp.where(rows == idx_ref[:, sl], 1.0, 0.0).astype(BF16)
        acc = jnp.dot(hi, onehot, preferred_element_type=F32)
        acc = acc + jnp.dot(mid, onehot, preferred_element_type=F32)
        acc = acc + jnp.dot(lo, onehot, preferred_element_type=F32)
        o_ref[:, sl] = acc
        return carry

    lax.fori_loop(0, o_ref.shape[1] // TABLE_CHUNK, chunk, 0)


def _bias_tables(rel_bias, lb):
    pidx = _prompt_bias_index()
    cached, new = _sample_bias_index(lb)
    flat = np.concatenate([pidx.reshape(-1), cached.reshape(-1), new.reshape(-1)])
    n_pad = -(-(pidx.size + cached.size + LANES) // TABLE_CHUNK) * TABLE_CHUNK
    idx = np.full((1, n_pad), MASK_BUCKET, np.int32)
    idx[0, :flat.size] = flat
    tab = jnp.concatenate(
        [rel_bias.astype(F32).T,
         jnp.full((N_HEADS, 1), MASK_VALUE, F32),
         jnp.zeros((N_HEADS, TABLE_ROWS - N_BUCKETS - 1), F32)], axis=1)
    out = pl.pallas_call(
        _bias_table_kernel,
        out_shape=jax.ShapeDtypeStruct((N_HEADS, n_pad), F32),
        grid=(1,),
        in_specs=[pl.BlockSpec((N_HEADS, TABLE_ROWS), lambda i: (0, 0)),
                  pl.BlockSpec((1, n_pad), lambda i: (0, 0))],
        out_specs=pl.BlockSpec((N_HEADS, n_pad), lambda i: (0, 0)),
        compiler_params=_cparams(("arbitrary",), 2 * (_nbytes((N_HEADS, n_pad), F32)
                                                      + _nbytes((SUBLANES, n_pad), jnp.int32))),
        name="bias_table",
    )(tab, jnp.asarray(idx))
    n_p = pidx.size
    prompt = out[:, :n_p].reshape(N_HEADS, len(DILATIONS), 2, DIL_STEPS, 2 * DIL_STEPS)
    samp = out[:, n_p:n_p + cached.size]
    samp_new = out[:, n_p + cached.size:n_p + cached.size + LANES]
    return prompt, samp, samp_new


def _rms(x, g):
    return x * lax.rsqrt(jnp.mean(x * x, axis=-1, keepdims=True) + EPS) * g


def _head_rms(p, g, gmat):
    ms = jnp.dot((p * p).astype(BF16), gmat, preferred_element_type=F32)
    return p * lax.rsqrt(ms + EPS) * g


def _cat_groups(ref, rows):
    return jnp.concatenate([ref[g, rows, :] for g in range(N_GROUPS)], axis=1)


def _sub_tiles(tm, sub):
    sub = min(tm, sub)
    return [pl.ds(r, sub) for r in range(0, tm, sub)]


def _proj_kernel(*refs, normed, emit_w, t_plane, tiles_per_seq, first_kept):
    refs = list(refs)
    n_sc = refs.pop()
    wcopy_ref = refs.pop() if emit_w else None
    if normed:
        x_ref, g_ref, w_ref, gain_ref, gmat_ref, o_ref, t_ref = refs
    else:
        x_ref, g_ref, w_ref, o_ref, t_ref = refs
    i, j = pl.program_id(0), pl.program_id(1)

    @pl.when(j == 0)
    def _():
        n_sc[...] = _rms(x_ref[...], g_ref[...]).astype(BF16)

    w = w_ref[...].astype(BF16)
    if emit_w:
        wcopy_ref[...] = w
    y = jnp.dot(n_sc[...], w, preferred_element_type=F32)
    if normed:
        y = _head_rms(y, gain_ref[...], gmat_ref[...])
    for g in range(PROJ_COLS // LANES):
        o_ref[g] = y[:, g * LANES:(g + 1) * LANES]

    @pl.when(jnp.logical_and(j // PROJ_SPLIT == t_plane, i % tiles_per_seq >= first_kept))
    def _():
        t_ref[...] = y.T


def _proj(x, g_mix, w_in, plane0, n_planes, t_plane, seq, kw, tm, gains=None, gmat=None, emit_bf16_weights=False):
    m, dm = x.shape
    da = D_ATTN
    assert seq % tm == 0 and kw % tm == 0 and m % seq == 0
    assert not emit_bf16_weights or m == tm, "weight copies are written once: one row tile"
    tps, first_kept, batch = seq // tm, (seq - kw) // tm, m // seq
    normed = gains is not None
    grp = PROJ_COLS // LANES
    in_specs = [pl.BlockSpec((tm, dm), lambda i, j: (i, 0)), pl.BlockSpec((1, dm), lambda i, j: (0, 0)),
                pl.BlockSpec((dm, PROJ_COLS), lambda i, j: (0, plane0 * PROJ_SPLIT + j))]
    args = [x, g_mix, w_in]
    if normed:
        in_specs += [pl.BlockSpec((None, 1, PROJ_COLS), lambda i, j: (j, 0, 0)),
                     _resident((PROJ_COLS, PROJ_COLS), lambda i, j: (0, 0))]
        args += [gains.reshape(n_planes * PROJ_SPLIT, 1, PROJ_COLS), gmat]

    def t_index(i, j):
        kept = i % tps >= first_kept
        half = jnp.where(kept, jnp.clip(j - t_plane * PROJ_SPLIT, 0, PROJ_SPLIT - 1), 0)
        return (i // tps, half, jnp.maximum(i % tps - first_kept, 0))

    out_specs = [pl.BlockSpec((None, grp, tm, LANES), lambda i, j: (j // PROJ_SPLIT, j % PROJ_SPLIT, i, 0)),
                 pl.BlockSpec((None, PROJ_COLS, tm), t_index)]
    out_shape = [jax.ShapeDtypeStruct((n_planes, N_GROUPS, m, LANES), F32),
                 jax.ShapeDtypeStruct((batch, da, kw), F32)]
    if emit_bf16_weights:
        out_specs.append(pl.BlockSpec((dm, PROJ_COLS), lambda i, j: (0, j)))
        out_shape.append(jax.ShapeDtypeStruct((dm, n_planes * da), BF16))
    vmem = (2 * (_nbytes((tm, dm), F32) + _nbytes((dm, PROJ_COLS), w_in.dtype) + _nbytes((dm, PROJ_COLS), BF16)
                 + 2 * _nbytes((tm, PROJ_COLS), F32))
            + _nbytes((tm, dm), BF16) + 6 * _nbytes((tm, PROJ_COLS), F32))
    return pl.pallas_call(
        functools.partial(_proj_kernel, normed=normed, emit_w=emit_bf16_weights, t_plane=t_plane,
                          tiles_per_seq=tps, first_kept=first_kept),
        out_shape=tuple(out_shape),
        grid=(m // tm, n_planes * PROJ_SPLIT),
        in_specs=in_specs,
        out_specs=tuple(out_specs),
        scratch_shapes=[pltpu.VMEM((tm, dm), BF16)],
        compiler_params=_cparams(("arbitrary", "arbitrary"), vmem),
        name="in_proj_qk" if normed else "in_proj_vhbc",
    )(*args)


def _gather_rows(ref, starts, run):
    parts = [ref[pl.ds(s, run), :] for s in starts]
    return parts[0] if len(parts) == 1 else jnp.concatenate(parts, axis=0)


def _scatter_rows(ref, starts, run, val):
    for i, s in enumerate(starts):
        ref[pl.ds(s, run), :] = val[i * run:(i + 1) * run]


def _sample_scores(row, q_ref, kn_ref, kt_ref, bias_ref, bnew_ref):
    lb = kt_ref.shape[1]
    nd = len(DILATIONS)
    seg = (lax.broadcasted_iota(jnp.int32, (N_HEADS, D_ATTN), 1) // HEAD_DIM
           == lax.broadcasted_iota(jnp.int32, (N_HEADS, D_ATTN), 0))
    qe = jnp.where(seg, _cat_groups(q_ref, row), 0.0).astype(BF16)
    kn = _cat_groups(kn_ref, row).astype(BF16).astype(F32)
    s_new = jnp.sum(qe.astype(F32) * kn, axis=1, keepdims=True)
    sn = [s_new + bnew_ref[:, di:di + 1] for di in range(nd)]
    starts = range(0, lb, SAMPLE_POS_CHUNK)
    chunks = [pl.ds(c, SAMPLE_POS_CHUNK) for c in starts]
    sc = []
    for c, ch in zip(starts, chunks):
        s = jnp.dot(qe, kt_ref[:, ch].astype(BF16), preferred_element_type=F32)
        sc.append([s + bias_ref[:, pl.ds(di * lb + c, SAMPLE_POS_CHUNK)] for di in range(nd)])
    m = functools.reduce(jnp.maximum, [jnp.max(x, axis=1, keepdims=True) for row_ in sc for x in row_] + sn)
    pn = functools.reduce(jnp.add, [jnp.exp(x - m) for x in sn])
    ps = [functools.reduce(jnp.add, [jnp.exp(x - m) for x in sd]) for sd in sc]
    return seg, chunks, ps, pn


def _sample_output(row, state, vn_ref, vt_ref, o_ref):
    seg, chunks, ps, pn = state
    den = pn
    o = pn * _cat_groups(vn_ref, row)
    for ch, p in zip(chunks, ps):
        den = den + jnp.sum(p, axis=1, keepdims=True)
        o = o + lax.dot_general(p.astype(BF16), vt_ref[:, ch].astype(BF16),
                                (((1,), (1,)), ((), ())), preferred_element_type=F32)
    out = jnp.sum(jnp.where(seg, o / den, 0.0), axis=0, keepdims=True)
    for g in range(N_GROUPS):
        o_ref[g, row, :] = out[:, g * LANES:(g + 1) * LANES]


def _attn_kernel(q_ref, k_ref, v_ref, bias_ref, sq_ref, skn_ref, svn_ref, ck_hbm, cv_hbm, sbias_ref,
                 sbnew_ref, o_ref, so_ref, qp, kp, vp, acc, m_sc, l_sc, tmp, kbuf, vbuf, sem, *, samples_per_step):
    sb = pl.program_id(2)
    step = (pl.program_id(0) * pl.num_programs(1) + pl.program_id(1)) * pl.num_programs(2) + sb
    n_samples = ck_hbm.shape[0]
    slot = sb % 2
    pslot = 1 - slot
    kcur, vcur = kp.at[slot], vp.at[slot]
    kprev, vprev = kp.at[pslot], vp.at[pslot]

    def copies(g, slot):
        return (pltpu.make_async_copy(ck_hbm.at[g], kbuf.at[slot], sem.at[0, slot]),
                pltpu.make_async_copy(cv_hbm.at[g], vbuf.at[slot], sem.at[1, slot]))

    @pl.when(step == 0)
    def _():
        for c in copies(0, 0):
            c.start()

    def sample(t):
        g = step * samples_per_step + t
        slot = t % 2
        row = pl.ds(g, 1)

        @pl.when(g + 1 < n_samples)
        def _():
            for c in copies(g + 1, 1 - slot):
                c.start()

        for c in copies(g, slot):
            c.wait()
        state = _sample_scores(row, sq_ref, skn_ref, kbuf.at[slot], sbias_ref, sbnew_ref)
        _sample_output(row, state, svn_ref, vbuf.at[slot], so_ref)

    quarter = SUPER // 4
    for src, dst in ((q_ref, qp), (k_ref, kcur), (v_ref, vcur)):
        for c in range(4):
            tmp[pl.ds(c * quarter, quarter), :] = src[pl.ds(c, quarter, stride=4), :]
        for c in range(4):
            for b in range(4):
                dst[pl.ds((4 * b + c) * DIL_STEPS, DIL_STEPS), :] = tmp[pl.ds(c * quarter + b, DIL_STEPS, stride=4), :]

    @pl.when(sb == 0)
    def _():
        kprev[...] = jnp.zeros(kprev.shape, F32)
        vprev[...] = jnp.zeros(vprev.shape, F32)

    head0 = lax.broadcasted_iota(jnp.int32, (DIL_STEPS, LANES), 1) < HEAD_DIM

    def block(di, starts, run, prev_ref_k, prev_ref_v, prev_starts, first, mode, out_rows=None):
        qb = _gather_rows(qp, starts, run)
        kb = jnp.concatenate([_gather_rows(prev_ref_k, prev_starts, run),
                              _gather_rows(kcur, starts, run)], axis=0).astype(BF16)
        vb = jnp.concatenate([_gather_rows(prev_ref_v, prev_starts, run),
                              _gather_rows(vcur, starts, run)], axis=0).astype(BF16)
        ms, ls, os_ = [], [], []
        for h in range(HEADS_PER_GROUP):
            keep = head0 if h == 0 else jnp.logical_not(head0)
            qh = jnp.where(keep, qb, 0.0).astype(BF16)
            s = lax.dot_general(qh, kb, (((1,), (1,)), ((), ())), preferred_element_type=F32)
            s = s + bias_ref[h, di, first]
            mh = jnp.max(s, axis=1, keepdims=True)
            p = jnp.exp(s - mh)
            ls.append(jnp.sum(p, axis=1, keepdims=True))
            os_.append(jnp.dot(p.astype(BF16), vb, preferred_element_type=F32))
            ms.append(mh)
        m_c = jnp.where(head0, ms[0], ms[1])
        l_c = jnp.where(head0, ls[0], ls[1])
        o_c = jnp.where(head0, os_[0], os_[1])
        if mode == "init":
            _scatter_rows(m_sc, starts, run, m_c)
            _scatter_rows(l_sc, starts, run, l_c)
            _scatter_rows(acc, starts, run, o_c)
            return
        m_o = _gather_rows(m_sc, starts, run)
        m_n = jnp.maximum(m_o, m_c)
        a_o = jnp.exp(m_o - m_n)
        a_c = jnp.exp(m_c - m_n)
        l_n = _gather_rows(l_sc, starts, run) * a_o + l_c * a_c
        o_n = _gather_rows(acc, starts, run) * a_o + o_c * a_c
        if mode == "merge":
            _scatter_rows(m_sc, starts, run, m_n)
            _scatter_rows(l_sc, starts, run, l_n)
            _scatter_rows(acc, starts, run, o_n)
        else:
            o_ref[out_rows, :] = o_n / l_n

    def run_dilation(di, mode):
        d = DILATIONS[di]
        runs = N_RES // d
        run = DIL_STEPS // runs
        nblk = SUPER // d // DIL_STEPS

        def body(it, carry):
            c = it // nblk
            n = it % nblk
            starts = [pl.multiple_of((d * b + c) * DIL_STEPS + run * n, SUBLANES) for b in range(runs)]
            pn = jnp.where(n > 0, n - 1, nblk - 1)
            prev_starts = [pl.multiple_of((d * b + c) * DIL_STEPS + run * pn, SUBLANES)
                           for b in range(runs)]
            pslot_n = jnp.where(n > 0, slot, pslot)
            first = jnp.logical_and(sb == 0, n == 0).astype(jnp.int32)
            out_rows = pl.ds(it, DIL_STEPS, stride=N_RES) if mode == "final" else None
            block(di, starts, run, kp.at[pslot_n], vp.at[pslot_n], prev_starts, first, mode, out_rows)
            return carry

        lax.fori_loop(0, d * nblk, body, 0, unroll=ATTN_UNROLL)

    phases = ("init", "merge", "final")
    for t in range(max(samples_per_step, len(phases))):
        if t < samples_per_step:
            sample(t)
        if t < len(phases):
            run_dilation(t, phases[t])


def _attn(qk, vhbc, bias, batch, seq, sqk, svhbc, cache_k, cache_v, sbias, sbias_new):
    nsb = seq // SUPER
    m = batch * seq
    bd, lb = cache_k.shape[0], cache_k.shape[1]
    da = D_ATTN
    n_steps = N_GROUPS * batch * nsb
    assert bd % (2 * n_steps) == 0 and lb % SAMPLE_POS_CHUNK == 0
    assert N_RES == 16, "the residue-major copy is written as two stride-4 passes"

    def feature_major(c):
        return jnp.transpose(c, (0, 2, 3, 1)).reshape(bd, da, lb)

    def plane(p):
        return pl.BlockSpec((None, None, SUPER, LANES), lambda g, b, s: (p, g, b * nsb + s, 0))

    def splane(p):
        return pl.BlockSpec((None, N_GROUPS, bd, LANES), lambda g, b, s: (p, 0, 0, 0))

    fixed2 = lambda g, b, s: (0, 0)
    bias_spec = pl.BlockSpec((HEADS_PER_GROUP,) + bias.shape[1:], lambda g, b, s: (g, 0, 0, 0, 0),
                             pipeline_mode=pl.Buffered(1))
    hbm = pl.BlockSpec(memory_space=pl.ANY)
    blk_bytes = _nbytes((SUPER, LANES), F32)
    return pl.pallas_call(
        functools.partial(_attn_kernel, samples_per_step=bd // n_steps),
        out_shape=(jax.ShapeDtypeStruct((N_GROUPS, m, LANES), F32),
                   jax.ShapeDtypeStruct((N_GROUPS, bd, LANES), F32)),
        grid=(N_GROUPS, batch, nsb),
        in_specs=[plane(0), plane(1), plane(0), bias_spec,
                  splane(0), splane(1), splane(0), hbm, hbm,
                  pl.BlockSpec(sbias.shape, fixed2), pl.BlockSpec(sbias_new.shape, fixed2)],
        out_specs=(pl.BlockSpec((None, SUPER, LANES), lambda g, b, s: (g, b * nsb + s, 0)),
                   pl.BlockSpec((N_GROUPS, bd, LANES), lambda g, b, s: (0, 0, 0))),
        scratch_shapes=[pltpu.VMEM((SUPER, LANES), F32),
                        pltpu.VMEM((2, SUPER, LANES), F32), pltpu.VMEM((2, SUPER, LANES), F32),
                        pltpu.VMEM((SUPER, LANES), F32), pltpu.VMEM((SUPER, LANES), F32),
                        pltpu.VMEM((SUPER, LANES), F32), pltpu.VMEM((SUPER, LANES), F32),
                        pltpu.VMEM((2, da, lb), F32), pltpu.VMEM((2, da, lb), F32),
                        pltpu.SemaphoreType.DMA((2, 2))],
        compiler_params=_cparams(("arbitrary", "arbitrary", "arbitrary"),
                                 (2 * 4 + 9) * blk_bytes
                                 + _nbytes((HEADS_PER_GROUP,) + bias.shape[1:], F32)
                                 + 4 * _nbytes((da, lb), F32) + 2 * 4 * _nbytes((bd, da), F32)
                                 + 2 * _nbytes(sbias.shape, F32)),
        name="attn",
    )(qk, qk, vhbc, bias, sqk, sqk, svhbc, feature_major(cache_k), feature_major(cache_v), sbias, sbias_new)


def _mix_tail(x, a, conv, ga_ref, gc_ref, w_ref):
    cat = jnp.concatenate([_rms(a, ga_ref[...]), _rms(conv, gc_ref[...])], axis=1).astype(BF16)
    return x + jnp.dot(cat, w_ref[...], preferred_element_type=F32)


def _mix_prompt_kernel(x_ref, a_ref, h_ref, b_ref, c_ref, hh_ref, ch_ref, cw_ref, ga_ref, gc_ref, gf_ref, w_ref,
                       o_ref, n_ref, ut_ref, *, tiles_per_seq):
    tm = x_ref.shape[0]
    seq_start = pl.program_id(0) % tiles_per_seq == 0
    all8 = pl.ds(0, SUBLANES)
    halo = jnp.where(seq_start, 0.0, _cat_groups(hh_ref, all8) * _cat_groups(ch_ref, all8))
    for rows in _sub_tiles(tm, MIX_SUB_ROWS):
        u = _cat_groups(h_ref, rows) * _cat_groups(c_ref, rows)
        rid = lax.broadcasted_iota(jnp.int32, u.shape, 0)
        u1 = jnp.where(rid == 0, halo[7:8], pltpu.roll(u, 1, axis=0))
        u2 = jnp.where(rid == 0, halo[6:7], jnp.where(rid == 1, halo[7:8], pltpu.roll(u, 2, axis=0)))
        cy = cw_ref[0:1] * u2 + cw_ref[1:2] * u1 + cw_ref[2:3] * u
        conv = _cat_groups(b_ref, rows) * cy
        h1 = _mix_tail(x_ref[rows, :], _cat_groups(a_ref, rows), conv, ga_ref, gc_ref, w_ref)
        o_ref[rows, :] = h1
        n_ref[rows, :] = _rms(h1, gf_ref[...]).astype(BF16)
        halo = u[u.shape[0] - SUBLANES:]
    ut_ref[...] = halo


def _mix_sample_kernel(x_ref, a_ref, h_ref, b_ref, c_ref, buf_ref, cw_ref, ga_ref, gc_ref, gf_ref, w_ref,
                       o_ref, n_ref, u_ref):
    every = pl.ds(0, x_ref.shape[0])
    u = _cat_groups(h_ref, every) * _cat_groups(c_ref, every)
    dc = u.shape[1]
    cy = cw_ref[0:1] * buf_ref[:, :dc] + cw_ref[1:2] * buf_ref[:, dc:] + cw_ref[2:3] * u
    conv = _cat_groups(b_ref, every) * cy
    h1 = _mix_tail(x_ref[...], _cat_groups(a_ref, every), conv, ga_ref, gc_ref, w_ref)
    o_ref[...] = h1
    n_ref[...] = _rms(h1, gf_ref[...]).astype(BF16)
    u_ref[...] = u


def _mix_out_prompt(x, attn, vhbc, conv_w, ga, gc, gf, w_out, seq, tm):
    m, dm = x.shape
    dc = conv_w.shape[1]
    fixed = lambda i: (0, 0)
    halo_blocks = tm // SUBLANES

    def plane(p):
        return pl.BlockSpec((None, N_GROUPS, tm, LANES), lambda i: (p, 0, i, 0))

    def halo(p):
        return pl.BlockSpec((None, N_GROUPS, SUBLANES, LANES),
                            lambda i: (p, 0, jnp.maximum(i * halo_blocks - 1, 0), 0))

    tile = _nbytes((tm, dc), F32)
    return pl.pallas_call(
        functools.partial(_mix_prompt_kernel, tiles_per_seq=seq // tm),
        out_shape=(jax.ShapeDtypeStruct((m, dm), F32), jax.ShapeDtypeStruct((m, dm), BF16),
                   jax.ShapeDtypeStruct((m // tm, SUBLANES, dc), F32)),
        grid=(m // tm,),
        in_specs=[pl.BlockSpec((tm, dm), lambda i: (i, 0)),
                  pl.BlockSpec((N_GROUPS, tm, LANES), lambda i: (0, i, 0)),
                  plane(1), plane(2), plane(3), halo(1), halo(3),
                  pl.BlockSpec((CONV_W, dc), fixed), pl.BlockSpec((1, D_ATTN), fixed),
                  pl.BlockSpec((1, dc), fixed), pl.BlockSpec((1, dm), fixed), _resident(w_out.shape, fixed)],
        out_specs=(pl.BlockSpec((tm, dm), lambda i: (i, 0)), pl.BlockSpec((tm, dm), lambda i: (i, 0)),
                   pl.BlockSpec((None, SUBLANES, dc), lambda i: (i, 0, 0))),
        compiler_params=_cparams(("arbitrary",), 2 * 9 * tile + _nbytes(w_out.shape, BF16)
                                 + 4 * _nbytes((min(tm, MIX_SUB_ROWS), dm), F32)),
        name="mix_out_prompt",
    )(x, attn, vhbc, vhbc, vhbc, vhbc, vhbc, conv_w, ga, gc, gf, w_out)


def _mix_out_sample(x, attn, vhbc, buf, conv_w, ga, gc, gf, w_out):
    m, dm = x.shape
    dc = conv_w.shape[1]
    full = lambda a: pl.BlockSpec(a.shape, lambda i: (0,) * a.ndim)

    def plane(p):
        return pl.BlockSpec((None, N_GROUPS, m, LANES), lambda i: (p, 0, 0, 0))

    return pl.pallas_call(
        _mix_sample_kernel,
        out_shape=(jax.ShapeDtypeStruct((m, dm), F32), jax.ShapeDtypeStruct((m, dm), BF16),
                   jax.ShapeDtypeStruct((m, dc), F32)),
        grid=(1,),
        in_specs=[full(x), full(attn), plane(1), plane(2), plane(3), full(buf), full(conv_w), full(ga),
                  full(gc), full(gf), full(w_out)],
        out_specs=(pl.BlockSpec((m, dm), lambda i: (0, 0)), pl.BlockSpec((m, dm), lambda i: (0, 0)),
                   pl.BlockSpec((m, dc), lambda i: (0, 0))),
        compiler_params=_cparams(("arbitrary",), 2 * (11 * _nbytes((m, dc), F32) + _nbytes(w_out.shape, BF16))),
        name="mix_out_sample",
    )(x, attn, vhbc, vhbc, vhbc, buf, conv_w, ga, gc, gf, w_out)


def _ffn_kernel(n_ref, wg_ref, wu_ref, wd_ref, o_ref, *wcopy_refs):
    @pl.when(pl.program_id(1) == 0)
    def _():
        o_ref[...] = jnp.zeros(o_ref.shape, F32)

    n = n_ref[...]
    acc = None
    for c in range(0, wg_ref.shape[1], FFN_SUB_COLS):
        cols = pl.ds(c, FFN_SUB_COLS)
        wg, wu, wd = wg_ref[:, cols].astype(BF16), wu_ref[:, cols].astype(BF16), wd_ref[cols, :].astype(BF16)
        if wcopy_refs:
            wcopy_refs[0][:, cols], wcopy_refs[1][:, cols], wcopy_refs[2][cols, :] = wg, wu, wd
        gate = jnp.dot(n, wg, preferred_element_type=F32)
        up = jnp.dot(n, wu, preferred_element_type=F32)
        act = (gate / (1.0 + jnp.exp(-gate)) * up).astype(BF16)
        part = jnp.dot(act, wd, preferred_element_type=F32)
        acc = part if acc is None else acc + part
    o_ref[...] += acc


def _ffn(n, wg, wu, wd, tm, tf, emit_bf16_weights=False):
    m, dm = n.shape
    dff = wg.shape[1]
    row = lambda i, f: (i, 0)
    w_specs = [pl.BlockSpec((dm, tf), lambda i, f: (0, f)), pl.BlockSpec((dm, tf), lambda i, f: (0, f)),
               pl.BlockSpec((tf, dm), lambda i, f: (f, 0))]
    out_shape = [jax.ShapeDtypeStruct((m, dm), F32)]
    out_specs = [pl.BlockSpec((tm, dm), row)]
    if emit_bf16_weights:
        assert m == tm, "weight copies are written once: one row tile"
        out_shape += [jax.ShapeDtypeStruct(w.shape, BF16) for w in (wg, wu, wd)]
        out_specs += w_specs
    out = pl.pallas_call(
        _ffn_kernel,
        out_shape=tuple(out_shape),
        grid=(m // tm, dff // tf),
        in_specs=[pl.BlockSpec((tm, dm), row)] + w_specs,
        out_specs=tuple(out_specs),
        compiler_params=_cparams(("arbitrary", "arbitrary"),
                                 2 * (_nbytes((tm, dm), F32) + _nbytes((tm, dm), BF16))
                                 + 2 * 3 * (_nbytes((dm, tf), wg.dtype)
                                            + (_nbytes((dm, tf), BF16) if emit_bf16_weights else 0))
                                 + 4 * _nbytes((tm, tf), F32)),
        name="ffn",
    )(n, wg, wu, wd)
    return out if emit_bf16_weights else out[0]


def _ple_kernel(h_ref, d_ref, p_ref, g_ref, wg_ref, wp_ref, o_ref):
    for rows in _sub_tiles(h_ref.shape[0], PLE_SUB_ROWS):
        x = h_ref[rows, :] + d_ref[rows, :]
        z = jnp.dot(_rms(x, g_ref[...]).astype(BF16), wg_ref[...], preferred_element_type=F32)
        e = jnp.dot(p_ref[rows, :].astype(BF16), wp_ref[...], preferred_element_type=F32)
        o_ref[rows, :] = x + e / (1.0 + jnp.exp(-z))


def _ple(h, d, p, g, wg, wp, tm):
    m, dm = h.shape
    dp = p.shape[1]
    row = lambda i: (i, 0)
    fixed = lambda i: (0, 0)
    return pl.pallas_call(
        _ple_kernel,
        out_shape=jax.ShapeDtypeStruct((m, dm), F32),
        grid=(m // tm,),
        in_specs=[pl.BlockSpec((tm, dm), row), pl.BlockSpec((tm, dm), row), pl.BlockSpec((tm, dp), row),
                  pl.BlockSpec((1, dm), fixed), _resident(wg.shape, fixed), _resident(wp.shape, fixed)],
        out_specs=pl.BlockSpec((tm, dm), row),
        compiler_params=_cparams(("arbitrary",),
                                 2 * (3 * _nbytes((tm, dm), F32) + _nbytes((tm, dp), F32))
                                 + _nbytes(wg.shape, BF16) + _nbytes(wp.shape, BF16)
                                 + 6 * _nbytes((min(tm, PLE_SUB_ROWS), dm), F32)),
        name="ple",
    )(h, d, p, g, wg, wp)


def _tiles(m):
    return dict(proj=min(m, 1024), mix=min(m, 512), ffn=min(m, 1024), ffn_cols=512, ple=min(m, 512))


def _window_rows(t, batch, kw):
    return jnp.transpose(t.reshape(batch, N_HEADS, HEAD_DIM, kw), (0, 3, 1, 2))


def kernel(x_prompt, x_sample, p_prompt, p_sample, cache_k, cache_v, state_conv, rel_bias, g_mix, w_in,
           q_norm, k_norm, conv_w, g_attn_out, g_conv_out, w_out, g_ffn, w_gate, w_up, w_down, g_ple,
           w_ple_gate, w_ple_proj):
    depth = g_mix.shape[0]
    batch, seq, dm = x_prompt.shape
    bd, dec_seq, _ = x_sample.shape
    dc = conv_w.shape[2]
    assert depth == 1 and dec_seq == 1, "single layer, one new position per sample"
    assert seq % SUPER == 0 and dm == D_ATTN + dc and dc == D_ATTN
    assert w_gate.shape[2] % 512 == 0

    bias_p, bias_s, bias_s_new = _bias_tables(rel_bias, cache_k.shape[2])
    gmat = jnp.asarray(np.kron(np.eye(PROJ_COLS // HEAD_DIM, dtype=np.float32),
                               np.full((HEAD_DIM, HEAD_DIM), 1.0 / HEAD_DIM, np.float32)), BF16)

    i = 0
    row2 = lambda a: a.reshape(1, -1).astype(F32)
    g_mix_i, g_ffn_i, g_ple_i = row2(g_mix[i]), row2(g_ffn[i]), row2(g_ple[i])
    qk_gains = jnp.stack([row2(jnp.tile(q_norm[i], N_HEADS) * SCALE), row2(jnp.tile(k_norm[i], N_HEADS))])
    ga, gc = row2(g_attn_out[i]), row2(g_conv_out[i])
    cw = conv_w[i].astype(F32)
    w_out_i = w_out[i].astype(BF16)
    wpg_i, wpp_i = w_ple_gate[i].astype(BF16), w_ple_proj[i].astype(BF16)

    mp = batch * seq
    tp, ts = _tiles(mp), _tiles(bd)
    kw = min(SUPER, seq)
    xp = x_prompt.reshape(mp, dm)
    xs = x_sample.reshape(bd, dm)
    qks, kts, w_qk = _proj(xs, g_mix_i, w_in[i], 0, 2, 1, bd, bd, ts["proj"], qk_gains, gmat,
                           emit_bf16_weights=True)
    vhbcs, vts, w_vhbc = _proj(xs, g_mix_i, w_in[i], 2, 4, 0, bd, bd, ts["proj"], emit_bf16_weights=True)
    qk, kt = _proj(xp, g_mix_i, w_qk, 0, 2, 1, seq, kw, tp["proj"], qk_gains, gmat)
    vhbc, vt = _proj(xp, g_mix_i, w_vhbc, 0, 4, 0, seq, kw, tp["proj"])
    attn, attn_s = _attn(qk, vhbc, bias_p, batch, seq, qks, vhbcs, cache_k[i], cache_v[i], bias_s, bias_s_new)

    buf = state_conv[i].astype(F32)
    hs, ns, us = _mix_out_sample(xs, attn_s, vhbcs, buf.reshape(bd, (CONV_W - 1) * dc), cw, ga, gc, g_ffn_i,
                                 w_out_i)
    ds, wg_i, wu_i, wd_i = _ffn(ns, w_gate[i], w_up[i], w_down[i], ts["ffn"], ts["ffn_cols"],
                                emit_bf16_weights=True)
    hs = _ple(hs, ds, p_sample[i].reshape(bd, -1), g_ple_i, wpg_i, wpp_i, ts["ple"])

    h, n, u_tail = _mix_out_prompt(xp, attn, vhbc, cw, ga, gc, g_ffn_i, w_out_i, seq, tp["mix"])
    d = _ffn(n, wg_i, wu_i, wd_i, tp["ffn"], tp["ffn_cols"])
    h = _ple(h, d, p_prompt[i].reshape(mp, -1), g_ple_i, wpg_i, wpp_i, tp["ple"])
    y_prompt = h.reshape(batch, seq, dm)
    k_prompt = _window_rows(kt, batch, kw)[None]
    v_prompt = _window_rows(vt, batch, kw)[None]
    tiles_per_seq = seq // tp["mix"]
    conv_prompt = u_tail.reshape(batch, tiles_per_seq, SUBLANES, dc)[None, :, -1, SUBLANES - (CONV_W - 1):]

    y_sample = hs.reshape(bd, dec_seq, dm)
    k_sample = _window_rows(kts, 1, bd).reshape(1, bd, dec_seq, N_HEADS, HEAD_DIM)
    v_sample = _window_rows(vts, 1, bd).reshape(1, bd, dec_seq, N_HEADS, HEAD_DIM)
    conv_sample = jnp.concatenate([buf[:, 1:], us[:, None, :]], axis=1)[None]

    return (y_prompt, y_sample, k_prompt, v_prompt, conv_prompt, k_sample, v_sample, conv_sample)
```

```python
import functools
import math

import numpy as np
import jax
import jax.numpy as jnp
from jax import lax
from jax.experimental import pallas as pl
from jax.experimental.pallas import tpu as pltpu

HEAD_DIM = 64
N_HEADS = 16
D_ATTN = N_HEADS * HEAD_DIM
CONV_W = 3
DIL_STEPS = 128
DILATIONS = (1, 4, 16)
N_BUCKETS = 32
MAX_EXACT = N_BUCKETS // 2
MAX_DIST = DIL_STEPS * max(DILATIONS)
EPS = 1e-6
SCALE = HEAD_DIM ** -0.5

LANES = 128
SUBLANES = 8
BF16_ROWS = 16
HEADS_PER_GROUP = LANES // HEAD_DIM
N_GROUPS = N_HEADS // HEADS_PER_GROUP
SUPER = DIL_STEPS * max(DILATIONS)
N_RES = max(DILATIONS)
MASK_VALUE = -1e30
MASK_BUCKET = N_BUCKETS
TABLE_ROWS = -(-(N_BUCKETS + 1) // BF16_ROWS) * BF16_ROWS
TABLE_CHUNK = 2048
TABLE_UNROLL = 4
SAMPLE_POS_CHUNK = 512
ATTN_UNROLL = 8
PROJ_COLS = 512
PROJ_SPLIT = D_ATTN // PROJ_COLS
MIX_SUB_ROWS = 256
PLE_SUB_ROWS = 512
FFN_SUB_COLS = 256
VMEM_SLACK_BYTES = 8 * 1024 * 1024

F32 = jnp.float32
BF16 = jnp.bfloat16


def _cparams(sem, vmem_bytes):
    return pltpu.CompilerParams(dimension_semantics=sem,
                                vmem_limit_bytes=int(vmem_bytes + VMEM_SLACK_BYTES))


def _nbytes(shape, dtype):
    return int(np.prod(shape)) * jnp.dtype(dtype).itemsize


def _resident(shape, index_map):
    return pl.BlockSpec(shape, index_map, pipeline_mode=pl.Buffered(1))


def _t5_bucket(dist):
    n = np.asarray(dist, np.int32)
    nf = np.maximum(n, 1).astype(np.float32)
    large = MAX_EXACT + (np.log(nf / MAX_EXACT) / np.float32(math.log(MAX_DIST / MAX_EXACT))
                         * (N_BUCKETS - MAX_EXACT)).astype(np.int32)
    large = np.minimum(large, N_BUCKETS - 1)
    return np.where(n < MAX_EXACT, n, large).astype(np.int32)


def _stored_to_natural(d):
    runs = N_RES // d
    run = DIL_STEPS // runs
    j = np.arange(DIL_STEPS)
    return (j % run) * runs + j // run


def _prompt_bias_index():
    out = np.empty((len(DILATIONS), 2, DIL_STEPS, 2 * DIL_STEPS), np.int32)
    for di, d in enumerate(DILATIONS):
        nat = _stored_to_natural(d)
        qi = nat[:, None]
        kj = np.concatenate([nat, nat + DIL_STEPS])[None, :]
        steps = DIL_STEPS + qi - kj
        band = (steps >= 0) & (steps <= DIL_STEPS)
        bucket = _t5_bucket(d * np.clip(steps, 0, DIL_STEPS))
        out[di, 0] = np.where(band, bucket, MASK_BUCKET)
        out[di, 1] = np.where(band & (kj >= DIL_STEPS), bucket, MASK_BUCKET)
    return out


def _sample_bias_index(lb):
    back = lb - np.arange(lb)
    cached = np.stack([np.where((back % d == 0) & (back // d <= DIL_STEPS), _t5_bucket(back), MASK_BUCKET)
                       for d in DILATIONS])
    new = np.stack([_t5_bucket(d * np.zeros(1, np.int32)) for d in DILATIONS])
    return cached, new


def _bias_table_kernel(tab_ref, idx_ref, o_ref):
    t = tab_ref[...]
    hi = t.astype(BF16)
    r1 = t - hi.astype(F32)
    mid = r1.astype(BF16)
    lo = (r1 - mid.astype(F32)).astype(BF16)
    rows = lax.broadcasted_iota(jnp.int32, (TABLE_ROWS, TABLE_CHUNK), 0)

    def chunk(c, carry):
        sl = pl.ds(pl.multiple_of(c * TABLE_CHUNK, TABLE_CHUNK), TABLE_CHUNK)
        onehot = jnp.where(rows == idx_ref[:, sl], 1.0, 0.0).astype(BF16)
        acc = jnp.dot(hi, onehot, preferred_element_type=F32)
        acc = acc + jnp.dot(mid, onehot, preferred_element_type=F32)
        acc = acc + jnp.dot(lo, onehot, preferred_element_type=F32)
        o_ref[:, sl] = acc
        return carry

    lax.fori_loop(0, o_ref.shape[1] // TABLE_CHUNK, chunk, 0, unroll=TABLE_UNROLL)


def _bias_tables(rel_bias, lb):
    pidx = _prompt_bias_index()
    cached, new = _sample_bias_index(lb)
    flat = np.concatenate([pidx.reshape(-1), cached.reshape(-1), new.reshape(-1)])
    n_pad = -(-(pidx.size + cached.size + LANES) // TABLE_CHUNK) * TABLE_CHUNK
    idx = np.full((1, n_pad), MASK_BUCKET, np.int32)
    idx[0, :flat.size] = flat
    tab = jnp.concatenate(
        [rel_bias.astype(F32).T,
         jnp.full((N_HEADS, 1), MASK_VALUE, F32),
         jnp.zeros((N_HEADS, TABLE_ROWS - N_BUCKETS - 1), F32)], axis=1)
    out = pl.pallas_call(
        _bias_table_kernel,
        out_shape=jax.ShapeDtypeStruct((N_HEADS, n_pad), F32),
        grid=(1,),
        in_specs=[pl.BlockSpec((N_HEADS, TABLE_ROWS), lambda i: (0, 0)),
                  pl.BlockSpec((1, n_pad), lambda i: (0, 0))],
        out_specs=pl.BlockSpec((N_HEADS, n_pad), lambda i: (0, 0)),
        compiler_params=_cparams(("arbitrary",), 2 * (_nbytes((N_HEADS, n_pad), F32)
                                                      + _nbytes((SUBLANES, n_pad), jnp.int32))),
        name="bias_table",
    )(tab, jnp.asarray(idx))
    n_p = pidx.size
    prompt = out[:, :n_p].reshape(N_HEADS, len(DILATIONS), 2, DIL_STEPS, 2 * DIL_STEPS)
    samp = out[:, n_p:n_p + cached.size]
    samp_new = out[:, n_p + cached.size:n_p + cached.size + LANES]
    return prompt, samp, samp_new


def _rms(x, g):
    return x * lax.rsqrt(jnp.mean(x * x, axis=-1, keepdims=True) + EPS) * g


def _head_rms(p, g, gmat):
    ms = jnp.dot((p * p).astype(BF16), gmat, preferred_element_type=F32)
    return p * lax.rsqrt(ms + EPS) * g


def _cat_groups(ref, rows):
    return jnp.concatenate([ref[g, rows, :] for g in range(N_GROUPS)], axis=1)


def _sub_tiles(tm, sub):
    sub = min(tm, sub)
    return [pl.ds(r, sub) for r in range(0, tm, sub)]


def _proj_kernel(*refs, normed, emit_w, t_plane, tiles_per_seq, first_kept):
    i, j = pl.program_id(0), pl.program_id(1)
    if normed:
        x_ref, g_ref, w_ref, gain_ref, gmat_ref, o_ref, t_ref, *rest = refs
        n_ref = rest[-1]

        @pl.when(j == 0)
        def _():
            n_ref[...] = _rms(x_ref[...], g_ref[...]).astype(BF16)
    else:
        n_ref, w_ref, o_ref, t_ref, *rest = refs

    w = w_ref[...].astype(BF16)
    if emit_w:
        rest[0][...] = w
    y = jnp.dot(n_ref[...], w, preferred_element_type=F32)
    if normed:
        y = _head_rms(y, gain_ref[...], gmat_ref[...])
    for g in range(PROJ_COLS // LANES):
        o_ref[g] = y[:, g * LANES:(g + 1) * LANES]

    @pl.when(jnp.logical_and(j // PROJ_SPLIT == t_plane, i % tiles_per_seq >= first_kept))
    def _():
        t_ref[...] = y.T


def _proj(x, w_in, plane0, n_planes, t_plane, seq, kw, tm, g_mix=None, gains=None, gmat=None,
          emit_bf16_weights=False):
    m, dm = x.shape
    da = D_ATTN
    assert seq % tm == 0 and kw % tm == 0 and m % seq == 0
    assert not emit_bf16_weights or m == tm, "weight copies are written once: one row tile"
    tps, first_kept, batch = seq // tm, (seq - kw) // tm, m // seq
    normed = gains is not None
    grp = PROJ_COLS // LANES
    w_spec = pl.BlockSpec((dm, PROJ_COLS), lambda i, j: (0, plane0 * PROJ_SPLIT + j))
    if normed:
        in_specs = [pl.BlockSpec((tm, dm), lambda i, j: (i, 0)), pl.BlockSpec((1, dm), lambda i, j: (0, 0)), w_spec,
                    pl.BlockSpec((None, 1, PROJ_COLS), lambda i, j: (j, 0, 0)),
                    _resident((PROJ_COLS, PROJ_COLS), lambda i, j: (0, 0))]
        args = [x, g_mix, w_in, gains.reshape(n_planes * PROJ_SPLIT, 1, PROJ_COLS), gmat]
    else:
        in_specs = [pl.BlockSpec((tm, dm), lambda i, j: (i, 0)), w_spec]
        args = [x, w_in]

    def t_index(i, j):
        kept = i % tps >= first_kept
        half = jnp.where(kept, jnp.clip(j - t_plane * PROJ_SPLIT, 0, PROJ_SPLIT - 1), 0)
        return (i // tps, half, jnp.maximum(i % tps - first_kept, 0))

    out_specs = [pl.BlockSpec((None, grp, tm, LANES), lambda i, j: (j // PROJ_SPLIT, j % PROJ_SPLIT, i, 0)),
                 pl.BlockSpec((None, PROJ_COLS, tm), t_index)]
    out_shape = [jax.ShapeDtypeStruct((n_planes, N_GROUPS, m, LANES), F32),
                 jax.ShapeDtypeStruct((batch, da, kw), F32)]
    if emit_bf16_weights:
        out_specs.append(pl.BlockSpec((dm, PROJ_COLS), lambda i, j: (0, j)))
        out_shape.append(jax.ShapeDtypeStruct((dm, n_planes * da), BF16))
    if normed:
        out_specs.append(pl.BlockSpec((tm, dm), lambda i, j: (i, 0)))
        out_shape.append(jax.ShapeDtypeStruct((m, dm), BF16))
    vmem = (2 * (_nbytes((tm, dm), x.dtype) + _nbytes((dm, PROJ_COLS), w_in.dtype) + _nbytes((dm, PROJ_COLS), BF16)
                 + 2 * _nbytes((tm, PROJ_COLS), F32) + _nbytes((tm, dm), BF16))
            + 6 * _nbytes((tm, PROJ_COLS), F32))
    return pl.pallas_call(
        functools.partial(_proj_kernel, normed=normed, emit_w=emit_bf16_weights, t_plane=t_plane,
                          tiles_per_seq=tps, first_kept=first_kept),
        out_shape=tuple(out_shape),
        grid=(m // tm, n_planes * PROJ_SPLIT),
        in_specs=in_specs,
        out_specs=tuple(out_specs),
        compiler_params=_cparams(("arbitrary", "arbitrary"), vmem),
        name="in_proj_qk" if normed else "in_proj_vhbc",
    )(*args)


def _gather_rows(ref, starts, run):
    parts = [ref[pl.ds(s, run), :] for s in starts]
    return parts[0] if len(parts) == 1 else jnp.concatenate(parts, axis=0)


def _scatter_rows(ref, starts, run, val):
    for i, s in enumerate(starts):
        ref[pl.ds(s, run), :] = val[i * run:(i + 1) * run]


def _sample_scores(row, q_ref, kn_ref, kt_ref, bias_ref, bnew_ref):
    lb = kt_ref.shape[1]
    nd = len(DILATIONS)
    seg = (lax.broadcasted_iota(jnp.int32, (N_HEADS, D_ATTN), 1) // HEAD_DIM
           == lax.broadcasted_iota(jnp.int32, (N_HEADS, D_ATTN), 0))
    qe = jnp.where(seg, _cat_groups(q_ref, row), 0.0).astype(BF16)
    kn = _cat_groups(kn_ref, row).astype(BF16).astype(F32)
    s_new = jnp.sum(qe.astype(F32) * kn, axis=1, keepdims=True)
    sn = [s_new + bnew_ref[:, di:di + 1] for di in range(nd)]
    starts = range(0, lb, SAMPLE_POS_CHUNK)
    chunks = [pl.ds(c, SAMPLE_POS_CHUNK) for c in starts]
    sc = []
    for c, ch in zip(starts, chunks):
        s = jnp.dot(qe, kt_ref[:, ch].astype(BF16), preferred_element_type=F32)
        sc.append([s + bias_ref[:, pl.ds(di * lb + c, SAMPLE_POS_CHUNK)] for di in range(nd)])
    m = functools.reduce(jnp.maximum, [jnp.max(x, axis=1, keepdims=True) for row_ in sc for x in row_] + sn)
    pn = functools.reduce(jnp.add, [jnp.exp(x - m) for x in sn])
    ps = [functools.reduce(jnp.add, [jnp.exp(x - m) for x in sd]) for sd in sc]
    return seg, chunks, ps, pn


def _sample_output(row, state, vn_ref, vt_ref, o_ref):
    seg, chunks, ps, pn = state
    den = pn
    o = pn * _cat_groups(vn_ref, row)
    for ch, p in zip(chunks, ps):
        den = den + jnp.sum(p, axis=1, keepdims=True)
        o = o + lax.dot_general(p.astype(BF16), vt_ref[:, ch].astype(BF16),
                                (((1,), (1,)), ((), ())), preferred_element_type=F32)
    out = jnp.sum(jnp.where(seg, o / den, 0.0), axis=0, keepdims=True)
    for g in range(N_GROUPS):
        o_ref[g, row, :] = out[:, g * LANES:(g + 1) * LANES]


def _attn_kernel(q_ref, k_ref, v_ref, bias_ref, sq_ref, skn_ref, svn_ref, ck_hbm, cv_hbm, sbias_ref,
                 sbnew_ref, o_ref, so_ref, qp, kp, vp, acc, m_sc, l_sc, tmp, kbuf, vbuf, sem, *, samples_per_step):
    sb = pl.program_id(2)
    step = (pl.program_id(0) * pl.num_programs(1) + pl.program_id(1)) * pl.num_programs(2) + sb
    n_samples = ck_hbm.shape[0]
    slot = sb % 2
    pslot = 1 - slot
    kcur, vcur = kp.at[slot], vp.at[slot]
    kprev, vprev = kp.at[pslot], vp.at[pslot]

    def copies(g, slot):
        return (pltpu.make_async_copy(ck_hbm.at[g], kbuf.at[slot], sem.at[0, slot]),
                pltpu.make_async_copy(cv_hbm.at[g], vbuf.at[slot], sem.at[1, slot]))

    @pl.when(step == 0)
    def _():
        for c in copies(0, 0):
            c.start()

    def sample(t):
        g = step * samples_per_step + t
        slot = t % 2
        row = pl.ds(g, 1)

        @pl.when(g + 1 < n_samples)
        def _():
            for c in copies(g + 1, 1 - slot):
                c.start()

        for c in copies(g, slot):
            c.wait()
        state = _sample_scores(row, sq_ref, skn_ref, kbuf.at[slot], sbias_ref, sbnew_ref)
        _sample_output(row, state, svn_ref, vbuf.at[slot], so_ref)

    quarter = SUPER // 4
    for src, dst in ((q_ref, qp), (k_ref, kcur), (v_ref, vcur)):
        for c in range(4):
            tmp[pl.ds(c * quarter, quarter), :] = src[pl.ds(c, quarter, stride=4), :]
        for c in range(4):
            for b in range(4):
                dst[pl.ds((4 * b + c) * DIL_STEPS, DIL_STEPS), :] = tmp[pl.ds(c * quarter + b, DIL_STEPS, stride=4), :]

    @pl.when(sb == 0)
    def _():
        kprev[...] = jnp.zeros(kprev.shape, F32)
        vprev[...] = jnp.zeros(vprev.shape, F32)

    head0 = lax.broadcasted_iota(jnp.int32, (DIL_STEPS, LANES), 1) < HEAD_DIM

    def block(di, starts, run, prev_ref_k, prev_ref_v, prev_starts, first, mode, out_rows=None):
        qb = _gather_rows(qp, starts, run)
        kb = jnp.concatenate([_gather_rows(prev_ref_k, prev_starts, run),
                              _gather_rows(kcur, starts, run)], axis=0).astype(BF16)
        vb = jnp.concatenate([_gather_rows(prev_ref_v, prev_starts, run),
                              _gather_rows(vcur, starts, run)], axis=0).astype(BF16)
        ms, ls, os_ = [], [], []
        for h in range(HEADS_PER_GROUP):
            keep = head0 if h == 0 else jnp.logical_not(head0)
            qh = jnp.where(keep, qb, 0.0).astype(BF16)
            s = lax.dot_general(qh, kb, (((1,), (1,)), ((), ())), preferred_element_type=F32)
            s = s + bias_ref[h, di, first]
            mh = jnp.max(s, axis=1, keepdims=True)
            p = jnp.exp(s - mh)
            ls.append(jnp.sum(p, axis=1, keepdims=True))
            os_.append(jnp.dot(p.astype(BF16), vb, preferred_element_type=F32))
            ms.append(mh)
        m_c = jnp.where(head0, ms[0], ms[1])
        l_c = jnp.where(head0, ls[0], ls[1])
        o_c = jnp.where(head0, os_[0], os_[1])
        if mode == "init":
            _scatter_rows(m_sc, starts, run, m_c)
            _scatter_rows(l_sc, starts, run, l_c)
            _scatter_rows(acc, starts, run, o_c)
            return
        m_o = _gather_rows(m_sc, starts, run)
        m_n = jnp.maximum(m_o, m_c)
        a_o = jnp.exp(m_o - m_n)
        a_c = jnp.exp(m_c - m_n)
        l_n = _gather_rows(l_sc, starts, run) * a_o + l_c * a_c
        o_n = _gather_rows(acc, starts, run) * a_o + o_c * a_c
        if mode == "merge":
            _scatter_rows(m_sc, starts, run, m_n)
            _scatter_rows(l_sc, starts, run, l_n)
            _scatter_rows(acc, starts, run, o_n)
        else:
            o_ref[out_rows, :] = o_n / l_n

    def run_dilation(di, mode):
        d = DILATIONS[di]
        runs = N_RES // d
        run = DIL_STEPS // runs
        nblk = SUPER // d // DIL_STEPS

        def body(it, carry):
            c = it // nblk
            n = it % nblk
            starts = [pl.multiple_of((d * b + c) * DIL_STEPS + run * n, SUBLANES) for b in range(runs)]
            pn = jnp.where(n > 0, n - 1, nblk - 1)
            prev_starts = [pl.multiple_of((d * b + c) * DIL_STEPS + run * pn, SUBLANES)
                           for b in range(runs)]
            pslot_n = jnp.where(n > 0, slot, pslot)
            first = jnp.logical_and(sb == 0, n == 0).astype(jnp.int32)
            out_rows = pl.ds(it, DIL_STEPS, stride=N_RES) if mode == "final" else None
            block(di, starts, run, kp.at[pslot_n], vp.at[pslot_n], prev_starts, first, mode, out_rows)
            return carry

        lax.fori_loop(0, d * nblk, body, 0, unroll=ATTN_UNROLL)

    phases = ("init", "merge", "final")
    for t in range(max(samples_per_step, len(phases))):
        if t < samples_per_step:
            sample(t)
        if t < len(phases):
            run_dilation(t, phases[t])


def _attn(qk, vhbc, bias, batch, seq, sqk, svhbc, cache_k, cache_v, sbias, sbias_new):
    nsb = seq // SUPER
    m = batch * seq
    bd, lb = cache_k.shape[0], cache_k.shape[1]
    da = D_ATTN
    n_steps = N_GROUPS * batch * nsb
    assert bd % (2 * n_steps) == 0 and lb % SAMPLE_POS_CHUNK == 0
    assert N_RES == 16, "the residue-major copy is written as two stride-4 passes"

    def feature_major(c):
        return jnp.transpose(c, (0, 2, 3, 1)).reshape(bd, da, lb)

    def plane(p):
        return pl.BlockSpec((None, None, SUPER, LANES), lambda g, b, s: (p, g, b * nsb + s, 0))

    def splane(p):
        return pl.BlockSpec((None, N_GROUPS, bd, LANES), lambda g, b, s: (p, 0, 0, 0))

    fixed2 = lambda g, b, s: (0, 0)
    bias_spec = pl.BlockSpec((HEADS_PER_GROUP,) + bias.shape[1:], lambda g, b, s: (g, 0, 0, 0, 0),
                             pipeline_mode=pl.Buffered(1))
    hbm = pl.BlockSpec(memory_space=pl.ANY)
    blk_bytes = _nbytes((SUPER, LANES), F32)
    return pl.pallas_call(
        functools.partial(_attn_kernel, samples_per_step=bd // n_steps),
        out_shape=(jax.ShapeDtypeStruct((N_GROUPS, m, LANES), F32),
                   jax.ShapeDtypeStruct((N_GROUPS, bd, LANES), F32)),
        grid=(N_GROUPS, batch, nsb),
        in_specs=[plane(0), plane(1), plane(0), bias_spec,
                  splane(0), splane(1), splane(0), hbm, hbm,
                  pl.BlockSpec(sbias.shape, fixed2), pl.BlockSpec(sbias_new.shape, fixed2)],
        out_specs=(pl.BlockSpec((None, SUPER, LANES), lambda g, b, s: (g, b * nsb + s, 0)),
                   pl.BlockSpec((N_GROUPS, bd, LANES), lambda g, b, s: (0, 0, 0))),
        scratch_shapes=[pltpu.VMEM((SUPER, LANES), F32),
                        pltpu.VMEM((2, SUPER, LANES), F32), pltpu.VMEM((2, SUPER, LANES), F32),
                        pltpu.VMEM((SUPER, LANES), F32), pltpu.VMEM((SUPER, LANES), F32),
                        pltpu.VMEM((SUPER, LANES), F32), pltpu.VMEM((SUPER, LANES), F32),
                        pltpu.VMEM((2, da, lb), F32), pltpu.VMEM((2, da, lb), F32),
                        pltpu.SemaphoreType.DMA((2, 2))],
        compiler_params=_cparams(("arbitrary", "arbitrary", "arbitrary"),
                                 (2 * 4 + 9) * blk_bytes
                                 + _nbytes((HEADS_PER_GROUP,) + bias.shape[1:], F32)
                                 + 4 * _nbytes((da, lb), F32) + 2 * 4 * _nbytes((bd, da), F32)
                                 + 2 * _nbytes(sbias.shape, F32)),
        name="attn",
    )(qk, qk, vhbc, bias, sqk, sqk, svhbc, feature_major(cache_k), feature_major(cache_v), sbias, sbias_new)


def _mix_tail(x, a, conv, ga_ref, gc_ref, w_ref):
    cat = jnp.concatenate([_rms(a, ga_ref[...]), _rms(conv, gc_ref[...])], axis=1).astype(BF16)
    return x + jnp.dot(cat, w_ref[...], preferred_element_type=F32)


def _mix_prompt_kernel(x_ref, a_ref, h_ref, b_ref, c_ref, hh_ref, ch_ref, cw_ref, ga_ref, gc_ref, gf_ref, w_ref,
                       o_ref, n_ref, ut_ref, *, tiles_per_seq):
    tm = x_ref.shape[0]
    seq_start = pl.program_id(0) % tiles_per_seq == 0
    all8 = pl.ds(0, SUBLANES)
    halo = jnp.where(seq_start, 0.0, _cat_groups(hh_ref, all8) * _cat_groups(ch_ref, all8))
    for rows in _sub_tiles(tm, MIX_SUB_ROWS):
        u = _cat_groups(h_ref, rows) * _cat_groups(c_ref, rows)
        rid = lax.broadcasted_iota(jnp.int32, u.shape, 0)
        u1 = jnp.where(rid == 0, halo[7:8], pltpu.roll(u, 1, axis=0))
        u2 = jnp.where(rid == 0, halo[6:7], jnp.where(rid == 1, halo[7:8], pltpu.roll(u, 2, axis=0)))
        cy = cw_ref[0:1] * u2 + cw_ref[1:2] * u1 + cw_ref[2:3] * u
        conv = _cat_groups(b_ref, rows) * cy
        h1 = _mix_tail(x_ref[rows, :], _cat_groups(a_ref, rows), conv, ga_ref, gc_ref, w_ref)
        o_ref[rows, :] = h1
        n_ref[rows, :] = _rms(h1, gf_ref[...]).astype(BF16)
        halo = u[u.shape[0] - SUBLANES:]
    ut_ref[...] = halo


def _mix_sample_kernel(x_ref, a_ref, h_ref, b_ref, c_ref, buf_ref, cw_ref, ga_ref, gc_ref, gf_ref, w_ref,
                       o_ref, n_ref, u_ref):
    every = pl.ds(0, x_ref.shape[0])
    u = _cat_groups(h_ref, every) * _cat_groups(c_ref, every)
    dc = u.shape[1]
    cy = cw_ref[0:1] * buf_ref[:, :dc] + cw_ref[1:2] * buf_ref[:, dc:] + cw_ref[2:3] * u
    conv = _cat_groups(b_ref, every) * cy
    h1 = _mix_tail(x_ref[...], _cat_groups(a_ref, every), conv, ga_ref, gc_ref, w_ref)
    o_ref[...] = h1
    n_ref[...] = _rms(h1, gf_ref[...]).astype(BF16)
    u_ref[...] = u


def _mix_out_prompt(x, attn, vhbc, conv_w, ga, gc, gf, w_out, seq, tm):
    m, dm = x.shape
    dc = conv_w.shape[1]
    fixed = lambda i: (0, 0)
    halo_blocks = tm // SUBLANES

    def plane(p):
        return pl.BlockSpec((None, N_GROUPS, tm, LANES), lambda i: (p, 0, i, 0))

    def halo(p):
        return pl.BlockSpec((None, N_GROUPS, SUBLANES, LANES),
                            lambda i: (p, 0, jnp.maximum(i * halo_blocks - 1, 0), 0))

    tile = _nbytes((tm, dc), F32)
    return pl.pallas_call(
        functools.partial(_mix_prompt_kernel, tiles_per_seq=seq // tm),
        out_shape=(jax.ShapeDtypeStruct((m, dm), F32), jax.ShapeDtypeStruct((m, dm), BF16),
                   jax.ShapeDtypeStruct((m // tm, SUBLANES, dc), F32)),
        grid=(m // tm,),
        in_specs=[pl.BlockSpec((tm, dm), lambda i: (i, 0)),
                  pl.BlockSpec((N_GROUPS, tm, LANES), lambda i: (0, i, 0)),
                  plane(1), plane(2), plane(3), halo(1), halo(3),
                  pl.BlockSpec((CONV_W, dc), fixed), pl.BlockSpec((1, D_ATTN), fixed),
                  pl.BlockSpec((1, dc), fixed), pl.BlockSpec((1, dm), fixed), _resident(w_out.shape, fixed)],
        out_specs=(pl.BlockSpec((tm, dm), lambda i: (i, 0)), pl.BlockSpec((tm, dm), lambda i: (i, 0)),
                   pl.BlockSpec((None, SUBLANES, dc), lambda i: (i, 0, 0))),
        compiler_params=_cparams(("arbitrary",), 2 * 9 * tile + _nbytes(w_out.shape, BF16)
                                 + 4 * _nbytes((min(tm, MIX_SUB_ROWS), dm), F32)),
        name="mix_out_prompt",
    )(x, attn, vhbc, vhbc, vhbc, vhbc, vhbc, conv_w, ga, gc, gf, w_out)


def _mix_out_sample(x, attn, vhbc, buf, conv_w, ga, gc, gf, w_out):
    m, dm = x.shape
    dc = conv_w.shape[1]
    full = lambda a: pl.BlockSpec(a.shape, lambda i: (0,) * a.ndim)

    def plane(p):
        return pl.BlockSpec((None, N_GROUPS, m, LANES), lambda i: (p, 0, 0, 0))

    return pl.pallas_call(
        _mix_sample_kernel,
        out_shape=(jax.ShapeDtypeStruct((m, dm), F32), jax.ShapeDtypeStruct((m, dm), BF16),
                   jax.ShapeDtypeStruct((m, dc), F32)),
        grid=(1,),
        in_specs=[full(x), full(attn), plane(1), plane(2), plane(3), full(buf), full(conv_w), full(ga),
                  full(gc), full(gf), full(w_out)],
        out_specs=(pl.BlockSpec((m, dm), lambda i: (0, 0)), pl.BlockSpec((m, dm), lambda i: (0, 0)),
                   pl.BlockSpec((m, dc), lambda i: (0, 0))),
        compiler_params=_cparams(("arbitrary",), 2 * (11 * _nbytes((m, dc), F32) + _nbytes(w_out.shape, BF16))),
        name="mix_out_sample",
    )(x, attn, vhbc, vhbc, vhbc, buf, conv_w, ga, gc, gf, w_out)


def _ffn_kernel(n_ref, wg_ref, wu_ref, wd_ref, o_ref, *wcopy_refs):
    @pl.when(pl.program_id(1) == 0)
    def _():
        o_ref[...] = jnp.zeros(o_ref.shape, F32)

    n = n_ref[...]
    acc = None
    for c in range(0, wg_ref.shape[1], FFN_SUB_COLS):
        cols = pl.ds(c, FFN_SUB_COLS)
        wg, wu, wd = wg_ref[:, cols].astype(BF16), wu_ref[:, cols].astype(BF16), wd_ref[cols, :].astype(BF16)
        if wcopy_refs:
            wcopy_refs[0][:, cols], wcopy_refs[1][:, cols], wcopy_refs[2][cols, :] = wg, wu, wd
        gate = jnp.dot(n, wg, preferred_element_type=F32)
        up = jnp.dot(n, wu, preferred_element_type=F32)
        act = (gate / (1.0 + jnp.exp(-gate)) * up).astype(BF16)
        part = jnp.dot(act, wd, preferred_element_type=F32)
        acc = part if acc is None else acc + part
    o_ref[...] += acc


def _ffn(n, wg, wu, wd, tm, tf, emit_bf16_weights=False):
    m, dm = n.shape
    dff = wg.shape[1]
    row = lambda i, f: (i, 0)
    w_specs = [pl.BlockSpec((dm, tf), lambda i, f: (0, f)), pl.BlockSpec((dm, tf), lambda i, f: (0, f)),
               pl.BlockSpec((tf, dm), lambda i, f: (f, 0))]
    out_shape = [jax.ShapeDtypeStruct((m, dm), F32)]
    out_specs = [pl.BlockSpec((tm, dm), row)]
    if emit_bf16_weights:
        assert m == tm, "weight copies are written once: one row tile"
        out_shape += [jax.ShapeDtypeStruct(w.shape, BF16) for w in (wg, wu, wd)]
        out_specs += w_specs
    out = pl.pallas_call(
        _ffn_kernel,
        out_shape=tuple(out_shape),
        grid=(m // tm, dff // tf),
        in_specs=[pl.BlockSpec((tm, dm), row)] + w_specs,
        out_specs=tuple(out_specs),
        compiler_params=_cparams(("arbitrary", "arbitrary"),
                                 2 * (_nbytes((tm, dm), F32) + _nbytes((tm, dm), BF16))
                                 + 2 * 3 * (_nbytes((dm, tf), wg.dtype)
                                            + (_nbytes((dm, tf), BF16) if emit_bf16_weights else 0))
                                 + 4 * _nbytes((tm, tf), F32)),
        name="ffn",
    )(n, wg, wu, wd)
    return out if emit_bf16_weights else out[0]


def _ple_kernel(h_ref, d_ref, p_ref, g_ref, wg_ref, wp_ref, o_ref):
    for rows in _sub_tiles(h_ref.shape[0], PLE_SUB_ROWS):
        x = h_ref[rows, :] + d_ref[rows, :]
        z = jnp.dot(_rms(x, g_ref[...]).astype(BF16), wg_ref[...], preferred_element_type=F32)
        e = jnp.dot(p_ref[rows, :].astype(BF16), wp_ref[...], preferred_element_type=F32)
        o_ref[rows, :] = x + e / (1.0 + jnp.exp(-z))


def _ple(h, d, p, g, wg, wp, tm):
    m, dm = h.shape
    dp = p.shape[1]
    row = lambda i: (i, 0)
    fixed = lambda i: (0, 0)
    return pl.pallas_call(
        _ple_kernel,
        out_shape=jax.ShapeDtypeStruct((m, dm), F32),
        grid=(m // tm,),
        in_specs=[pl.BlockSpec((tm, dm), row), pl.BlockSpec((tm, dm), row), pl.BlockSpec((tm, dp), row),
                  pl.BlockSpec((1, dm), fixed), _resident(wg.shape, fixed), _resident(wp.shape, fixed)],
        out_specs=pl.BlockSpec((tm, dm), row),
        compiler_params=_cparams(("arbitrary",),
                                 2 * (3 * _nbytes((tm, dm), F32) + _nbytes((tm, dp), F32))
                                 + _nbytes(wg.shape, BF16) + _nbytes(wp.shape, BF16)
                                 + 6 * _nbytes((min(tm, PLE_SUB_ROWS), dm), F32)),
        name="ple",
    )(h, d, p, g, wg, wp)


def _tiles(m):
    return dict(proj=min(m, 1024), mix=min(m, 512), ffn=min(m, 1024), ffn_cols=512, ple=min(m, 512))


def _window_rows(t, batch, kw):
    return jnp.transpose(t.reshape(batch, N_HEADS, HEAD_DIM, kw), (0, 3, 1, 2))


def kernel(x_prompt, x_sample, p_prompt, p_sample, cache_k, cache_v, state_conv, rel_bias, g_mix, w_in,
           q_norm, k_norm, conv_w, g_attn_out, g_conv_out, w_out, g_ffn, w_gate, w_up, w_down, g_ple,
           w_ple_gate, w_ple_proj):
    depth = g_mix.shape[0]
    batch, seq, dm = x_prompt.shape
    bd, dec_seq, _ = x_sample.shape
    dc = conv_w.shape[2]
    assert depth == 1 and dec_seq == 1, "single layer, one new position per sample"
    assert seq % SUPER == 0 and dm == D_ATTN + dc and dc == D_ATTN
    assert w_gate.shape[2] % 512 == 0

    bias_p, bias_s, bias_s_new = _bias_tables(rel_bias, cache_k.shape[2])
    gmat = jnp.asarray(np.kron(np.eye(PROJ_COLS // HEAD_DIM, dtype=np.float32),
                               np.full((HEAD_DIM, HEAD_DIM), 1.0 / HEAD_DIM, np.float32)), BF16)

    i = 0
    row2 = lambda a: a.reshape(1, -1).astype(F32)
    g_mix_i, g_ffn_i, g_ple_i = row2(g_mix[i]), row2(g_ffn[i]), row2(g_ple[i])
    qk_gains = jnp.stack([row2(jnp.tile(q_norm[i], N_HEADS) * SCALE), row2(jnp.tile(k_norm[i], N_HEADS))])
    ga, gc = row2(g_attn_out[i]), row2(g_conv_out[i])
    cw = conv_w[i].astype(F32)
    w_out_i = w_out[i].astype(BF16)
    wpg_i, wpp_i = w_ple_gate[i].astype(BF16), w_ple_proj[i].astype(BF16)

    mp = batch * seq
    tp, ts = _tiles(mp), _tiles(bd)
    kw = min(SUPER, seq)
    xp = x_prompt.reshape(mp, dm)
    xs = x_sample.reshape(bd, dm)
    qks, kts, w_qk, nrm_s = _proj(xs, w_in[i], 0, 2, 1, bd, bd, ts["proj"], g_mix_i, qk_gains, gmat,
                                  emit_bf16_weights=True)
    vhbcs, vts, w_vhbc = _proj(nrm_s, w_in[i], 2, 4, 0, bd, bd, ts["proj"], emit_bf16_weights=True)
    qk, kt, nrm = _proj(xp, w_qk, 0, 2, 1, seq, kw, tp["proj"], g_mix_i, qk_gains, gmat)
    vhbc, vt = _proj(nrm, w_vhbc, 0, 4, 0, seq, kw, tp["proj"])
    attn, attn_s = _attn(qk, vhbc, bias_p, batch, seq, qks, vhbcs, cache_k[i], cache_v[i], bias_s, bias_s_new)

    buf = state_conv[i].astype(F32)
    hs, ns, us = _mix_out_sample(xs, attn_s, vhbcs, buf.reshape(bd, (CONV_W - 1) * dc), cw, ga, gc, g_ffn_i,
                                 w_out_i)
    ds, wg_i, wu_i, wd_i = _ffn(ns, w_gate[i], w_up[i], w_down[i], ts["ffn"], ts["ffn_cols"],
                                emit_bf16_weights=True)
    hs = _ple(hs, ds, p_sample[i].reshape(bd, -1), g_ple_i, wpg_i, wpp_i, ts["ple"])

    h, n, u_tail = _mix_out_prompt(xp, attn, vhbc, cw, ga, gc, g_ffn_i, w_out_i, seq, tp["mix"])
    d = _ffn(n, wg_i, wu_i, wd_i, tp["ffn"], tp["ffn_cols"])
    h = _ple(h, d, p_prompt[i].reshape(mp, -1), g_ple_i, wpg_i, wpp_i, tp["ple"])
    y_prompt = h.reshape(batch, seq, dm)
    k_prompt = _window_rows(kt, batch, kw)[None]
    v_prompt = _window_rows(vt, batch, kw)[None]
    tiles_per_seq = seq // tp["mix"]
    conv_prompt = u_tail.reshape(batch, tiles_per_seq, SUBLANES, dc)[None, :, -1, SUBLANES - (CONV_W - 1):]

    y_sample = hs.reshape(bd, dec_seq, dm)
    k_sample = _window_rows(kts, 1, bd).reshape(1, bd, dec_seq, N_HEADS, HEAD_DIM)
    v_sample = _window_rows(vts, 1, bd).reshape(1, bd, dec_seq, N_HEADS, HEAD_DIM)
    conv_sample = jnp.concatenate([buf[:, 1:], us[:, None, :]], axis=1)[None]

    return (y_prompt, y_sample, k_prompt, v_prompt, conv_prompt, k_sample, v_sample, conv_sample)
```

```python
import functools
import math

import numpy as np
import jax
import jax.numpy as jnp
from jax import lax
from jax.experimental import pallas as pl
from jax.experimental.pallas import tpu as pltpu

HEAD_DIM = 64
N_HEADS = 16
D_ATTN = N_HEADS * HEAD_DIM
CONV_W = 3
DIL_STEPS = 128
DILATIONS = (1, 4, 16)
N_BUCKETS = 32
MAX_EXACT = N_BUCKETS // 2
MAX_DIST = DIL_STEPS * max(DILATIONS)
EPS = 1e-6
SCALE = HEAD_DIM ** -0.5

LANES = 128
SUBLANES = 8
BF16_ROWS = 16
HEADS_PER_GROUP = LANES // HEAD_DIM
N_GROUPS = N_HEADS // HEADS_PER_GROUP
SUPER = DIL_STEPS * max(DILATIONS)
N_RES = max(DILATIONS)
MASK_VALUE = -1e30
MASK_BUCKET = N_BUCKETS
TABLE_ROWS = -(-(N_BUCKETS + 1) // BF16_ROWS) * BF16_ROWS
TABLE_CHUNK = 2048
TABLE_UNROLL = 4
SAMPLE_POS_CHUNK = 512
ATTN_UNROLL = 16
PROJ_COLS = 512
PROJ_SPLIT = D_ATTN // PROJ_COLS
MIX_SUB_ROWS = 256
PLE_SUB_ROWS = 512
FFN_SUB_COLS = 256
VMEM_SLACK_BYTES = 8 * 1024 * 1024

F32 = jnp.float32
BF16 = jnp.bfloat16


def _cparams(sem, vmem_bytes):
    return pltpu.CompilerParams(dimension_semantics=sem,
                                vmem_limit_bytes=int(vmem_bytes + VMEM_SLACK_BYTES))


def _nbytes(shape, dtype):
    return int(np.prod(shape)) * jnp.dtype(dtype).itemsize


def _resident(shape, index_map):
    return pl.BlockSpec(shape, index_map, pipeline_mode=pl.Buffered(1))


def _t5_bucket(dist):
    n = np.asarray(dist, np.int32)
    nf = np.maximum(n, 1).astype(np.float32)
    large = MAX_EXACT + (np.log(nf / MAX_EXACT) / np.float32(math.log(MAX_DIST / MAX_EXACT))
                         * (N_BUCKETS - MAX_EXACT)).astype(np.int32)
    large = np.minimum(large, N_BUCKETS - 1)
    return np.where(n < MAX_EXACT, n, large).astype(np.int32)


def _stored_to_natural(d):
    runs = N_RES // d
    run = DIL_STEPS // runs
    j = np.arange(DIL_STEPS)
    return (j % run) * runs + j // run


def _prompt_bias_index():
    out = np.empty((len(DILATIONS), 2, DIL_STEPS, 2 * DIL_STEPS), np.int32)
    for di, d in enumerate(DILATIONS):
        nat = _stored_to_natural(d)
        qi = nat[:, None]
        kj = np.concatenate([nat, nat + DIL_STEPS])[None, :]
        steps = DIL_STEPS + qi - kj
        band = (steps >= 0) & (steps <= DIL_STEPS)
        bucket = _t5_bucket(d * np.clip(steps, 0, DIL_STEPS))
        out[di, 0] = np.where(band, bucket, MASK_BUCKET)
        out[di, 1] = np.where(band & (kj >= DIL_STEPS), bucket, MASK_BUCKET)
    return out


def _sample_bias_index(lb):
    back = lb - np.arange(lb)
    cached = np.stack([np.where((back % d == 0) & (back // d <= DIL_STEPS), _t5_bucket(back), MASK_BUCKET)
                       for d in DILATIONS])
    new = np.stack([_t5_bucket(d * np.zeros(1, np.int32)) for d in DILATIONS])
    return cached, new


def _bias_table_kernel(tab_ref, idx_ref, o_ref):
    t = tab_ref[...]
    hi = t.astype(BF16)
    r1 = t - hi.astype(F32)
    mid = r1.astype(BF16)
    lo = (r1 - mid.astype(F32)).astype(BF16)
    rows = lax.broadcasted_iota(jnp.int32, (TABLE_ROWS, TABLE_CHUNK), 0)

    def chunk(c, carry):
        sl = pl.ds(pl.multiple_of(c * TABLE_CHUNK, TABLE_CHUNK), TABLE_CHUNK)
        onehot = jnp.where(rows == idx_ref[:, sl], 1.0, 0.0).astype(BF16)
        acc = jnp.dot(hi, onehot, preferred_element_type=F32)
        acc = acc + jnp.dot(mid, onehot, preferred_element_type=F32)
        acc = acc + jnp.dot(lo, onehot, preferred_element_type=F32)
        o_ref[:, sl] = acc
        return carry

    lax.fori_loop(0, o_ref.shape[1] // TABLE_CHUNK, chunk, 0, unroll=TABLE_UNROLL)


def _bias_tables(rel_bias, lb):
    pidx = _prompt_bias_index()
    cached, new = _sample_bias_index(lb)
    flat = np.concatenate([pidx.reshape(-1), cached.reshape(-1), new.reshape(-1)])
    n_pad = -(-(pidx.size + cached.size + LANES) // TABLE_CHUNK) * TABLE_CHUNK
    idx = np.full((1, n_pad), MASK_BUCKET, np.int32)
    idx[0, :flat.size] = flat
    tab = jnp.concatenate(
        [rel_bias.astype(F32).T,
         jnp.full((N_HEADS, 1), MASK_VALUE, F32),
         jnp.zeros((N_HEADS, TABLE_ROWS - N_BUCKETS - 1), F32)], axis=1)
    out = pl.pallas_call(
        _bias_table_kernel,
        out_shape=jax.ShapeDtypeStruct((N_HEADS, n_pad), F32),
        grid=(1,),
        in_specs=[pl.BlockSpec((N_HEADS, TABLE_ROWS), lambda i: (0, 0)),
                  pl.BlockSpec((1, n_pad), lambda i: (0, 0))],
        out_specs=pl.BlockSpec((N_HEADS, n_pad), lambda i: (0, 0)),
        compiler_params=_cparams(("arbitrary",), 2 * (_nbytes((N_HEADS, n_pad), F32)
                                                      + _nbytes((SUBLANES, n_pad), jnp.int32))),
        name="bias_table",
    )(tab, jnp.asarray(idx))
    n_p = pidx.size
    prompt = out[:, :n_p].reshape(N_HEADS, len(DILATIONS), 2, DIL_STEPS, 2 * DIL_STEPS)
    samp = out[:, n_p:n_p + cached.size]
    samp_new = out[:, n_p + cached.size:n_p + cached.size + LANES]
    return prompt, samp, samp_new


def _rms(x, g):
    return x * lax.rsqrt(jnp.mean(x * x, axis=-1, keepdims=True) + EPS) * g


def _head_rms(p, g, gmat):
    ms = jnp.dot((p * p).astype(BF16), gmat, preferred_element_type=F32)
    return p * lax.rsqrt(ms + EPS) * g


def _cat_groups(ref, rows):
    return jnp.concatenate([ref[g, rows, :] for g in range(N_GROUPS)], axis=1)


def _sub_tiles(tm, sub):
    sub = min(tm, sub)
    return [pl.ds(r, sub) for r in range(0, tm, sub)]


def _proj_kernel(*refs, normed, emit_w, t_plane, tiles_per_seq, first_kept):
    i, j = pl.program_id(0), pl.program_id(1)
    if normed:
        x_ref, g_ref, w_ref, gain_ref, gmat_ref, o_ref, t_ref, *rest = refs
        n_ref = rest[-1]

        @pl.when(j == 0)
        def _():
            n_ref[...] = _rms(x_ref[...], g_ref[...]).astype(BF16)
    else:
        n_ref, w_ref, o_ref, t_ref, *rest = refs

    w = w_ref[...].astype(BF16)
    if emit_w:
        rest[0][...] = w
    y = jnp.dot(n_ref[...], w, preferred_element_type=F32)
    if normed:
        y = _head_rms(y, gain_ref[...], gmat_ref[...])
    for g in range(PROJ_COLS // LANES):
        o_ref[g] = y[:, g * LANES:(g + 1) * LANES]

    @pl.when(jnp.logical_and(j // PROJ_SPLIT == t_plane, i % tiles_per_seq >= first_kept))
    def _():
        t_ref[...] = y.T


def _proj(x, w_in, plane0, n_planes, t_plane, seq, kw, tm, g_mix=None, gains=None, gmat=None,
          emit_bf16_weights=False):
    m, dm = x.shape
    da = D_ATTN
    assert seq % tm == 0 and kw % tm == 0 and m % seq == 0
    assert not emit_bf16_weights or m == tm, "weight copies are written once: one row tile"
    tps, first_kept, batch = seq // tm, (seq - kw) // tm, m // seq
    normed = gains is not None
    grp = PROJ_COLS // LANES
    w_spec = pl.BlockSpec((dm, PROJ_COLS), lambda i, j: (0, plane0 * PROJ_SPLIT + j))
    if normed:
        in_specs = [pl.BlockSpec((tm, dm), lambda i, j: (i, 0)), pl.BlockSpec((1, dm), lambda i, j: (0, 0)), w_spec,
                    pl.BlockSpec((None, 1, PROJ_COLS), lambda i, j: (j, 0, 0)),
                    _resident((PROJ_COLS, PROJ_COLS), lambda i, j: (0, 0))]
        args = [x, g_mix, w_in, gains.reshape(n_planes * PROJ_SPLIT, 1, PROJ_COLS), gmat]
    else:
        in_specs = [pl.BlockSpec((tm, dm), lambda i, j: (i, 0)), w_spec]
        args = [x, w_in]

    def t_index(i, j):
        kept = i % tps >= first_kept
        half = jnp.where(kept, jnp.clip(j - t_plane * PROJ_SPLIT, 0, PROJ_SPLIT - 1), 0)
        return (i // tps, half, jnp.maximum(i % tps - first_kept, 0))

    out_specs = [pl.BlockSpec((None, grp, tm, LANES), lambda i, j: (j // PROJ_SPLIT, j % PROJ_SPLIT, i, 0)),
                 pl.BlockSpec((None, PROJ_COLS, tm), t_index)]
    out_shape = [jax.ShapeDtypeStruct((n_planes, N_GROUPS, m, LANES), F32),
                 jax.ShapeDtypeStruct((batch, da, kw), F32)]
    if emit_bf16_weights:
        out_specs.append(pl.BlockSpec((dm, PROJ_COLS), lambda i, j: (0, j)))
        out_shape.append(jax.ShapeDtypeStruct((dm, n_planes * da), BF16))
    if normed:
        out_specs.append(pl.BlockSpec((tm, dm), lambda i, j: (i, 0)))
        out_shape.append(jax.ShapeDtypeStruct((m, dm), BF16))
    vmem = (2 * (_nbytes((tm, dm), x.dtype) + _nbytes((dm, PROJ_COLS), w_in.dtype) + _nbytes((dm, PROJ_COLS), BF16)
                 + 2 * _nbytes((tm, PROJ_COLS), F32) + _nbytes((tm, dm), BF16))
            + 6 * _nbytes((tm, PROJ_COLS), F32))
    return pl.pallas_call(
        functools.partial(_proj_kernel, normed=normed, emit_w=emit_bf16_weights, t_plane=t_plane,
                          tiles_per_seq=tps, first_kept=first_kept),
        out_shape=tuple(out_shape),
        grid=(m // tm, n_planes * PROJ_SPLIT),
        in_specs=in_specs,
        out_specs=tuple(out_specs),
        compiler_params=_cparams(("arbitrary", "arbitrary"), vmem),
        name="in_proj_qk" if normed else "in_proj_vhbc",
    )(*args)


def _gather_rows(ref, starts, run):
    parts = [ref[pl.ds(s, run), :] for s in starts]
    return parts[0] if len(parts) == 1 else jnp.concatenate(parts, axis=0)


def _scatter_rows(ref, starts, run, val):
    for i, s in enumerate(starts):
        ref[pl.ds(s, run), :] = val[i * run:(i + 1) * run]


def _sample_scores(row, q_ref, kn_ref, kt_ref, bias_ref, bnew_ref):
    lb = kt_ref.shape[1]
    nd = len(DILATIONS)
    seg = (lax.broadcasted_iota(jnp.int32, (N_HEADS, D_ATTN), 1) // HEAD_DIM
           == lax.broadcasted_iota(jnp.int32, (N_HEADS, D_ATTN), 0))
    qe = jnp.where(seg, _cat_groups(q_ref, row), 0.0).astype(BF16)
    kn = _cat_groups(kn_ref, row).astype(BF16).astype(F32)
    s_new = jnp.sum(qe.astype(F32) * kn, axis=1, keepdims=True)
    sn = [s_new + bnew_ref[:, di:di + 1] for di in range(nd)]
    starts = range(0, lb, SAMPLE_POS_CHUNK)
    chunks = [pl.ds(c, SAMPLE_POS_CHUNK) for c in starts]
    sc = []
    for c, ch in zip(starts, chunks):
        s = jnp.dot(qe, kt_ref[:, ch].astype(BF16), preferred_element_type=F32)
        sc.append([s + bias_ref[:, pl.ds(di * lb + c, SAMPLE_POS_CHUNK)] for di in range(nd)])
    m = functools.reduce(jnp.maximum, [jnp.max(x, axis=1, keepdims=True) for row_ in sc for x in row_] + sn)
    pn = functools.reduce(jnp.add, [jnp.exp(x - m) for x in sn])
    ps = [functools.reduce(jnp.add, [jnp.exp(x - m) for x in sd]) for sd in sc]
    return seg, chunks, ps, pn


def _sample_output(row, state, vn_ref, vt_ref, o_ref):
    seg, chunks, ps, pn = state
    den = pn
    o = pn * _cat_groups(vn_ref, row)
    for ch, p in zip(chunks, ps):
        den = den + jnp.sum(p, axis=1, keepdims=True)
        o = o + lax.dot_general(p.astype(BF16), vt_ref[:, ch].astype(BF16),
                                (((1,), (1,)), ((), ())), preferred_element_type=F32)
    out = jnp.sum(jnp.where(seg, o / den, 0.0), axis=0, keepdims=True)
    for g in range(N_GROUPS):
        o_ref[g, row, :] = out[:, g * LANES:(g + 1) * LANES]


def _attn_kernel(q_ref, k_ref, v_ref, bias_ref, sq_ref, skn_ref, svn_ref, ck_hbm, cv_hbm, sbias_ref,
                 sbnew_ref, o_ref, so_ref, qp, kp, vp, acc, m_sc, l_sc, tmp, kbuf, vbuf, sem, *, samples_per_step):
    sb = pl.program_id(2)
    step = (pl.program_id(0) * pl.num_programs(1) + pl.program_id(1)) * pl.num_programs(2) + sb
    n_samples = ck_hbm.shape[0]
    slot = sb % 2
    pslot = 1 - slot
    kcur, vcur = kp.at[slot], vp.at[slot]
    kprev, vprev = kp.at[pslot], vp.at[pslot]

    def copies(g, slot):
        return (pltpu.make_async_copy(ck_hbm.at[g], kbuf.at[slot], sem.at[0, slot]),
                pltpu.make_async_copy(cv_hbm.at[g], vbuf.at[slot], sem.at[1, slot]))

    @pl.when(step == 0)
    def _():
        for c in copies(0, 0):
            c.start()

    def sample(t):
        g = step * samples_per_step + t
        slot = t % 2
        row = pl.ds(g, 1)

        @pl.when(g + 1 < n_samples)
        def _():
            for c in copies(g + 1, 1 - slot):
                c.start()

        for c in copies(g, slot):
            c.wait()
        state = _sample_scores(row, sq_ref, skn_ref, kbuf.at[slot], sbias_ref, sbnew_ref)
        _sample_output(row, state, svn_ref, vbuf.at[slot], so_ref)

    quarter = SUPER // 4
    for src, dst in ((q_ref, qp), (k_ref, kcur), (v_ref, vcur)):
        for c in range(4):
            tmp[pl.ds(c * quarter, quarter), :] = src[pl.ds(c, quarter, stride=4), :]
        for c in range(4):
            for b in range(4):
                dst[pl.ds((4 * b + c) * DIL_STEPS, DIL_STEPS), :] = tmp[pl.ds(c * quarter + b, DIL_STEPS, stride=4), :]

    @pl.when(sb == 0)
    def _():
        kprev[...] = jnp.zeros(kprev.shape, F32)
        vprev[...] = jnp.zeros(vprev.shape, F32)

    head0 = lax.broadcasted_iota(jnp.int32, (DIL_STEPS, LANES), 1) < HEAD_DIM

    def block(di, starts, run, prev_ref_k, prev_ref_v, prev_starts, first, mode, out_rows=None):
        qb = _gather_rows(qp, starts, run)
        kb = jnp.concatenate([_gather_rows(prev_ref_k, prev_starts, run),
                              _gather_rows(kcur, starts, run)], axis=0).astype(BF16)
        vb = jnp.concatenate([_gather_rows(prev_ref_v, prev_starts, run),
                              _gather_rows(vcur, starts, run)], axis=0).astype(BF16)
        ms, ls, os_ = [], [], []
        for h in range(HEADS_PER_GROUP):
            keep = head0 if h == 0 else jnp.logical_not(head0)
            qh = jnp.where(keep, qb, 0.0).astype(BF16)
            s = lax.dot_general(qh, kb, (((1,), (1,)), ((), ())), preferred_element_type=F32)
            s = s + bias_ref[h, di, first]
            mh = jnp.max(s, axis=1, keepdims=True)
            p = jnp.exp(s - mh)
            ls.append(jnp.sum(p, axis=1, keepdims=True))
            os_.append(jnp.dot(p.astype(BF16), vb, preferred_element_type=F32))
            ms.append(mh)
        m_c = jnp.where(head0, ms[0], ms[1])
        l_c = jnp.where(head0, ls[0], ls[1])
        o_c = jnp.where(head0, os_[0], os_[1])
        if mode == "init":
            _scatter_rows(m_sc, starts, run, m_c)
            _scatter_rows(l_sc, starts, run, l_c)
            _scatter_rows(acc, starts, run, o_c)
            return
        m_o = _gather_rows(m_sc, starts, run)
        m_n = jnp.maximum(m_o, m_c)
        a_o = jnp.exp(m_o - m_n)
        a_c = jnp.exp(m_c - m_n)
        l_n = _gather_rows(l_sc, starts, run) * a_o + l_c * a_c
        o_n = _gather_rows(acc, starts, run) * a_o + o_c * a_c
        if mode == "merge":
            _scatter_rows(m_sc, starts, run, m_n)
            _scatter_rows(l_sc, starts, run, l_n)
            _scatter_rows(acc, starts, run, o_n)
        else:
            o_ref[out_rows, :] = o_n / l_n

    def run_dilation(di, mode):
        d = DILATIONS[di]
        runs = N_RES // d
        run = DIL_STEPS // runs
        nblk = SUPER // d // DIL_STEPS

        def body(it, carry):
            c = it // nblk
            n = it % nblk
            starts = [pl.multiple_of((d * b + c) * DIL_STEPS + run * n, SUBLANES) for b in range(runs)]
            pn = jnp.where(n > 0, n - 1, nblk - 1)
            prev_starts = [pl.multiple_of((d * b + c) * DIL_STEPS + run * pn, SUBLANES)
                           for b in range(runs)]
            pslot_n = jnp.where(n > 0, slot, pslot)
            first = jnp.logical_and(sb == 0, n == 0).astype(jnp.int32)
            out_rows = pl.ds(it, DIL_STEPS, stride=N_RES) if mode == "final" else None
            block(di, starts, run, kp.at[pslot_n], vp.at[pslot_n], prev_starts, first, mode, out_rows)
            return carry

        lax.fori_loop(0, d * nblk, body, 0, unroll=ATTN_UNROLL)

    phases = ("init", "merge", "final")
    for t in range(max(samples_per_step, len(phases))):
        if t < samples_per_step:
            sample(t)
        if t < len(phases):
            run_dilation(t, phases[t])


def _attn(qk, vhbc, bias, batch, seq, sqk, svhbc, cache_k, cache_v, sbias, sbias_new):
    nsb = seq // SUPER
    m = batch * seq
    bd, lb = cache_k.shape[0], cache_k.shape[1]
    da = D_ATTN
    n_steps = N_GROUPS * batch * nsb
    assert bd % (2 * n_steps) == 0 and lb % SAMPLE_POS_CHUNK == 0
    assert N_RES == 16, "the residue-major copy is written as two stride-4 passes"

    def feature_major(c):
        return jnp.transpose(c, (0, 2, 3, 1)).reshape(bd, da, lb)

    def plane(p):
        return pl.BlockSpec((None, None, SUPER, LANES), lambda g, b, s: (p, g, b * nsb + s, 0))

    def splane(p):
        return pl.BlockSpec((None, N_GROUPS, bd, LANES), lambda g, b, s: (p, 0, 0, 0))

    fixed2 = lambda g, b, s: (0, 0)
    bias_spec = pl.BlockSpec((HEADS_PER_GROUP,) + bias.shape[1:], lambda g, b, s: (g, 0, 0, 0, 0),
                             pipeline_mode=pl.Buffered(1))
    hbm = pl.BlockSpec(memory_space=pl.ANY)
    blk_bytes = _nbytes((SUPER, LANES), F32)
    return pl.pallas_call(
        functools.partial(_attn_kernel, samples_per_step=bd // n_steps),
        out_shape=(jax.ShapeDtypeStruct((N_GROUPS, m, LANES), F32),
                   jax.ShapeDtypeStruct((N_GROUPS, bd, LANES), F32)),
        grid=(N_GROUPS, batch, nsb),
        in_specs=[plane(0), plane(1), plane(0), bias_spec,
                  splane(0), splane(1), splane(0), hbm, hbm,
                  pl.BlockSpec(sbias.shape, fixed2), pl.BlockSpec(sbias_new.shape, fixed2)],
        out_specs=(pl.BlockSpec((None, SUPER, LANES), lambda g, b, s: (g, b * nsb + s, 0)),
                   pl.BlockSpec((N_GROUPS, bd, LANES), lambda g, b, s: (0, 0, 0))),
        scratch_shapes=[pltpu.VMEM((SUPER, LANES), F32),
                        pltpu.VMEM((2, SUPER, LANES), F32), pltpu.VMEM((2, SUPER, LANES), F32),
                        pltpu.VMEM((SUPER, LANES), F32), pltpu.VMEM((SUPER, LANES), F32),
                        pltpu.VMEM((SUPER, LANES), F32), pltpu.VMEM((SUPER, LANES), F32),
                        pltpu.VMEM((2, da, lb), F32), pltpu.VMEM((2, da, lb), F32),
                        pltpu.SemaphoreType.DMA((2, 2))],
        compiler_params=_cparams(("arbitrary", "arbitrary", "arbitrary"),
                                 (2 * 4 + 9) * blk_bytes
                                 + _nbytes((HEADS_PER_GROUP,) + bias.shape[1:], F32)
                                 + 4 * _nbytes((da, lb), F32) + 2 * 4 * _nbytes((bd, da), F32)
                                 + 2 * _nbytes(sbias.shape, F32)),
        name="attn",
    )(qk, qk, vhbc, bias, sqk, sqk, svhbc, feature_major(cache_k), feature_major(cache_v), sbias, sbias_new)


def _mix_tail(x, a, conv, ga_ref, gc_ref, w_ref):
    cat = jnp.concatenate([_rms(a, ga_ref[...]), _rms(conv, gc_ref[...])], axis=1).astype(BF16)
    return x + jnp.dot(cat, w_ref[...], preferred_element_type=F32)


def _mix_prompt_kernel(x_ref, a_ref, h_ref, b_ref, c_ref, hh_ref, ch_ref, cw_ref, ga_ref, gc_ref, gf_ref, w_ref,
                       o_ref, n_ref, ut_ref, *, tiles_per_seq):
    tm = x_ref.shape[0]
    seq_start = pl.program_id(0) % tiles_per_seq == 0
    all8 = pl.ds(0, SUBLANES)
    halo = jnp.where(seq_start, 0.0, _cat_groups(hh_ref, all8) * _cat_groups(ch_ref, all8))
    for rows in _sub_tiles(tm, MIX_SUB_ROWS):
        u = _cat_groups(h_ref, rows) * _cat_groups(c_ref, rows)
        rid = lax.broadcasted_iota(jnp.int32, u.shape, 0)
        u1 = jnp.where(rid == 0, halo[7:8], pltpu.roll(u, 1, axis=0))
        u2 = jnp.where(rid == 0, halo[6:7], jnp.where(rid == 1, halo[7:8], pltpu.roll(u, 2, axis=0)))
        cy = cw_ref[0:1] * u2 + cw_ref[1:2] * u1 + cw_ref[2:3] * u
        conv = _cat_groups(b_ref, rows) * cy
        h1 = _mix_tail(x_ref[rows, :], _cat_groups(a_ref, rows), conv, ga_ref, gc_ref, w_ref)
        o_ref[rows, :] = h1
        n_ref[rows, :] = _rms(h1, gf_ref[...]).astype(BF16)
        halo = u[u.shape[0] - SUBLANES:]
    ut_ref[...] = halo


def _mix_sample_kernel(x_ref, a_ref, h_ref, b_ref, c_ref, buf_ref, cw_ref, ga_ref, gc_ref, gf_ref, w_ref,
                       o_ref, n_ref, u_ref):
    every = pl.ds(0, x_ref.shape[0])
    u = _cat_groups(h_ref, every) * _cat_groups(c_ref, every)
    dc = u.shape[1]
    cy = cw_ref[0:1] * buf_ref[:, :dc] + cw_ref[1:2] * buf_ref[:, dc:] + cw_ref[2:3] * u
    conv = _cat_groups(b_ref, every) * cy
    h1 = _mix_tail(x_ref[...], _cat_groups(a_ref, every), conv, ga_ref, gc_ref, w_ref)
    o_ref[...] = h1
    n_ref[...] = _rms(h1, gf_ref[...]).astype(BF16)
    u_ref[...] = u


def _mix_out_prompt(x, attn, vhbc, conv_w, ga, gc, gf, w_out, seq, tm):
    m, dm = x.shape
    dc = conv_w.shape[1]
    fixed = lambda i: (0, 0)
    halo_blocks = tm // SUBLANES

    def plane(p):
        return pl.BlockSpec((None, N_GROUPS, tm, LANES), lambda i: (p, 0, i, 0))

    def halo(p):
        return pl.BlockSpec((None, N_GROUPS, SUBLANES, LANES),
                            lambda i: (p, 0, jnp.maximum(i * halo_blocks - 1, 0), 0))

    tile = _nbytes((tm, dc), F32)
    return pl.pallas_call(
        functools.partial(_mix_prompt_kernel, tiles_per_seq=seq // tm),
        out_shape=(jax.ShapeDtypeStruct((m, dm), F32), jax.ShapeDtypeStruct((m, dm), BF16),
                   jax.ShapeDtypeStruct((m // tm, SUBLANES, dc), F32)),
        grid=(m // tm,),
        in_specs=[pl.BlockSpec((tm, dm), lambda i: (i, 0)),
                  pl.BlockSpec((N_GROUPS, tm, LANES), lambda i: (0, i, 0)),
                  plane(1), plane(2), plane(3), halo(1), halo(3),
                  pl.BlockSpec((CONV_W, dc), fixed), pl.BlockSpec((1, D_ATTN), fixed),
                  pl.BlockSpec((1, dc), fixed), pl.BlockSpec((1, dm), fixed), _resident(w_out.shape, fixed)],
        out_specs=(pl.BlockSpec((tm, dm), lambda i: (i, 0)), pl.BlockSpec((tm, dm), lambda i: (i, 0)),
                   pl.BlockSpec((None, SUBLANES, dc), lambda i: (i, 0, 0))),
        compiler_params=_cparams(("arbitrary",), 2 * 9 * tile + _nbytes(w_out.shape, BF16)
                                 + 4 * _nbytes((min(tm, MIX_SUB_ROWS), dm), F32)),
        name="mix_out_prompt",
    )(x, attn, vhbc, vhbc, vhbc, vhbc, vhbc, conv_w, ga, gc, gf, w_out)


def _mix_out_sample(x, attn, vhbc, buf, conv_w, ga, gc, gf, w_out):
    m, dm = x.shape
    dc = conv_w.shape[1]
    full = lambda a: pl.BlockSpec(a.shape, lambda i: (0,) * a.ndim)

    def plane(p):
        return pl.BlockSpec((None, N_GROUPS, m, LANES), lambda i: (p, 0, 0, 0))

    return pl.pallas_call(
        _mix_sample_kernel,
        out_shape=(jax.ShapeDtypeStruct((m, dm), F32), jax.ShapeDtypeStruct((m, dm), BF16),
                   jax.ShapeDtypeStruct((m, dc), F32)),
        grid=(1,),
        in_specs=[full(x), full(attn), plane(1), plane(2), plane(3), full(buf), full(conv_w), full(ga),
                  full(gc), full(gf), full(w_out)],
        out_specs=(pl.BlockSpec((m, dm), lambda i: (0, 0)), pl.BlockSpec((m, dm), lambda i: (0, 0)),
                   pl.BlockSpec((m, dc), lambda i: (0, 0))),
        compiler_params=_cparams(("arbitrary",), 2 * (11 * _nbytes((m, dc), F32) + _nbytes(w_out.shape, BF16))),
        name="mix_out_sample",
    )(x, attn, vhbc, vhbc, vhbc, buf, conv_w, ga, gc, gf, w_out)


def _ffn_kernel(n_ref, wg_ref, wu_ref, wd_ref, o_ref, *wcopy_refs):
    @pl.when(pl.program_id(1) == 0)
    def _():
        o_ref[...] = jnp.zeros(o_ref.shape, F32)

    n = n_ref[...]
    acc = None
    for c in range(0, wg_ref.shape[1], FFN_SUB_COLS):
        cols = pl.ds(c, FFN_SUB_COLS)
        wg, wu, wd = wg_ref[:, cols].astype(BF16), wu_ref[:, cols].astype(BF16), wd_ref[cols, :].astype(BF16)
        if wcopy_refs:
            wcopy_refs[0][:, cols], wcopy_refs[1][:, cols], wcopy_refs[2][cols, :] = wg, wu, wd
        gate = jnp.dot(n, wg, preferred_element_type=F32)
        up = jnp.dot(n, wu, preferred_element_type=F32)
        act = (gate / (1.0 + jnp.exp(-gate)) * up).astype(BF16)
        part = jnp.dot(act, wd, preferred_element_type=F32)
        acc = part if acc is None else acc + part
    o_ref[...] += acc


def _ffn(n, wg, wu, wd, tm, tf, emit_bf16_weights=False):
    m, dm = n.shape
    dff = wg.shape[1]
    row = lambda i, f: (i, 0)
    w_specs = [pl.BlockSpec((dm, tf), lambda i, f: (0, f)), pl.BlockSpec((dm, tf), lambda i, f: (0, f)),
               pl.BlockSpec((tf, dm), lambda i, f: (f, 0))]
    out_shape = [jax.ShapeDtypeStruct((m, dm), F32)]
    out_specs = [pl.BlockSpec((tm, dm), row)]
    if emit_bf16_weights:
        assert m == tm, "weight copies are written once: one row tile"
        out_shape += [jax.ShapeDtypeStruct(w.shape, BF16) for w in (wg, wu, wd)]
        out_specs += w_specs
    out = pl.pallas_call(
        _ffn_kernel,
        out_shape=tuple(out_shape),
        grid=(m // tm, dff // tf),
        in_specs=[pl.BlockSpec((tm, dm), row)] + w_specs,
        out_specs=tuple(out_specs),
        compiler_params=_cparams(("arbitrary", "arbitrary"),
                                 2 * (_nbytes((tm, dm), F32) + _nbytes((tm, dm), BF16))
                                 + 2 * 3 * (_nbytes((dm, tf), wg.dtype)
                                            + (_nbytes((dm, tf), BF16) if emit_bf16_weights else 0))
                                 + 4 * _nbytes((tm, tf), F32)),
        name="ffn",
    )(n, wg, wu, wd)
    return out if emit_bf16_weights else out[0]


def _ple_kernel(h_ref, d_ref, p_ref, g_ref, wg_ref, wp_ref, o_ref):
    for rows in _sub_tiles(h_ref.shape[0], PLE_SUB_ROWS):
        x = h_ref[rows, :] + d_ref[rows, :]
        z = jnp.dot(_rms(x, g_ref[...]).astype(BF16), wg_ref[...], preferred_element_type=F32)
        e = jnp.dot(p_ref[rows, :].astype(BF16), wp_ref[...], preferred_element_type=F32)
        o_ref[rows, :] = x + e / (1.0 + jnp.exp(-z))


def _ple(h, d, p, g, wg, wp, tm):
    m, dm = h.shape
    dp = p.shape[1]
    row = lambda i: (i, 0)
    fixed = lambda i: (0, 0)
    return pl.pallas_call(
        _ple_kernel,
        out_shape=jax.ShapeDtypeStruct((m, dm), F32),
        grid=(m // tm,),
        in_specs=[pl.BlockSpec((tm, dm), row), pl.BlockSpec((tm, dm), row), pl.BlockSpec((tm, dp), row),
                  pl.BlockSpec((1, dm), fixed), _resident(wg.shape, fixed), _resident(wp.shape, fixed)],
        out_specs=pl.BlockSpec((tm, dm), row),
        compiler_params=_cparams(("arbitrary",),
                                 2 * (3 * _nbytes((tm, dm), F32) + _nbytes((tm, dp), F32))
                                 + _nbytes(wg.shape, BF16) + _nbytes(wp.shape, BF16)
                                 + 6 * _nbytes((min(tm, PLE_SUB_ROWS), dm), F32)),
        name="ple",
    )(h, d, p, g, wg, wp)


def _tiles(m):
    return dict(proj=min(m, 1024), mix=min(m, 512), ffn=min(m, 1024), ffn_cols=512, ple=min(m, 512))


def _window_rows(t, batch, kw):
    return jnp.transpose(t.reshape(batch, N_HEADS, HEAD_DIM, kw), (0, 3, 1, 2))


def kernel(x_prompt, x_sample, p_prompt, p_sample, cache_k, cache_v, state_conv, rel_bias, g_mix, w_in,
           q_norm, k_norm, conv_w, g_attn_out, g_conv_out, w_out, g_ffn, w_gate, w_up, w_down, g_ple,
           w_ple_gate, w_ple_proj):
    depth = g_mix.shape[0]
    batch, seq, dm = x_prompt.shape
    bd, dec_seq, _ = x_sample.shape
    dc = conv_w.shape[2]
    assert depth == 1 and dec_seq == 1, "single layer, one new position per sample"
    assert seq % SUPER == 0 and dm == D_ATTN + dc and dc == D_ATTN
    assert w_gate.shape[2] % 512 == 0

    bias_p, bias_s, bias_s_new = _bias_tables(rel_bias, cache_k.shape[2])
    gmat = jnp.asarray(np.kron(np.eye(PROJ_COLS // HEAD_DIM, dtype=np.float32),
                               np.full((HEAD_DIM, HEAD_DIM), 1.0 / HEAD_DIM, np.float32)), BF16)

    i = 0
    row2 = lambda a: a.reshape(1, -1).astype(F32)
    g_mix_i, g_ffn_i, g_ple_i = row2(g_mix[i]), row2(g_ffn[i]), row2(g_ple[i])
    qk_gains = jnp.stack([row2(jnp.tile(q_norm[i], N_HEADS) * SCALE), row2(jnp.tile(k_norm[i], N_HEADS))])
    ga, gc = row2(g_attn_out[i]), row2(g_conv_out[i])
    cw = conv_w[i].astype(F32)
    w_out_i = w_out[i].astype(BF16)
    wpg_i, wpp_i = w_ple_gate[i].astype(BF16), w_ple_proj[i].astype(BF16)

    mp = batch * seq
    tp, ts = _tiles(mp), _tiles(bd)
    kw = min(SUPER, seq)
    xp = x_prompt.reshape(mp, dm)
    xs = x_sample.reshape(bd, dm)
    qks, kts, w_qk, nrm_s = _proj(xs, w_in[i], 0, 2, 1, bd, bd, ts["proj"], g_mix_i, qk_gains, gmat,
                                  emit_bf16_weights=True)
    vhbcs, vts, w_vhbc = _proj(nrm_s, w_in[i], 2, 4, 0, bd, bd, ts["proj"], emit_bf16_weights=True)
    qk, kt, nrm = _proj(xp, w_qk, 0, 2, 1, seq, kw, tp["proj"], g_mix_i, qk_gains, gmat)
    vhbc, vt = _proj(nrm, w_vhbc, 0, 4, 0, seq, kw, tp["proj"])
    attn, attn_s = _attn(qk, vhbc, bias_p, batch, seq, qks, vhbcs, cache_k[i], cache_v[i], bias_s, bias_s_new)

    buf = state_conv[i].astype(F32)
    hs, ns, us = _mix_out_sample(xs, attn_s, vhbcs, buf.reshape(bd, (CONV_W - 1) * dc), cw, ga, gc, g_ffn_i,
                                 w_out_i)
    ds, wg_i, wu_i, wd_i = _ffn(ns, w_gate[i], w_up[i], w_down[i], ts["ffn"], ts["ffn_cols"],
                                emit_bf16_weights=True)
    hs = _ple(hs, ds, p_sample[i].reshape(bd, -1), g_ple_i, wpg_i, wpp_i, ts["ple"])

    h, n, u_tail = _mix_out_prompt(xp, attn, vhbc, cw, ga, gc, g_ffn_i, w_out_i, seq, tp["mix"])
    d = _ffn(n, wg_i, wu_i, wd_i, tp["ffn"], tp["ffn_cols"])
    h = _ple(h, d, p_prompt[i].reshape(mp, -1), g_ple_i, wpg_i, wpp_i, tp["ple"])
    y_prompt = h.reshape(batch, seq, dm)
    k_prompt = _window_rows(kt, batch, kw)[None]
    v_prompt = _window_rows(vt, batch, kw)[None]
    tiles_per_seq = seq // tp["mix"]
    conv_prompt = u_tail.reshape(batch, tiles_per_seq, SUBLANES, dc)[None, :, -1, SUBLANES - (CONV_W - 1):]

    y_sample = hs.reshape(bd, dec_seq, dm)
    k_sample = _window_rows(kts, 1, bd).reshape(1, bd, dec_seq, N_HEADS, HEAD_DIM)
    v_sample = _window_rows(vts, 1, bd).reshape(1, bd, dec_seq, N_HEADS, HEAD_DIM)
    conv_sample = jnp.concatenate([buf[:, 1:], us[:, None, :]], axis=1)[None]

    return (y_prompt, y_sample, k_prompt, v_prompt, conv_prompt, k_sample, v_sample, conv_sample)
```

```python
import functools
import math

import numpy as np
import jax
import jax.numpy as jnp
from jax import lax
from jax.experimental import pallas as pl
from jax.experimental.pallas import tpu as pltpu

HEAD_DIM = 64
N_HEADS = 16
D_ATTN = N_HEADS * HEAD_DIM
CONV_W = 3
DIL_STEPS = 128
DILATIONS = (1, 4, 16)
N_BUCKETS = 32
MAX_EXACT = N_BUCKETS // 2
MAX_DIST = DIL_STEPS * max(DILATIONS)
EPS = 1e-6
SCALE = HEAD_DIM ** -0.5

LANES = 128
SUBLANES = 8
BF16_ROWS = 16
HEADS_PER_GROUP = LANES // HEAD_DIM
N_GROUPS = N_HEADS // HEADS_PER_GROUP
SUPER = DIL_STEPS * max(DILATIONS)
N_RES = max(DILATIONS)
MASK_VALUE = -1e30
MASK_BUCKET = N_BUCKETS
TABLE_ROWS = -(-(N_BUCKETS + 1) // BF16_ROWS) * BF16_ROWS
TABLE_CHUNK = 2048
TABLE_UNROLL = 4
SAMPLE_POS_CHUNK = 512
ATTN_UNROLL = 16
PROJ_COLS = 512
PROJ_SPLIT = D_ATTN // PROJ_COLS
MIX_SUB_ROWS = 256
PLE_SUB_ROWS = 512
FFN_SUB_COLS = 256
VMEM_SLACK_BYTES = 8 * 1024 * 1024

F32 = jnp.float32
BF16 = jnp.bfloat16


def _cparams(sem, vmem_bytes):
    return pltpu.CompilerParams(dimension_semantics=sem,
                                vmem_limit_bytes=int(vmem_bytes + VMEM_SLACK_BYTES))


def _nbytes(shape, dtype):
    return int(np.prod(shape)) * jnp.dtype(dtype).itemsize


def _resident(shape, index_map):
    return pl.BlockSpec(shape, index_map, pipeline_mode=pl.Buffered(1))


def _t5_bucket(dist):
    n = np.asarray(dist, np.int32)
    nf = np.maximum(n, 1).astype(np.float32)
    large = MAX_EXACT + (np.log(nf / MAX_EXACT) / np.float32(math.log(MAX_DIST / MAX_EXACT))
                         * (N_BUCKETS - MAX_EXACT)).astype(np.int32)
    large = np.minimum(large, N_BUCKETS - 1)
    return np.where(n < MAX_EXACT, n, large).astype(np.int32)


def _stored_to_natural(d):
    runs = N_RES // d
    run = DIL_STEPS // runs
    j = np.arange(DIL_STEPS)
    return (j % run) * runs + j // run


def _prompt_bias_index():
    out = np.empty((len(DILATIONS), 2, DIL_STEPS, 2 * DIL_STEPS), np.int32)
    for di, d in enumerate(DILATIONS):
        nat = _stored_to_natural(d)
        qi = nat[:, None]
        kj = np.concatenate([nat, nat + DIL_STEPS])[None, :]
        steps = DIL_STEPS + qi - kj
        band = (steps >= 0) & (steps <= DIL_STEPS)
        bucket = _t5_bucket(d * np.clip(steps, 0, DIL_STEPS))
        out[di, 0] = np.where(band, bucket, MASK_BUCKET)
        out[di, 1] = np.where(band & (kj >= DIL_STEPS), bucket, MASK_BUCKET)
    return out


def _sample_bias_index(lb):
    back = lb - np.arange(lb)
    cached = np.stack([np.where((back % d == 0) & (back // d <= DIL_STEPS), _t5_bucket(back), MASK_BUCKET)
                       for d in DILATIONS])
    new = np.stack([_t5_bucket(d * np.zeros(1, np.int32)) for d in DILATIONS])
    return cached, new


def _bias_table_kernel(tab_ref, idx_ref, op_ref, os_ref):
    t = tab_ref[...]
    hi = t.astype(BF16)
    r1 = t - hi.astype(F32)
    mid = r1.astype(BF16)
    lo = (r1 - mid.astype(F32)).astype(BF16)
    rows = lax.broadcasted_iota(jnp.int32, (TABLE_ROWS, TABLE_CHUNK), 0)

    def lookup(o_ref, base):
        def chunk(c, carry):
            dst = pl.multiple_of(c * TABLE_CHUNK, TABLE_CHUNK)
            src = pl.ds(pl.multiple_of(base + dst, TABLE_CHUNK), TABLE_CHUNK)
            onehot = jnp.where(rows == idx_ref[:, src], 1.0, 0.0).astype(BF16)
            acc = jnp.dot(hi, onehot, preferred_element_type=F32)
            acc = acc + jnp.dot(mid, onehot, preferred_element_type=F32)
            acc = acc + jnp.dot(lo, onehot, preferred_element_type=F32)
            o_ref[:, pl.ds(dst, TABLE_CHUNK)] = acc
            return carry

        lax.fori_loop(0, o_ref.shape[1] // TABLE_CHUNK, chunk, 0, unroll=TABLE_UNROLL)

    lookup(op_ref, 0)
    lookup(os_ref, op_ref.shape[1])


def _bias_tables(rel_bias, lb):
    pidx = _prompt_bias_index()
    cached, new = _sample_bias_index(lb)
    n_p = pidx.size
    n_s = -(-(cached.size + LANES) // (TABLE_CHUNK * TABLE_UNROLL)) * TABLE_CHUNK * TABLE_UNROLL
    assert n_p % (TABLE_CHUNK * TABLE_UNROLL) == 0
    idx = np.full((1, n_p + n_s), MASK_BUCKET, np.int32)
    flat = np.concatenate([pidx.reshape(-1), cached.reshape(-1), new.reshape(-1)])
    idx[0, :flat.size] = flat
    tab = jnp.concatenate(
        [rel_bias.astype(F32).T,
         jnp.full((N_HEADS, 1), MASK_VALUE, F32),
         jnp.zeros((N_HEADS, TABLE_ROWS - N_BUCKETS - 1), F32)], axis=1)
    whole = lambda a: pl.BlockSpec(a, lambda i: (0, 0))
    prompt, samp = pl.pallas_call(
        _bias_table_kernel,
        out_shape=(jax.ShapeDtypeStruct((N_HEADS, n_p), F32), jax.ShapeDtypeStruct((N_HEADS, n_s), F32)),
        grid=(1,),
        in_specs=[whole((N_HEADS, TABLE_ROWS)), whole((1, n_p + n_s))],
        out_specs=(whole((N_HEADS, n_p)), whole((N_HEADS, n_s))),
        compiler_params=_cparams(("arbitrary",), 2 * (_nbytes((N_HEADS, n_p + n_s), F32)
                                                      + _nbytes((SUBLANES, n_p + n_s), jnp.int32))),
        name="bias_table",
    )(tab, jnp.asarray(idx))
    prompt = prompt.reshape(N_HEADS, len(DILATIONS), 2, DIL_STEPS, 2 * DIL_STEPS)
    return prompt, samp[:, :cached.size], samp[:, cached.size:cached.size + LANES]


def _rms(x, g):
    return x * lax.rsqrt(jnp.mean(x * x, axis=-1, keepdims=True) + EPS) * g


def _head_rms(p, g, gmat):
    ms = jnp.dot((p * p).astype(BF16), gmat, preferred_element_type=F32)
    return p * lax.rsqrt(ms + EPS) * g


def _cat_groups(ref, rows):
    return jnp.concatenate([ref[g, rows, :] for g in range(N_GROUPS)], axis=1)


def _sub_tiles(tm, sub):
    sub = min(tm, sub)
    return [pl.ds(r, sub) for r in range(0, tm, sub)]


def _proj_kernel(*refs, normed, emit_w, t_plane, tiles_per_seq, first_kept):
    i, j = pl.program_id(0), pl.program_id(1)
    if normed:
        x_ref, g_ref, w_ref, gain_ref, gmat_ref, o_ref, t_ref, *rest = refs
        n_ref = rest[-1]

        @pl.when(j == 0)
        def _():
            n_ref[...] = _rms(x_ref[...], g_ref[...]).astype(BF16)
    else:
        n_ref, w_ref, o_ref, t_ref, *rest = refs

    w = w_ref[...].astype(BF16)
    if emit_w:
        rest[0][...] = w
    y = jnp.dot(n_ref[...], w, preferred_element_type=F32)
    if normed:
        y = _head_rms(y, gain_ref[...], gmat_ref[...])
    for g in range(PROJ_COLS // LANES):
        o_ref[g] = y[:, g * LANES:(g + 1) * LANES]

    @pl.when(jnp.logical_and(j // PROJ_SPLIT == t_plane, i % tiles_per_seq >= first_kept))
    def _():
        t_ref[...] = y.T


def _proj(x, w_in, plane0, n_planes, t_plane, seq, kw, tm, g_mix=None, gains=None, gmat=None,
          emit_bf16_weights=False):
    m, dm = x.shape
    da = D_ATTN
    assert seq % tm == 0 and kw % tm == 0 and m % seq == 0
    assert not emit_bf16_weights or m == tm, "weight copies are written once: one row tile"
    tps, first_kept, batch = seq // tm, (seq - kw) // tm, m // seq
    normed = gains is not None
    grp = PROJ_COLS // LANES
    w_spec = pl.BlockSpec((dm, PROJ_COLS), lambda i, j: (0, plane0 * PROJ_SPLIT + j))
    if normed:
        in_specs = [pl.BlockSpec((tm, dm), lambda i, j: (i, 0)), pl.BlockSpec((1, dm), lambda i, j: (0, 0)), w_spec,
                    pl.BlockSpec((None, 1, PROJ_COLS), lambda i, j: (j, 0, 0)),
                    _resident((PROJ_COLS, PROJ_COLS), lambda i, j: (0, 0))]
        args = [x, g_mix, w_in, gains.reshape(n_planes * PROJ_SPLIT, 1, PROJ_COLS), gmat]
    else:
        in_specs = [pl.BlockSpec((tm, dm), lambda i, j: (i, 0)), w_spec]
        args = [x, w_in]

    def t_index(i, j):
        kept = i % tps >= first_kept
        half = jnp.where(kept, jnp.clip(j - t_plane * PROJ_SPLIT, 0, PROJ_SPLIT - 1), 0)
        return (i // tps, half, jnp.maximum(i % tps - first_kept, 0))

    out_specs = [pl.BlockSpec((None, grp, tm, LANES), lambda i, j: (j // PROJ_SPLIT, j % PROJ_SPLIT, i, 0)),
                 pl.BlockSpec((None, PROJ_COLS, tm), t_index)]
    out_shape = [jax.ShapeDtypeStruct((n_planes, N_GROUPS, m, LANES), F32),
                 jax.ShapeDtypeStruct((batch, da, kw), F32)]
    if emit_bf16_weights:
        out_specs.append(pl.BlockSpec((dm, PROJ_COLS), lambda i, j: (0, j)))
        out_shape.append(jax.ShapeDtypeStruct((dm, n_planes * da), BF16))
    if normed:
        out_specs.append(pl.BlockSpec((tm, dm), lambda i, j: (i, 0)))
        out_shape.append(jax.ShapeDtypeStruct((m, dm), BF16))
    vmem = (2 * (_nbytes((tm, dm), x.dtype) + _nbytes((dm, PROJ_COLS), w_in.dtype) + _nbytes((dm, PROJ_COLS), BF16)
                 + 2 * _nbytes((tm, PROJ_COLS), F32) + _nbytes((tm, dm), BF16))
            + 6 * _nbytes((tm, PROJ_COLS), F32))
    return pl.pallas_call(
        functools.partial(_proj_kernel, normed=normed, emit_w=emit_bf16_weights, t_plane=t_plane,
                          tiles_per_seq=tps, first_kept=first_kept),
        out_shape=tuple(out_shape),
        grid=(m // tm, n_planes * PROJ_SPLIT),
        in_specs=in_specs,
        out_specs=tuple(out_specs),
        compiler_params=_cparams(("arbitrary", "arbitrary"), vmem),
        name="in_proj_qk" if normed else "in_proj_vhbc",
    )(*args)


def _gather_rows(ref, starts, run):
    parts = [ref[pl.ds(s, run), :] for s in starts]
    return parts[0] if len(parts) == 1 else jnp.concatenate(parts, axis=0)


def _scatter_rows(ref, starts, run, val):
    for i, s in enumerate(starts):
        ref[pl.ds(s, run), :] = val[i * run:(i + 1) * run]


def _sample_scores(row, q_ref, kn_ref, kt_ref, bias_ref, bnew_ref):
    lb = kt_ref.shape[1]
    nd = len(DILATIONS)
    seg = (lax.broadcasted_iota(jnp.int32, (N_HEADS, D_ATTN), 1) // HEAD_DIM
           == lax.broadcasted_iota(jnp.int32, (N_HEADS, D_ATTN), 0))
    qe = jnp.where(seg, _cat_groups(q_ref, row), 0.0).astype(BF16)
    kn = _cat_groups(kn_ref, row).astype(BF16).astype(F32)
    s_new = jnp.sum(qe.astype(F32) * kn, axis=1, keepdims=True)
    sn = [s_new + bnew_ref[:, di:di + 1] for di in range(nd)]
    starts = range(0, lb, SAMPLE_POS_CHUNK)
    chunks = [pl.ds(c, SAMPLE_POS_CHUNK) for c in starts]
    sc = []
    for c, ch in zip(starts, chunks):
        s = jnp.dot(qe, kt_ref[:, ch].astype(BF16), preferred_element_type=F32)
        sc.append([s + bias_ref[:, pl.ds(di * lb + c, SAMPLE_POS_CHUNK)] for di in range(nd)])
    m = functools.reduce(jnp.maximum, [jnp.max(x, axis=1, keepdims=True) for row_ in sc for x in row_] + sn)
    pn = functools.reduce(jnp.add, [jnp.exp(x - m) for x in sn])
    ps = [functools.reduce(jnp.add, [jnp.exp(x - m) for x in sd]) for sd in sc]
    return seg, chunks, ps, pn


def _sample_output(row, state, vn_ref, vt_ref, o_ref):
    seg, chunks, ps, pn = state
    den = pn
    o = pn * _cat_groups(vn_ref, row)
    for ch, p in zip(chunks, ps):
        den = den + jnp.sum(p, axis=1, keepdims=True)
        o = o + lax.dot_general(p.astype(BF16), vt_ref[:, ch].astype(BF16),
                                (((1,), (1,)), ((), ())), preferred_element_type=F32)
    out = jnp.sum(jnp.where(seg, o / den, 0.0), axis=0, keepdims=True)
    for g in range(N_GROUPS):
        o_ref[g, row, :] = out[:, g * LANES:(g + 1) * LANES]


def _attn_kernel(q_ref, k_ref, v_ref, bias_ref, sq_ref, skn_ref, svn_ref, ck_hbm, cv_hbm, sbias_ref,
                 sbnew_ref, o_ref, so_ref, qp, kp, vp, acc, m_sc, l_sc, tmp, kbuf, vbuf, sem, *, samples_per_step):
    sb = pl.program_id(2)
    step = (pl.program_id(0) * pl.num_programs(1) + pl.program_id(1)) * pl.num_programs(2) + sb
    n_samples = ck_hbm.shape[0]
    slot = sb % 2
    pslot = 1 - slot
    kcur, vcur = kp.at[slot], vp.at[slot]
    kprev, vprev = kp.at[pslot], vp.at[pslot]

    def copies(g, slot):
        return (pltpu.make_async_copy(ck_hbm.at[g], kbuf.at[slot], sem.at[0, slot]),
                pltpu.make_async_copy(cv_hbm.at[g], vbuf.at[slot], sem.at[1, slot]))

    @pl.when(step == 0)
    def _():
        for c in copies(0, 0):
            c.start()

    def sample(t):
        g = step * samples_per_step + t
        slot = t % 2
        row = pl.ds(g, 1)

        @pl.when(g + 1 < n_samples)
        def _():
            for c in copies(g + 1, 1 - slot):
                c.start()

        for c in copies(g, slot):
            c.wait()
        state = _sample_scores(row, sq_ref, skn_ref, kbuf.at[slot], sbias_ref, sbnew_ref)
        _sample_output(row, state, svn_ref, vbuf.at[slot], so_ref)

    quarter = SUPER // 4
    for src, dst in ((q_ref, qp), (k_ref, kcur), (v_ref, vcur)):
        for c in range(4):
            tmp[pl.ds(c * quarter, quarter), :] = src[pl.ds(c, quarter, stride=4), :]
        for c in range(4):
            for b in range(4):
                dst[pl.ds((4 * b + c) * DIL_STEPS, DIL_STEPS), :] = tmp[pl.ds(c * quarter + b, DIL_STEPS, stride=4), :]

    @pl.when(sb == 0)
    def _():
        kprev[...] = jnp.zeros(kprev.shape, F32)
        vprev[...] = jnp.zeros(vprev.shape, F32)

    head0 = lax.broadcasted_iota(jnp.int32, (DIL_STEPS, LANES), 1) < HEAD_DIM

    def block(di, starts, run, prev_ref_k, prev_ref_v, prev_starts, first, mode, out_rows=None):
        qb = _gather_rows(qp, starts, run)
        kb = jnp.concatenate([_gather_rows(prev_ref_k, prev_starts, run),
                              _gather_rows(kcur, starts, run)], axis=0).astype(BF16)
        vb = jnp.concatenate([_gather_rows(prev_ref_v, prev_starts, run),
                              _gather_rows(vcur, starts, run)], axis=0).astype(BF16)
        ms, ls, os_ = [], [], []
        for h in range(HEADS_PER_GROUP):
            keep = head0 if h == 0 else jnp.logical_not(head0)
            qh = jnp.where(keep, qb, 0.0).astype(BF16)
            s = lax.dot_general(qh, kb, (((1,), (1,)), ((), ())), preferred_element_type=F32)
            s = s + bias_ref[h, di, first]
            mh = jnp.max(s, axis=1, keepdims=True)
            p = jnp.exp(s - mh)
            ls.append(jnp.sum(p, axis=1, keepdims=True))
            os_.append(jnp.dot(p.astype(BF16), vb, preferred_element_type=F32))
            ms.append(mh)
        m_c = jnp.where(head0, ms[0], ms[1])
        l_c = jnp.where(head0, ls[0], ls[1])
        o_c = jnp.where(head0, os_[0], os_[1])
        if mode == "init":
            _scatter_rows(m_sc, starts, run, m_c)
            _scatter_rows(l_sc, starts, run, l_c)
            _scatter_rows(acc, starts, run, o_c)
            return
        m_o = _gather_rows(m_sc, starts, run)
        m_n = jnp.maximum(m_o, m_c)
        a_o = jnp.exp(m_o - m_n)
        a_c = jnp.exp(m_c - m_n)
        l_n = _gather_rows(l_sc, starts, run) * a_o + l_c * a_c
        o_n = _gather_rows(acc, starts, run) * a_o + o_c * a_c
        if mode == "merge":
            _scatter_rows(m_sc, starts, run, m_n)
            _scatter_rows(l_sc, starts, run, l_n)
            _scatter_rows(acc, starts, run, o_n)
        else:
            o_ref[out_rows, :] = o_n / l_n

    def run_dilation(di, mode):
        d = DILATIONS[di]
        runs = N_RES // d
        run = DIL_STEPS // runs
        nblk = SUPER // d // DIL_STEPS

        def body(it, carry):
            c = it // nblk
            n = it % nblk
            starts = [pl.multiple_of((d * b + c) * DIL_STEPS + run * n, SUBLANES) for b in range(runs)]
            pn = jnp.where(n > 0, n - 1, nblk - 1)
            prev_starts = [pl.multiple_of((d * b + c) * DIL_STEPS + run * pn, SUBLANES)
                           for b in range(runs)]
            pslot_n = jnp.where(n > 0, slot, pslot)
            first = jnp.logical_and(sb == 0, n == 0).astype(jnp.int32)
            out_rows = pl.ds(it, DIL_STEPS, stride=N_RES) if mode == "final" else None
            block(di, starts, run, kp.at[pslot_n], vp.at[pslot_n], prev_starts, first, mode, out_rows)
            return carry

        lax.fori_loop(0, d * nblk, body, 0, unroll=ATTN_UNROLL)

    phases = ("init", "merge", "final")
    for t in range(max(samples_per_step, len(phases))):
        if t < samples_per_step:
            sample(t)
        if t < len(phases):
            run_dilation(t, phases[t])


def _attn(qk, vhbc, bias, batch, seq, sqk, svhbc, cache_k, cache_v, sbias, sbias_new):
    nsb = seq // SUPER
    m = batch * seq
    bd, lb = cache_k.shape[0], cache_k.shape[1]
    da = D_ATTN
    n_steps = N_GROUPS * batch * nsb
    assert bd % (2 * n_steps) == 0 and lb % SAMPLE_POS_CHUNK == 0
    assert N_RES == 16, "the residue-major copy is written as two stride-4 passes"

    def feature_major(c):
        return jnp.transpose(c, (0, 2, 3, 1)).reshape(bd, da, lb)

    def plane(p):
        return pl.BlockSpec((None, None, SUPER, LANES), lambda g, b, s: (p, g, b * nsb + s, 0))

    def splane(p):
        return pl.BlockSpec((None, N_GROUPS, bd, LANES), lambda g, b, s: (p, 0, 0, 0))

    fixed2 = lambda g, b, s: (0, 0)
    bias_spec = pl.BlockSpec((HEADS_PER_GROUP,) + bias.shape[1:], lambda g, b, s: (g, 0, 0, 0, 0),
                             pipeline_mode=pl.Buffered(1))
    hbm = pl.BlockSpec(memory_space=pl.ANY)
    blk_bytes = _nbytes((SUPER, LANES), F32)
    return pl.pallas_call(
        functools.partial(_attn_kernel, samples_per_step=bd // n_steps),
        out_shape=(jax.ShapeDtypeStruct((N_GROUPS, m, LANES), F32),
                   jax.ShapeDtypeStruct((N_GROUPS, bd, LANES), F32)),
        grid=(N_GROUPS, batch, nsb),
        in_specs=[plane(0), plane(1), plane(0), bias_spec,
                  splane(0), splane(1), splane(0), hbm, hbm,
                  pl.BlockSpec(sbias.shape, fixed2), pl.BlockSpec(sbias_new.shape, fixed2)],
        out_specs=(pl.BlockSpec((None, SUPER, LANES), lambda g, b, s: (g, b * nsb + s, 0)),
                   pl.BlockSpec((N_GROUPS, bd, LANES), lambda g, b, s: (0, 0, 0))),
        scratch_shapes=[pltpu.VMEM((SUPER, LANES), F32),
                        pltpu.VMEM((2, SUPER, LANES), F32), pltpu.VMEM((2, SUPER, LANES), F32),
                        pltpu.VMEM((SUPER, LANES), F32), pltpu.VMEM((SUPER, LANES), F32),
                        pltpu.VMEM((SUPER, LANES), F32), pltpu.VMEM((SUPER, LANES), F32),
                        pltpu.VMEM((2, da, lb), F32), pltpu.VMEM((2, da, lb), F32),
                        pltpu.SemaphoreType.DMA((2, 2))],
        compiler_params=_cparams(("arbitrary", "arbitrary", "arbitrary"),
                                 (2 * 4 + 9) * blk_bytes
                                 + _nbytes((HEADS_PER_GROUP,) + bias.shape[1:], F32)
                                 + 4 * _nbytes((da, lb), F32) + 2 * 4 * _nbytes((bd, da), F32)
                                 + 2 * _nbytes(sbias.shape, F32)),
        name="attn",
    )(qk, qk, vhbc, bias, sqk, sqk, svhbc, feature_major(cache_k), feature_major(cache_v), sbias, sbias_new)


def _mix_tail(x, a, conv, ga_ref, gc_ref, w_ref):
    cat = jnp.concatenate([_rms(a, ga_ref[...]), _rms(conv, gc_ref[...])], axis=1).astype(BF16)
    return x + jnp.dot(cat, w_ref[...], preferred_element_type=F32)


def _mix_prompt_kernel(x_ref, a_ref, h_ref, b_ref, c_ref, hh_ref, ch_ref, cw_ref, ga_ref, gc_ref, gf_ref, w_ref,
                       o_ref, n_ref, ut_ref, *, tiles_per_seq):
    tm = x_ref.shape[0]
    seq_start = pl.program_id(0) % tiles_per_seq == 0
    all8 = pl.ds(0, SUBLANES)
    halo = jnp.where(seq_start, 0.0, _cat_groups(hh_ref, all8) * _cat_groups(ch_ref, all8))
    for rows in _sub_tiles(tm, MIX_SUB_ROWS):
        u = _cat_groups(h_ref, rows) * _cat_groups(c_ref, rows)
        rid = lax.broadcasted_iota(jnp.int32, u.shape, 0)
        u1 = jnp.where(rid == 0, halo[7:8], pltpu.roll(u, 1, axis=0))
        u2 = jnp.where(rid == 0, halo[6:7], jnp.where(rid == 1, halo[7:8], pltpu.roll(u, 2, axis=0)))
        cy = cw_ref[0:1] * u2 + cw_ref[1:2] * u1 + cw_ref[2:3] * u
        conv = _cat_groups(b_ref, rows) * cy
        h1 = _mix_tail(x_ref[rows, :], _cat_groups(a_ref, rows), conv, ga_ref, gc_ref, w_ref)
        o_ref[rows, :] = h1
        n_ref[rows, :] = _rms(h1, gf_ref[...]).astype(BF16)
        halo = u[u.shape[0] - SUBLANES:]
    ut_ref[...] = halo


def _mix_sample_kernel(x_ref, a_ref, h_ref, b_ref, c_ref, buf_ref, cw_ref, ga_ref, gc_ref, gf_ref, w_ref,
                       o_ref, n_ref, u_ref):
    every = pl.ds(0, x_ref.shape[0])
    u = _cat_groups(h_ref, every) * _cat_groups(c_ref, every)
    dc = u.shape[1]
    cy = cw_ref[0:1] * buf_ref[:, :dc] + cw_ref[1:2] * buf_ref[:, dc:] + cw_ref[2:3] * u
    conv = _cat_groups(b_ref, every) * cy
    h1 = _mix_tail(x_ref[...], _cat_groups(a_ref, every), conv, ga_ref, gc_ref, w_ref)
    o_ref[...] = h1
    n_ref[...] = _rms(h1, gf_ref[...]).astype(BF16)
    u_ref[...] = u


def _mix_out_prompt(x, attn, vhbc, conv_w, ga, gc, gf, w_out, seq, tm):
    m, dm = x.shape
    dc = conv_w.shape[1]
    fixed = lambda i: (0, 0)
    halo_blocks = tm // SUBLANES

    def plane(p):
        return pl.BlockSpec((None, N_GROUPS, tm, LANES), lambda i: (p, 0, i, 0))

    def halo(p):
        return pl.BlockSpec((None, N_GROUPS, SUBLANES, LANES),
                            lambda i: (p, 0, jnp.maximum(i * halo_blocks - 1, 0), 0))

    tile = _nbytes((tm, dc), F32)
    return pl.pallas_call(
        functools.partial(_mix_prompt_kernel, tiles_per_seq=seq // tm),
        out_shape=(jax.ShapeDtypeStruct((m, dm), F32), jax.ShapeDtypeStruct((m, dm), BF16),
                   jax.ShapeDtypeStruct((m // tm, SUBLANES, dc), F32)),
        grid=(m // tm,),
        in_specs=[pl.BlockSpec((tm, dm), lambda i: (i, 0)),
                  pl.BlockSpec((N_GROUPS, tm, LANES), lambda i: (0, i, 0)),
                  plane(1), plane(2), plane(3), halo(1), halo(3),
                  pl.BlockSpec((CONV_W, dc), fixed), pl.BlockSpec((1, D_ATTN), fixed),
                  pl.BlockSpec((1, dc), fixed), pl.BlockSpec((1, dm), fixed), _resident(w_out.shape, fixed)],
        out_specs=(pl.BlockSpec((tm, dm), lambda i: (i, 0)), pl.BlockSpec((tm, dm), lambda i: (i, 0)),
                   pl.BlockSpec((None, SUBLANES, dc), lambda i: (i, 0, 0))),
        compiler_params=_cparams(("arbitrary",), 2 * 9 * tile + _nbytes(w_out.shape, BF16)
                                 + 4 * _nbytes((min(tm, MIX_SUB_ROWS), dm), F32)),
        name="mix_out_prompt",
    )(x, attn, vhbc, vhbc, vhbc, vhbc, vhbc, conv_w, ga, gc, gf, w_out)


def _mix_out_sample(x, attn, vhbc, buf, conv_w, ga, gc, gf, w_out):
    m, dm = x.shape
    dc = conv_w.shape[1]
    full = lambda a: pl.BlockSpec(a.shape, lambda i: (0,) * a.ndim)

    def plane(p):
        return pl.BlockSpec((None, N_GROUPS, m, LANES), lambda i: (p, 0, 0, 0))

    return pl.pallas_call(
        _mix_sample_kernel,
        out_shape=(jax.ShapeDtypeStruct((m, dm), F32), jax.ShapeDtypeStruct((m, dm), BF16),
                   jax.ShapeDtypeStruct((m, dc), F32)),
        grid=(1,),
        in_specs=[full(x), full(attn), plane(1), plane(2), plane(3), full(buf), full(conv_w), full(ga),
                  full(gc), full(gf), full(w_out)],
        out_specs=(pl.BlockSpec((m, dm), lambda i: (0, 0)), pl.BlockSpec((m, dm), lambda i: (0, 0)),
                   pl.BlockSpec((m, dc), lambda i: (0, 0))),
        compiler_params=_cparams(("arbitrary",), 2 * (11 * _nbytes((m, dc), F32) + _nbytes(w_out.shape, BF16))),
        name="mix_out_sample",
    )(x, attn, vhbc, vhbc, vhbc, buf, conv_w, ga, gc, gf, w_out)


def _ffn_kernel(n_ref, wg_ref, wu_ref, wd_ref, o_ref, *wcopy_refs):
    @pl.when(pl.program_id(1) == 0)
    def _():
        o_ref[...] = jnp.zeros(o_ref.shape, F32)

    n = n_ref[...]
    acc = None
    for c in range(0, wg_ref.shape[1], FFN_SUB_COLS):
        cols = pl.ds(c, FFN_SUB_COLS)
        wg, wu, wd = wg_ref[:, cols].astype(BF16), wu_ref[:, cols].astype(BF16), wd_ref[cols, :].astype(BF16)
        if wcopy_refs:
            wcopy_refs[0][:, cols], wcopy_refs[1][:, cols], wcopy_refs[2][cols, :] = wg, wu, wd
        gate = jnp.dot(n, wg, preferred_element_type=F32)
        up = jnp.dot(n, wu, preferred_element_type=F32)
        act = (gate / (1.0 + jnp.exp(-gate)) * up).astype(BF16)
        part = jnp.dot(act, wd, preferred_element_type=F32)
        acc = part if acc is None else acc + part
    o_ref[...] += acc


def _ffn(n, wg, wu, wd, tm, tf, emit_bf16_weights=False):
    m, dm = n.shape
    dff = wg.shape[1]
    row = lambda i, f: (i, 0)
    w_specs = [pl.BlockSpec((dm, tf), lambda i, f: (0, f)), pl.BlockSpec((dm, tf), lambda i, f: (0, f)),
               pl.BlockSpec((tf, dm), lambda i, f: (f, 0))]
    out_shape = [jax.ShapeDtypeStruct((m, dm), F32)]
    out_specs = [pl.BlockSpec((tm, dm), row)]
    if emit_bf16_weights:
        assert m == tm, "weight copies are written once: one row tile"
        out_shape += [jax.ShapeDtypeStruct(w.shape, BF16) for w in (wg, wu, wd)]
        out_specs += w_specs
    out = pl.pallas_call(
        _ffn_kernel,
        out_shape=tuple(out_shape),
        grid=(m // tm, dff // tf),
        in_specs=[pl.BlockSpec((tm, dm), row)] + w_specs,
        out_specs=tuple(out_specs),
        compiler_params=_cparams(("arbitrary", "arbitrary"),
                                 2 * (_nbytes((tm, dm), F32) + _nbytes((tm, dm), BF16))
                                 + 2 * 3 * (_nbytes((dm, tf), wg.dtype)
                                            + (_nbytes((dm, tf), BF16) if emit_bf16_weights else 0))
                                 + 4 * _nbytes((tm, tf), F32)),
        name="ffn",
    )(n, wg, wu, wd)
    return out if emit_bf16_weights else out[0]


def _ple_kernel(h_ref, d_ref, p_ref, g_ref, wg_ref, wp_ref, o_ref):
    for rows in _sub_tiles(h_ref.shape[0], PLE_SUB_ROWS):
        x = h_ref[rows, :] + d_ref[rows, :]
        z = jnp.dot(_rms(x, g_ref[...]).astype(BF16), wg_ref[...], preferred_element_type=F32)
        e = jnp.dot(p_ref[rows, :].astype(BF16), wp_ref[...], preferred_element_type=F32)
        o_ref[rows, :] = x + e / (1.0 + jnp.exp(-z))


def _ple(h, d, p, g, wg, wp, tm):
    m, dm = h.shape
    dp = p.shape[1]
    row = lambda i: (i, 0)
    fixed = lambda i: (0, 0)
    return pl.pallas_call(
        _ple_kernel,
        out_shape=jax.ShapeDtypeStruct((m, dm), F32),
        grid=(m // tm,),
        in_specs=[pl.BlockSpec((tm, dm), row), pl.BlockSpec((tm, dm), row), pl.BlockSpec((tm, dp), row),
                  pl.BlockSpec((1, dm), fixed), _resident(wg.shape, fixed), _resident(wp.shape, fixed)],
        out_specs=pl.BlockSpec((tm, dm), row),
        compiler_params=_cparams(("arbitrary",),
                                 2 * (3 * _nbytes((tm, dm), F32) + _nbytes((tm, dp), F32))
                                 + _nbytes(wg.shape, BF16) + _nbytes(wp.shape, BF16)
                                 + 6 * _nbytes((min(tm, PLE_SUB_ROWS), dm), F32)),
        name="ple",
    )(h, d, p, g, wg, wp)


def _tiles(m):
    return dict(proj=min(m, 1024), mix=min(m, 512), ffn=min(m, 1024), ffn_cols=512, ple=min(m, 512))


def _window_rows(t, batch, kw):
    return jnp.transpose(t.reshape(batch, N_HEADS, HEAD_DIM, kw), (0, 3, 1, 2))


def kernel(x_prompt, x_sample, p_prompt, p_sample, cache_k, cache_v, state_conv, rel_bias, g_mix, w_in,
           q_norm, k_norm, conv_w, g_attn_out, g_conv_out, w_out, g_ffn, w_gate, w_up, w_down, g_ple,
           w_ple_gate, w_ple_proj):
    depth = g_mix.shape[0]
    batch, seq, dm = x_prompt.shape
    bd, dec_seq, _ = x_sample.shape
    dc = conv_w.shape[2]
    assert depth == 1 and dec_seq == 1, "single layer, one new position per sample"
    assert seq % SUPER == 0 and dm == D_ATTN + dc and dc == D_ATTN
    assert w_gate.shape[2] % 512 == 0

    bias_p, bias_s, bias_s_new = _bias_tables(rel_bias, cache_k.shape[2])
    gmat = jnp.asarray(np.kron(np.eye(PROJ_COLS // HEAD_DIM, dtype=np.float32),
                               np.full((HEAD_DIM, HEAD_DIM), 1.0 / HEAD_DIM, np.float32)), BF16)

    i = 0
    row2 = lambda a: a.reshape(1, -1).astype(F32)
    g_mix_i, g_ffn_i, g_ple_i = row2(g_mix[i]), row2(g_ffn[i]), row2(g_ple[i])
    qk_gains = jnp.stack([row2(jnp.tile(q_norm[i], N_HEADS) * SCALE), row2(jnp.tile(k_norm[i], N_HEADS))])
    ga, gc = row2(g_attn_out[i]), row2(g_conv_out[i])
    cw = conv_w[i].astype(F32)
    w_out_i = w_out[i].astype(BF16)
    wpg_i, wpp_i = w_ple_gate[i].astype(BF16), w_ple_proj[i].astype(BF16)

    mp = batch * seq
    tp, ts = _tiles(mp), _tiles(bd)
    kw = min(SUPER, seq)
    xp = x_prompt.reshape(mp, dm)
    xs = x_sample.reshape(bd, dm)
    qks, kts, w_qk, nrm_s = _proj(xs, w_in[i], 0, 2, 1, bd, bd, ts["proj"], g_mix_i, qk_gains, gmat,
                                  emit_bf16_weights=True)
    vhbcs, vts, w_vhbc = _proj(nrm_s, w_in[i], 2, 4, 0, bd, bd, ts["proj"], emit_bf16_weights=True)
    qk, kt, nrm = _proj(xp, w_qk, 0, 2, 1, seq, kw, tp["proj"], g_mix_i, qk_gains, gmat)
    vhbc, vt = _proj(nrm, w_vhbc, 0, 4, 0, seq, kw, tp["proj"])
    attn, attn_s = _attn(qk, vhbc, bias_p, batch, seq, qks, vhbcs, cache_k[i], cache_v[i], bias_s, bias_s_new)

    buf = state_conv[i].astype(F32)
    hs, ns, us = _mix_out_sample(xs, attn_s, vhbcs, buf.reshape(bd, (CONV_W - 1) * dc), cw, ga, gc, g_ffn_i,
                                 w_out_i)
    ds, wg_i, wu_i, wd_i = _ffn(ns, w_gate[i], w_up[i], w_down[i], ts["ffn"], ts["ffn_cols"],
                                emit_bf16_weights=True)
    hs = _ple(hs, ds, p_sample[i].reshape(bd, -1), g_ple_i, wpg_i, wpp_i, ts["ple"])

    h, n, u_tail = _mix_out_prompt(xp, attn, vhbc, cw, ga, gc, g_ffn_i, w_out_i, seq, tp["mix"])
    d = _ffn(n, wg_i, wu_i, wd_i, tp["ffn"], tp["ffn_cols"])
    h = _ple(h, d, p_prompt[i].reshape(mp, -1), g_ple_i, wpg_i, wpp_i, tp["ple"])
    y_prompt = h.reshape(batch, seq, dm)
    k_prompt = _window_rows(kt, batch, kw)[None]
    v_prompt = _window_rows(vt, batch, kw)[None]
    tiles_per_seq = seq // tp["mix"]
    conv_prompt = u_tail.reshape(batch, tiles_per_seq, SUBLANES, dc)[None, :, -1, SUBLANES - (CONV_W - 1):]

    y_sample = hs.reshape(bd, dec_seq, dm)
    k_sample = _window_rows(kts, 1, bd).reshape(1, bd, dec_seq, N_HEADS, HEAD_DIM)
    v_sample = _window_rows(vts, 1, bd).reshape(1, bd, dec_seq, N_HEADS, HEAD_DIM)
    conv_sample = jnp.concatenate([buf[:, 1:], us[:, None, :]], axis=1)[None]

    return (y_prompt, y_sample, k_prompt, v_prompt, conv_prompt, k_sample, v_sample, conv_sample)
```

```python
import functools
import math

import numpy as np
import jax
import jax.numpy as jnp
from jax import lax
from jax.experimental import pallas as pl
from jax.experimental.pallas import tpu as pltpu

HEAD_DIM = 64
N_HEADS = 16
D_ATTN = N_HEADS * HEAD_DIM
CONV_W = 3
DIL_STEPS = 128
DILATIONS = (1, 4, 16)
N_BUCKETS = 32
MAX_EXACT = N_BUCKETS // 2
MAX_DIST = DIL_STEPS * max(DILATIONS)
EPS = 1e-6
SCALE = HEAD_DIM ** -0.5

LANES = 128
SUBLANES = 8
BF16_ROWS = 16
HEADS_PER_GROUP = LANES // HEAD_DIM
N_GROUPS = N_HEADS // HEADS_PER_GROUP
SUPER = DIL_STEPS * max(DILATIONS)
N_RES = max(DILATIONS)
MASK_VALUE = -1e30
MASK_BUCKET = N_BUCKETS
TABLE_ROWS = -(-(N_BUCKETS + 1) // BF16_ROWS) * BF16_ROWS
TABLE_CHUNK = 2048
TABLE_UNROLL = 4
SAMPLE_POS_CHUNK = 512
ATTN_UNROLL = 16
PROJ_COLS = 512
PROJ_SPLIT = D_ATTN // PROJ_COLS
MIX_SUB_ROWS = 256
PLE_SUB_ROWS = 512
FFN_SUB_COLS = 256
VMEM_SLACK_BYTES = 8 * 1024 * 1024

F32 = jnp.float32
BF16 = jnp.bfloat16


def _cparams(sem, vmem_bytes):
    return pltpu.CompilerParams(dimension_semantics=sem,
                                vmem_limit_bytes=int(vmem_bytes + VMEM_SLACK_BYTES))


def _nbytes(shape, dtype):
    return int(np.prod(shape)) * jnp.dtype(dtype).itemsize


def _resident(shape, index_map):
    return pl.BlockSpec(shape, index_map, pipeline_mode=pl.Buffered(1))


def _t5_bucket(dist):
    n = np.asarray(dist, np.int32)
    nf = np.maximum(n, 1).astype(np.float32)
    large = MAX_EXACT + (np.log(nf / MAX_EXACT) / np.float32(math.log(MAX_DIST / MAX_EXACT))
                         * (N_BUCKETS - MAX_EXACT)).astype(np.int32)
    large = np.minimum(large, N_BUCKETS - 1)
    return np.where(n < MAX_EXACT, n, large).astype(np.int32)


def _stored_to_natural(d):
    runs = N_RES // d
    run = DIL_STEPS // runs
    j = np.arange(DIL_STEPS)
    return (j % run) * runs + j // run


def _prompt_bias_index():
    out = np.empty((len(DILATIONS), 2, DIL_STEPS, 2 * DIL_STEPS), np.int32)
    for di, d in enumerate(DILATIONS):
        nat = _stored_to_natural(d)
        qi = nat[:, None]
        kj = np.concatenate([nat, nat + DIL_STEPS])[None, :]
        steps = DIL_STEPS + qi - kj
        band = (steps >= 0) & (steps <= DIL_STEPS)
        bucket = _t5_bucket(d * np.clip(steps, 0, DIL_STEPS))
        out[di, 0] = np.where(band, bucket, MASK_BUCKET)
        out[di, 1] = np.where(band & (kj >= DIL_STEPS), bucket, MASK_BUCKET)
    return out


def _sample_bias_index(lb):
    back = lb - np.arange(lb)
    cached = np.stack([np.where((back % d == 0) & (back // d <= DIL_STEPS), _t5_bucket(back), MASK_BUCKET)
                       for d in DILATIONS])
    new = np.stack([_t5_bucket(d * np.zeros(1, np.int32)) for d in DILATIONS])
    return cached, new


def _bias_table_kernel(tab_ref, idx_ref, op_ref, os_ref):
    t = tab_ref[...]
    hi = t.astype(BF16)
    r1 = t - hi.astype(F32)
    mid = r1.astype(BF16)
    lo = (r1 - mid.astype(F32)).astype(BF16)
    rows = lax.broadcasted_iota(jnp.int32, (TABLE_ROWS, TABLE_CHUNK), 0)

    def lookup(o_ref, base):
        def chunk(c, carry):
            dst = pl.multiple_of(c * TABLE_CHUNK, TABLE_CHUNK)
            src = pl.ds(pl.multiple_of(base + dst, TABLE_CHUNK), TABLE_CHUNK)
            onehot = jnp.where(rows == idx_ref[:, src], 1.0, 0.0).astype(BF16)
            acc = jnp.dot(hi, onehot, preferred_element_type=F32)
            acc = acc + jnp.dot(mid, onehot, preferred_element_type=F32)
            acc = acc + jnp.dot(lo, onehot, preferred_element_type=F32)
            o_ref[:, pl.ds(dst, TABLE_CHUNK)] = acc
            return carry

        lax.fori_loop(0, o_ref.shape[1] // TABLE_CHUNK, chunk, 0, unroll=TABLE_UNROLL)

    lookup(op_ref, 0)
    lookup(os_ref, op_ref.shape[1])


def _bias_tables(rel_bias, lb):
    pidx = _prompt_bias_index()
    cached, new = _sample_bias_index(lb)
    n_p = pidx.size
    n_s = -(-(cached.size + LANES) // (TABLE_CHUNK * TABLE_UNROLL)) * TABLE_CHUNK * TABLE_UNROLL
    assert n_p % (TABLE_CHUNK * TABLE_UNROLL) == 0
    idx = np.full((1, n_p + n_s), MASK_BUCKET, np.int32)
    flat = np.concatenate([pidx.reshape(-1), cached.reshape(-1), new.reshape(-1)])
    idx[0, :flat.size] = flat
    tab = jnp.concatenate(
        [rel_bias.astype(F32).T,
         jnp.full((N_HEADS, 1), MASK_VALUE, F32),
         jnp.zeros((N_HEADS, TABLE_ROWS - N_BUCKETS - 1), F32)], axis=1)
    whole = lambda a: pl.BlockSpec(a, lambda i: (0, 0))
    prompt, samp = pl.pallas_call(
        _bias_table_kernel,
        out_shape=(jax.ShapeDtypeStruct((N_HEADS, n_p), F32), jax.ShapeDtypeStruct((N_HEADS, n_s), F32)),
        grid=(1,),
        in_specs=[whole((N_HEADS, TABLE_ROWS)), whole((1, n_p + n_s))],
        out_specs=(whole((N_HEADS, n_p)), whole((N_HEADS, n_s))),
        compiler_params=_cparams(("arbitrary",), 2 * (_nbytes((N_HEADS, n_p + n_s), F32)
                                                      + _nbytes((SUBLANES, n_p + n_s), jnp.int32))),
        name="bias_table",
    )(tab, jnp.asarray(idx))
    prompt = prompt.reshape(N_HEADS, len(DILATIONS), 2, DIL_STEPS, 2 * DIL_STEPS)
    return prompt, samp[:, :cached.size], samp[:, cached.size:cached.size + LANES]


def _rms(x, g):
    return x * lax.rsqrt(jnp.mean(x * x, axis=-1, keepdims=True) + EPS) * g


def _head_rms(p, g, gmat):
    sq = (p * p).astype(BF16)
    ms = jnp.concatenate([jnp.dot(sq[:, c:c + LANES], gmat, preferred_element_type=F32)
                          for c in range(0, p.shape[1], LANES)], axis=1)
    return p * lax.rsqrt(ms + EPS) * g


def _cat_groups(ref, rows):
    return jnp.concatenate([ref[g, rows, :] for g in range(N_GROUPS)], axis=1)


def _sub_tiles(tm, sub):
    sub = min(tm, sub)
    return [pl.ds(r, sub) for r in range(0, tm, sub)]


def _proj_kernel(*refs, normed, emit_w, t_plane, tiles_per_seq, first_kept):
    i, j = pl.program_id(0), pl.program_id(1)
    if normed:
        x_ref, g_ref, w_ref, gain_ref, gmat_ref, o_ref, t_ref, *rest = refs
        n_ref = rest[-1]

        @pl.when(j == 0)
        def _():
            n_ref[...] = _rms(x_ref[...], g_ref[...]).astype(BF16)
    else:
        n_ref, w_ref, o_ref, t_ref, *rest = refs

    w = w_ref[...].astype(BF16)
    if emit_w:
        rest[0][...] = w
    y = jnp.dot(n_ref[...], w, preferred_element_type=F32)
    if normed:
        y = _head_rms(y, gain_ref[...], gmat_ref[...])
    for g in range(PROJ_COLS // LANES):
        o_ref[g] = y[:, g * LANES:(g + 1) * LANES]

    @pl.when(jnp.logical_and(j // PROJ_SPLIT == t_plane, i % tiles_per_seq >= first_kept))
    def _():
        t_ref[...] = y.T


def _proj(x, w_in, plane0, n_planes, t_plane, seq, kw, tm, g_mix=None, gains=None, gmat=None,
          emit_bf16_weights=False):
    m, dm = x.shape
    da = D_ATTN
    assert seq % tm == 0 and kw % tm == 0 and m % seq == 0
    assert not emit_bf16_weights or m == tm, "weight copies are written once: one row tile"
    tps, first_kept, batch = seq // tm, (seq - kw) // tm, m // seq
    normed = gains is not None
    grp = PROJ_COLS // LANES
    w_spec = pl.BlockSpec((dm, PROJ_COLS), lambda i, j: (0, plane0 * PROJ_SPLIT + j))
    if normed:
        in_specs = [pl.BlockSpec((tm, dm), lambda i, j: (i, 0)), pl.BlockSpec((1, dm), lambda i, j: (0, 0)), w_spec,
                    pl.BlockSpec((None, 1, PROJ_COLS), lambda i, j: (j, 0, 0)),
                    _resident((LANES, LANES), lambda i, j: (0, 0))]
        args = [x, g_mix, w_in, gains.reshape(n_planes * PROJ_SPLIT, 1, PROJ_COLS), gmat]
    else:
        in_specs = [pl.BlockSpec((tm, dm), lambda i, j: (i, 0)), w_spec]
        args = [x, w_in]

    def t_index(i, j):
        kept = i % tps >= first_kept
        half = jnp.where(kept, jnp.clip(j - t_plane * PROJ_SPLIT, 0, PROJ_SPLIT - 1), 0)
        return (i // tps, half, jnp.maximum(i % tps - first_kept, 0))

    out_specs = [pl.BlockSpec((None, grp, tm, LANES), lambda i, j: (j // PROJ_SPLIT, j % PROJ_SPLIT, i, 0)),
                 pl.BlockSpec((None, PROJ_COLS, tm), t_index)]
    out_shape = [jax.ShapeDtypeStruct((n_planes, N_GROUPS, m, LANES), F32),
                 jax.ShapeDtypeStruct((batch, da, kw), F32)]
    if emit_bf16_weights:
        out_specs.append(pl.BlockSpec((dm, PROJ_COLS), lambda i, j: (0, j)))
        out_shape.append(jax.ShapeDtypeStruct((dm, n_planes * da), BF16))
    if normed:
        out_specs.append(pl.BlockSpec((tm, dm), lambda i, j: (i, 0)))
        out_shape.append(jax.ShapeDtypeStruct((m, dm), BF16))
    vmem = (2 * (_nbytes((tm, dm), x.dtype) + _nbytes((dm, PROJ_COLS), w_in.dtype) + _nbytes((dm, PROJ_COLS), BF16)
                 + 2 * _nbytes((tm, PROJ_COLS), F32) + _nbytes((tm, dm), BF16))
            + 6 * _nbytes((tm, PROJ_COLS), F32))
    return pl.pallas_call(
        functools.partial(_proj_kernel, normed=normed, emit_w=emit_bf16_weights, t_plane=t_plane,
                          tiles_per_seq=tps, first_kept=first_kept),
        out_shape=tuple(out_shape),
        grid=(m // tm, n_planes * PROJ_SPLIT),
        in_specs=in_specs,
        out_specs=tuple(out_specs),
        compiler_params=_cparams(("arbitrary", "arbitrary"), vmem),
        name="in_proj_qk" if normed else "in_proj_vhbc",
    )(*args)


def _gather_rows(ref, starts, run):
    parts = [ref[pl.ds(s, run), :] for s in starts]
    return parts[0] if len(parts) == 1 else jnp.concatenate(parts, axis=0)


def _scatter_rows(ref, starts, run, val):
    for i, s in enumerate(starts):
        ref[pl.ds(s, run), :] = val[i * run:(i + 1) * run]


def _sample_scores(row, q_ref, kn_ref, kt_ref, bias_ref, bnew_ref):
    lb = kt_ref.shape[1]
    nd = len(DILATIONS)
    seg = (lax.broadcasted_iota(jnp.int32, (N_HEADS, D_ATTN), 1) // HEAD_DIM
           == lax.broadcasted_iota(jnp.int32, (N_HEADS, D_ATTN), 0))
    qe = jnp.where(seg, _cat_groups(q_ref, row), 0.0).astype(BF16)
    kn = _cat_groups(kn_ref, row).astype(BF16).astype(F32)
    s_new = jnp.sum(qe.astype(F32) * kn, axis=1, keepdims=True)
    sn = [s_new + bnew_ref[:, di:di + 1] for di in range(nd)]
    starts = range(0, lb, SAMPLE_POS_CHUNK)
    chunks = [pl.ds(c, SAMPLE_POS_CHUNK) for c in starts]
    sc = []
    for c, ch in zip(starts, chunks):
        s = jnp.dot(qe, kt_ref[:, ch].astype(BF16), preferred_element_type=F32)
        sc.append([s + bias_ref[:, pl.ds(di * lb + c, SAMPLE_POS_CHUNK)] for di in range(nd)])
    m = functools.reduce(jnp.maximum, [jnp.max(x, axis=1, keepdims=True) for row_ in sc for x in row_] + sn)
    pn = functools.reduce(jnp.add, [jnp.exp(x - m) for x in sn])
    ps = [functools.reduce(jnp.add, [jnp.exp(x - m) for x in sd]) for sd in sc]
    return seg, chunks, ps, pn


def _sample_output(row, state, vn_ref, vt_ref, o_ref):
    seg, chunks, ps, pn = state
    den = pn
    o = pn * _cat_groups(vn_ref, row)
    for ch, p in zip(chunks, ps):
        den = den + jnp.sum(p, axis=1, keepdims=True)
        o = o + lax.dot_general(p.astype(BF16), vt_ref[:, ch].astype(BF16),
                                (((1,), (1,)), ((), ())), preferred_element_type=F32)
    out = jnp.sum(jnp.where(seg, o / den, 0.0), axis=0, keepdims=True)
    for g in range(N_GROUPS):
        o_ref[g, row, :] = out[:, g * LANES:(g + 1) * LANES]


def _attn_kernel(q_ref, k_ref, v_ref, bias_ref, sq_ref, skn_ref, svn_ref, ck_hbm, cv_hbm, sbias_ref,
                 sbnew_ref, o_ref, so_ref, qp, kp, vp, acc, m_sc, l_sc, tmp, kbuf, vbuf, sem, *, samples_per_step):
    sb = pl.program_id(2)
    step = (pl.program_id(0) * pl.num_programs(1) + pl.program_id(1)) * pl.num_programs(2) + sb
    n_samples = ck_hbm.shape[0]
    slot = sb % 2
    pslot = 1 - slot
    kcur, vcur = kp.at[slot], vp.at[slot]
    kprev, vprev = kp.at[pslot], vp.at[pslot]

    def copies(g, slot):
        return (pltpu.make_async_copy(ck_hbm.at[g], kbuf.at[slot], sem.at[0, slot]),
                pltpu.make_async_copy(cv_hbm.at[g], vbuf.at[slot], sem.at[1, slot]))

    @pl.when(step == 0)
    def _():
        for c in copies(0, 0):
            c.start()

    def sample(t):
        g = step * samples_per_step + t
        slot = t % 2
        row = pl.ds(g, 1)

        @pl.when(g + 1 < n_samples)
        def _():
            for c in copies(g + 1, 1 - slot):
                c.start()

        for c in copies(g, slot):
            c.wait()
        state = _sample_scores(row, sq_ref, skn_ref, kbuf.at[slot], sbias_ref, sbnew_ref)
        _sample_output(row, state, svn_ref, vbuf.at[slot], so_ref)

    quarter = SUPER // 4
    for src, dst in ((q_ref, qp), (k_ref, kcur), (v_ref, vcur)):
        for c in range(4):
            tmp[pl.ds(c * quarter, quarter), :] = src[pl.ds(c, quarter, stride=4), :]
        for c in range(4):
            for b in range(4):
                dst[pl.ds((4 * b + c) * DIL_STEPS, DIL_STEPS), :] = tmp[pl.ds(c * quarter + b, DIL_STEPS, stride=4), :]

    @pl.when(sb == 0)
    def _():
        kprev[...] = jnp.zeros(kprev.shape, F32)
        vprev[...] = jnp.zeros(vprev.shape, F32)

    head0 = lax.broadcasted_iota(jnp.int32, (DIL_STEPS, LANES), 1) < HEAD_DIM

    def block(di, starts, run, prev_ref_k, prev_ref_v, prev_starts, first, mode, out_rows=None):
        qb = _gather_rows(qp, starts, run)
        kb = jnp.concatenate([_gather_rows(prev_ref_k, prev_starts, run),
                              _gather_rows(kcur, starts, run)], axis=0).astype(BF16)
        vb = jnp.concatenate([_gather_rows(prev_ref_v, prev_starts, run),
                              _gather_rows(vcur, starts, run)], axis=0).astype(BF16)
        ms, ls, os_ = [], [], []
        for h in range(HEADS_PER_GROUP):
            keep = head0 if h == 0 else jnp.logical_not(head0)
            qh = jnp.where(keep, qb, 0.0).astype(BF16)
            s = lax.dot_general(qh, kb, (((1,), (1,)), ((), ())), preferred_element_type=F32)
            s = s + bias_ref[h, di, first]
            mh = jnp.max(s, axis=1, keepdims=True)
            p = jnp.exp(s - mh)
            ls.append(jnp.sum(p, axis=1, keepdims=True))
            os_.append(jnp.dot(p.astype(BF16), vb, preferred_element_type=F32))
            ms.append(mh)
        m_c = jnp.where(head0, ms[0], ms[1])
        l_c = jnp.where(head0, ls[0], ls[1])
        o_c = jnp.where(head0, os_[0], os_[1])
        if mode == "init":
            _scatter_rows(m_sc, starts, run, m_c)
            _scatter_rows(l_sc, starts, run, l_c)
            _scatter_rows(acc, starts, run, o_c)
            return
        m_o = _gather_rows(m_sc, starts, run)
        m_n = jnp.maximum(m_o, m_c)
        a_o = jnp.exp(m_o - m_n)
        a_c = jnp.exp(m_c - m_n)
        l_n = _gather_rows(l_sc, starts, run) * a_o + l_c * a_c
        o_n = _gather_rows(acc, starts, run) * a_o + o_c * a_c
        if mode == "merge":
            _scatter_rows(m_sc, starts, run, m_n)
            _scatter_rows(l_sc, starts, run, l_n)
            _scatter_rows(acc, starts, run, o_n)
        else:
            o_ref[out_rows, :] = o_n / l_n

    def run_dilation(di, mode):
        d = DILATIONS[di]
        runs = N_RES // d
        run = DIL_STEPS // runs
        nblk = SUPER // d // DIL_STEPS

        def body(it, carry):
            c = it // nblk
            n = it % nblk
            starts = [pl.multiple_of((d * b + c) * DIL_STEPS + run * n, SUBLANES) for b in range(runs)]
            pn = jnp.where(n > 0, n - 1, nblk - 1)
            prev_starts = [pl.multiple_of((d * b + c) * DIL_STEPS + run * pn, SUBLANES)
                           for b in range(runs)]
            pslot_n = jnp.where(n > 0, slot, pslot)
            first = jnp.logical_and(sb == 0, n == 0).astype(jnp.int32)
            out_rows = pl.ds(it, DIL_STEPS, stride=N_RES) if mode == "final" else None
            block(di, starts, run, kp.at[pslot_n], vp.at[pslot_n], prev_starts, first, mode, out_rows)
            return carry

        lax.fori_loop(0, d * nblk, body, 0, unroll=ATTN_UNROLL)

    phases = ("init", "merge", "final")
    for t in range(max(samples_per_step, len(phases))):
        if t < samples_per_step:
            sample(t)
        if t < len(phases):
            run_dilation(t, phases[t])


def _attn(qk, vhbc, bias, batch, seq, sqk, svhbc, cache_k, cache_v, sbias, sbias_new):
    nsb = seq // SUPER
    m = batch * seq
    bd, lb = cache_k.shape[0], cache_k.shape[1]
    da = D_ATTN
    n_steps = N_GROUPS * batch * nsb
    assert bd % (2 * n_steps) == 0 and lb % SAMPLE_POS_CHUNK == 0
    assert N_RES == 16, "the residue-major copy is written as two stride-4 passes"

    def feature_major(c):
        return jnp.transpose(c, (0, 2, 3, 1)).reshape(bd, da, lb)

    def plane(p):
        return pl.BlockSpec((None, None, SUPER, LANES), lambda g, b, s: (p, g, b * nsb + s, 0))

    def splane(p):
        return pl.BlockSpec((None, N_GROUPS, bd, LANES), lambda g, b, s: (p, 0, 0, 0))

    fixed2 = lambda g, b, s: (0, 0)
    bias_spec = pl.BlockSpec((HEADS_PER_GROUP,) + bias.shape[1:], lambda g, b, s: (g, 0, 0, 0, 0),
                             pipeline_mode=pl.Buffered(1))
    hbm = pl.BlockSpec(memory_space=pl.ANY)
    blk_bytes = _nbytes((SUPER, LANES), F32)
    return pl.pallas_call(
        functools.partial(_attn_kernel, samples_per_step=bd // n_steps),
        out_shape=(jax.ShapeDtypeStruct((N_GROUPS, m, LANES), F32),
                   jax.ShapeDtypeStruct((N_GROUPS, bd, LANES), F32)),
        grid=(N_GROUPS, batch, nsb),
        in_specs=[plane(0), plane(1), plane(0), bias_spec,
                  splane(0), splane(1), splane(0), hbm, hbm,
                  pl.BlockSpec(sbias.shape, fixed2), pl.BlockSpec(sbias_new.shape, fixed2)],
        out_specs=(pl.BlockSpec((None, SUPER, LANES), lambda g, b, s: (g, b * nsb + s, 0)),
                   pl.BlockSpec((N_GROUPS, bd, LANES), lambda g, b, s: (0, 0, 0))),
        scratch_shapes=[pltpu.VMEM((SUPER, LANES), F32),
                        pltpu.VMEM((2, SUPER, LANES), F32), pltpu.VMEM((2, SUPER, LANES), F32),
                        pltpu.VMEM((SUPER, LANES), F32), pltpu.VMEM((SUPER, LANES), F32),
                        pltpu.VMEM((SUPER, LANES), F32), pltpu.VMEM((SUPER, LANES), F32),
                        pltpu.VMEM((2, da, lb), F32), pltpu.VMEM((2, da, lb), F32),
                        pltpu.SemaphoreType.DMA((2, 2))],
        compiler_params=_cparams(("arbitrary", "arbitrary", "arbitrary"),
                                 (2 * 4 + 9) * blk_bytes
                                 + _nbytes((HEADS_PER_GROUP,) + bias.shape[1:], F32)
                                 + 4 * _nbytes((da, lb), F32) + 2 * 4 * _nbytes((bd, da), F32)
                                 + 2 * _nbytes(sbias.shape, F32)),
        name="attn",
    )(qk, qk, vhbc, bias, sqk, sqk, svhbc, feature_major(cache_k), feature_major(cache_v), sbias, sbias_new)


def _mix_tail(x, a, conv, ga_ref, gc_ref, w_ref):
    cat = jnp.concatenate([_rms(a, ga_ref[...]), _rms(conv, gc_ref[...])], axis=1).astype(BF16)
    return x + jnp.dot(cat, w_ref[...], preferred_element_type=F32)


def _mix_prompt_kernel(x_ref, a_ref, h_ref, b_ref, c_ref, hh_ref, ch_ref, cw_ref, ga_ref, gc_ref, gf_ref, w_ref,
                       o_ref, n_ref, ut_ref, *, tiles_per_seq):
    tm = x_ref.shape[0]
    seq_start = pl.program_id(0) % tiles_per_seq == 0
    all8 = pl.ds(0, SUBLANES)
    halo = jnp.where(seq_start, 0.0, _cat_groups(hh_ref, all8) * _cat_groups(ch_ref, all8))
    for rows in _sub_tiles(tm, MIX_SUB_ROWS):
        u = _cat_groups(h_ref, rows) * _cat_groups(c_ref, rows)
        rid = lax.broadcasted_iota(jnp.int32, u.shape, 0)
        u1 = jnp.where(rid == 0, halo[7:8], pltpu.roll(u, 1, axis=0))
        u2 = jnp.where(rid == 0, halo[6:7], jnp.where(rid == 1, halo[7:8], pltpu.roll(u, 2, axis=0)))
        cy = cw_ref[0:1] * u2 + cw_ref[1:2] * u1 + cw_ref[2:3] * u
        conv = _cat_groups(b_ref, rows) * cy
        h1 = _mix_tail(x_ref[rows, :], _cat_groups(a_ref, rows), conv, ga_ref, gc_ref, w_ref)
        o_ref[rows, :] = h1
        n_ref[rows, :] = _rms(h1, gf_ref[...]).astype(BF16)
        halo = u[u.shape[0] - SUBLANES:]
    ut_ref[...] = halo


def _mix_sample_kernel(x_ref, a_ref, h_ref, b_ref, c_ref, buf_ref, cw_ref, ga_ref, gc_ref, gf_ref, w_ref,
                       o_ref, n_ref, u_ref):
    every = pl.ds(0, x_ref.shape[0])
    u = _cat_groups(h_ref, every) * _cat_groups(c_ref, every)
    dc = u.shape[1]
    cy = cw_ref[0:1] * buf_ref[:, :dc] + cw_ref[1:2] * buf_ref[:, dc:] + cw_ref[2:3] * u
    conv = _cat_groups(b_ref, every) * cy
    h1 = _mix_tail(x_ref[...], _cat_groups(a_ref, every), conv, ga_ref, gc_ref, w_ref)
    o_ref[...] = h1
    n_ref[...] = _rms(h1, gf_ref[...]).astype(BF16)
    u_ref[...] = u


def _mix_out_prompt(x, attn, vhbc, conv_w, ga, gc, gf, w_out, seq, tm):
    m, dm = x.shape
    dc = conv_w.shape[1]
    fixed = lambda i: (0, 0)
    halo_blocks = tm // SUBLANES

    def plane(p):
        return pl.BlockSpec((None, N_GROUPS, tm, LANES), lambda i: (p, 0, i, 0))

    def halo(p):
        return pl.BlockSpec((None, N_GROUPS, SUBLANES, LANES),
                            lambda i: (p, 0, jnp.maximum(i * halo_blocks - 1, 0), 0))

    tile = _nbytes((tm, dc), F32)
    return pl.pallas_call(
        functools.partial(_mix_prompt_kernel, tiles_per_seq=seq // tm),
        out_shape=(jax.ShapeDtypeStruct((m, dm), F32), jax.ShapeDtypeStruct((m, dm), BF16),
                   jax.ShapeDtypeStruct((m // tm, SUBLANES, dc), F32)),
        grid=(m // tm,),
        in_specs=[pl.BlockSpec((tm, dm), lambda i: (i, 0)),
                  pl.BlockSpec((N_GROUPS, tm, LANES), lambda i: (0, i, 0)),
                  plane(1), plane(2), plane(3), halo(1), halo(3),
                  pl.BlockSpec((CONV_W, dc), fixed), pl.BlockSpec((1, D_ATTN), fixed),
                  pl.BlockSpec((1, dc), fixed), pl.BlockSpec((1, dm), fixed), _resident(w_out.shape, fixed)],
        out_specs=(pl.BlockSpec((tm, dm), lambda i: (i, 0)), pl.BlockSpec((tm, dm), lambda i: (i, 0)),
                   pl.BlockSpec((None, SUBLANES, dc), lambda i: (i, 0, 0))),
        compiler_params=_cparams(("arbitrary",), 2 * 9 * tile + _nbytes(w_out.shape, BF16)
                                 + 4 * _nbytes((min(tm, MIX_SUB_ROWS), dm), F32)),
        name="mix_out_prompt",
    )(x, attn, vhbc, vhbc, vhbc, vhbc, vhbc, conv_w, ga, gc, gf, w_out)


def _mix_out_sample(x, attn, vhbc, buf, conv_w, ga, gc, gf, w_out):
    m, dm = x.shape
    dc = conv_w.shape[1]
    full = lambda a: pl.BlockSpec(a.shape, lambda i: (0,) * a.ndim)

    def plane(p):
        return pl.BlockSpec((None, N_GROUPS, m, LANES), lambda i: (p, 0, 0, 0))

    return pl.pallas_call(
        _mix_sample_kernel,
        out_shape=(jax.ShapeDtypeStruct((m, dm), F32), jax.ShapeDtypeStruct((m, dm), BF16),
                   jax.ShapeDtypeStruct((m, dc), F32)),
        grid=(1,),
        in_specs=[full(x), full(attn), plane(1), plane(2), plane(3), full(buf), full(conv_w), full(ga),
                  full(gc), full(gf), full(w_out)],
        out_specs=(pl.BlockSpec((m, dm), lambda i: (0, 0)), pl.BlockSpec((m, dm), lambda i: (0, 0)),
                   pl.BlockSpec((m, dc), lambda i: (0, 0))),
        compiler_params=_cparams(("arbitrary",), 2 * (11 * _nbytes((m, dc), F32) + _nbytes(w_out.shape, BF16))),
        name="mix_out_sample",
    )(x, attn, vhbc, vhbc, vhbc, buf, conv_w, ga, gc, gf, w_out)


def _ffn_kernel(n_ref, wg_ref, wu_ref, wd_ref, o_ref, *wcopy_refs):
    @pl.when(pl.program_id(1) == 0)
    def _():
        o_ref[...] = jnp.zeros(o_ref.shape, F32)

    n = n_ref[...]
    acc = None
    for c in range(0, wg_ref.shape[1], FFN_SUB_COLS):
        cols = pl.ds(c, FFN_SUB_COLS)
        wg, wu, wd = wg_ref[:, cols].astype(BF16), wu_ref[:, cols].astype(BF16), wd_ref[cols, :].astype(BF16)
        if wcopy_refs:
            wcopy_refs[0][:, cols], wcopy_refs[1][:, cols], wcopy_refs[2][cols, :] = wg, wu, wd
        gate = jnp.dot(n, wg, preferred_element_type=F32)
        up = jnp.dot(n, wu, preferred_element_type=F32)
        act = (gate / (1.0 + jnp.exp(-gate)) * up).astype(BF16)
        part = jnp.dot(act, wd, preferred_element_type=F32)
        acc = part if acc is None else acc + part
    o_ref[...] += acc


def _ffn(n, wg, wu, wd, tm, tf, emit_bf16_weights=False):
    m, dm = n.shape
    dff = wg.shape[1]
    row = lambda i, f: (i, 0)
    w_specs = [pl.BlockSpec((dm, tf), lambda i, f: (0, f)), pl.BlockSpec((dm, tf), lambda i, f: (0, f)),
               pl.BlockSpec((tf, dm), lambda i, f: (f, 0))]
    out_shape = [jax.ShapeDtypeStruct((m, dm), F32)]
    out_specs = [pl.BlockSpec((tm, dm), row)]
    if emit_bf16_weights:
        assert m == tm, "weight copies are written once: one row tile"
        out_shape += [jax.ShapeDtypeStruct(w.shape, BF16) for w in (wg, wu, wd)]
        out_specs += w_specs
    out = pl.pallas_call(
        _ffn_kernel,
        out_shape=tuple(out_shape),
        grid=(m // tm, dff // tf),
        in_specs=[pl.BlockSpec((tm, dm), row)] + w_specs,
        out_specs=tuple(out_specs),
        compiler_params=_cparams(("arbitrary", "arbitrary"),
                                 2 * (_nbytes((tm, dm), F32) + _nbytes((tm, dm), BF16))
                                 + 2 * 3 * (_nbytes((dm, tf), wg.dtype)
                                            + (_nbytes((dm, tf), BF16) if emit_bf16_weights else 0))
                                 + 4 * _nbytes((tm, tf), F32)),
        name="ffn",
    )(n, wg, wu, wd)
    return out if emit_bf16_weights else out[0]


def _ple_kernel(h_ref, d_ref, p_ref, g_ref, wg_ref, wp_ref, o_ref):
    for rows in _sub_tiles(h_ref.shape[0], PLE_SUB_ROWS):
        x = h_ref[rows, :] + d_ref[rows, :]
        z = jnp.dot(_rms(x, g_ref[...]).astype(BF16), wg_ref[...], preferred_element_type=F32)
        e = jnp.dot(p_ref[rows, :].astype(BF16), wp_ref[...], preferred_element_type=F32)
        o_ref[rows, :] = x + e / (1.0 + jnp.exp(-z))


def _ple(h, d, p, g, wg, wp, tm):
    m, dm = h.shape
    dp = p.shape[1]
    row = lambda i: (i, 0)
    fixed = lambda i: (0, 0)
    return pl.pallas_call(
        _ple_kernel,
        out_shape=jax.ShapeDtypeStruct((m, dm), F32),
        grid=(m // tm,),
        in_specs=[pl.BlockSpec((tm, dm), row), pl.BlockSpec((tm, dm), row), pl.BlockSpec((tm, dp), row),
                  pl.BlockSpec((1, dm), fixed), _resident(wg.shape, fixed), _resident(wp.shape, fixed)],
        out_specs=pl.BlockSpec((tm, dm), row),
        compiler_params=_cparams(("arbitrary",),
                                 2 * (3 * _nbytes((tm, dm), F32) + _nbytes((tm, dp), F32))
                                 + _nbytes(wg.shape, BF16) + _nbytes(wp.shape, BF16)
                                 + 6 * _nbytes((min(tm, PLE_SUB_ROWS), dm), F32)),
        name="ple",
    )(h, d, p, g, wg, wp)


def _tiles(m):
    return dict(proj=min(m, 1024), proj_n=min(m, 2048), mix=min(m, 512), ffn=min(m, 1024), ffn_cols=512,
                ple=min(m, 512))


def _window_rows(t, batch, kw):
    return jnp.transpose(t.reshape(batch, N_HEADS, HEAD_DIM, kw), (0, 3, 1, 2))


def kernel(x_prompt, x_sample, p_prompt, p_sample, cache_k, cache_v, state_conv, rel_bias, g_mix, w_in,
           q_norm, k_norm, conv_w, g_attn_out, g_conv_out, w_out, g_ffn, w_gate, w_up, w_down, g_ple,
           w_ple_gate, w_ple_proj):
    depth = g_mix.shape[0]
    batch, seq, dm = x_prompt.shape
    bd, dec_seq, _ = x_sample.shape
    dc = conv_w.shape[2]
    assert depth == 1 and dec_seq == 1, "single layer, one new position per sample"
    assert seq % SUPER == 0 and dm == D_ATTN + dc and dc == D_ATTN
    assert w_gate.shape[2] % 512 == 0

    bias_p, bias_s, bias_s_new = _bias_tables(rel_bias, cache_k.shape[2])
    gmat = jnp.asarray(np.kron(np.eye(HEADS_PER_GROUP, dtype=np.float32),
                               np.full((HEAD_DIM, HEAD_DIM), 1.0 / HEAD_DIM, np.float32)), BF16)

    i = 0
    row2 = lambda a: a.reshape(1, -1).astype(F32)
    g_mix_i, g_ffn_i, g_ple_i = row2(g_mix[i]), row2(g_ffn[i]), row2(g_ple[i])
    qk_gains = jnp.stack([row2(jnp.tile(q_norm[i], N_HEADS) * SCALE), row2(jnp.tile(k_norm[i], N_HEADS))])
    ga, gc = row2(g_attn_out[i]), row2(g_conv_out[i])
    cw = conv_w[i].astype(F32)
    w_out_i = w_out[i].astype(BF16)
    wpg_i, wpp_i = w_ple_gate[i].astype(BF16), w_ple_proj[i].astype(BF16)

    mp = batch * seq
    tp, ts = _tiles(mp), _tiles(bd)
    kw = min(SUPER, seq)
    xp = x_prompt.reshape(mp, dm)
    xs = x_sample.reshape(bd, dm)
    qks, kts, w_qk, nrm_s = _proj(xs, w_in[i], 0, 2, 1, bd, bd, ts["proj"], g_mix_i, qk_gains, gmat,
                                  emit_bf16_weights=True)
    vhbcs, vts, w_vhbc = _proj(nrm_s, w_in[i], 2, 4, 0, bd, bd, ts["proj_n"], emit_bf16_weights=True)
    qk, kt, nrm = _proj(xp, w_qk, 0, 2, 1, seq, kw, tp["proj"], g_mix_i, qk_gains, gmat)
    vhbc, vt = _proj(nrm, w_vhbc, 0, 4, 0, seq, kw, tp["proj_n"])
    attn, attn_s = _attn(qk, vhbc, bias_p, batch, seq, qks, vhbcs, cache_k[i], cache_v[i], bias_s, bias_s_new)

    buf = state_conv[i].astype(F32)
    hs, ns, us = _mix_out_sample(xs, attn_s, vhbcs, buf.reshape(bd, (CONV_W - 1) * dc), cw, ga, gc, g_ffn_i,
                                 w_out_i)
    ds, wg_i, wu_i, wd_i = _ffn(ns, w_gate[i], w_up[i], w_down[i], ts["ffn"], ts["ffn_cols"],
                                emit_bf16_weights=True)
    hs = _ple(hs, ds, p_sample[i].reshape(bd, -1), g_ple_i, wpg_i, wpp_i, ts["ple"])

    h, n, u_tail = _mix_out_prompt(xp, attn, vhbc, cw, ga, gc, g_ffn_i, w_out_i, seq, tp["mix"])
    d = _ffn(n, wg_i, wu_i, wd_i, tp["ffn"], tp["ffn_cols"])
    h = _ple(h, d, p_prompt[i].reshape(mp, -1), g_ple_i, wpg_i, wpp_i, tp["ple"])
    y_prompt = h.reshape(batch, seq, dm)
    k_prompt = _window_rows(kt, batch, kw)[None]
    v_prompt = _window_rows(vt, batch, kw)[None]
    tiles_per_seq = seq // tp["mix"]
    conv_prompt = u_tail.reshape(batch, tiles_per_seq, SUBLANES, dc)[None, :, -1, SUBLANES - (CONV_W - 1):]

    y_sample = hs.reshape(bd, dec_seq, dm)
    k_sample = _window_rows(kts, 1, bd).reshape(1, bd, dec_seq, N_HEADS, HEAD_DIM)
    v_sample = _window_rows(vts, 1, bd).reshape(1, bd, dec_seq, N_HEADS, HEAD_DIM)
    conv_sample = jnp.concatenate([buf[:, 1:], us[:, None, :]], axis=1)[None]

    return (y_prompt, y_sample, k_prompt, v_prompt, conv_prompt, k_sample, v_sample, conv_sample)
```

```python
import functools
import math

import numpy as np
import jax
import jax.numpy as jnp
from jax import lax
from jax.experimental import pallas as pl
from jax.experimental.pallas import tpu as pltpu

HEAD_DIM = 64
N_HEADS = 16
D_ATTN = N_HEADS * HEAD_DIM
CONV_W = 3
DIL_STEPS = 128
DILATIONS = (1, 4, 16)
N_BUCKETS = 32
MAX_EXACT = N_BUCKETS // 2
MAX_DIST = DIL_STEPS * max(DILATIONS)
EPS = 1e-6
SCALE = HEAD_DIM ** -0.5

LANES = 128
SUBLANES = 8
BF16_ROWS = 16
HEADS_PER_GROUP = LANES // HEAD_DIM
N_GROUPS = N_HEADS // HEADS_PER_GROUP
SUPER = DIL_STEPS * max(DILATIONS)
N_RES = max(DILATIONS)
MASK_VALUE = -1e30
MASK_BUCKET = N_BUCKETS
TABLE_ROWS = -(-(N_BUCKETS + 1) // BF16_ROWS) * BF16_ROWS
TABLE_CHUNK = 2048
TABLE_UNROLL = 4
SAMPLE_POS_CHUNK = 512
ATTN_UNROLL = 16
PROJ_COLS = 512
PROJ_SPLIT = D_ATTN // PROJ_COLS
MIX_SUB_ROWS = 256
PLE_SUB_ROWS = 512
FFN_SUB_COLS = 256
VMEM_SLACK_BYTES = 8 * 1024 * 1024

F32 = jnp.float32
BF16 = jnp.bfloat16


def _cparams(sem, vmem_bytes):
    return pltpu.CompilerParams(dimension_semantics=sem,
                                vmem_limit_bytes=int(vmem_bytes + VMEM_SLACK_BYTES))


def _nbytes(shape, dtype):
    return int(np.prod(shape)) * jnp.dtype(dtype).itemsize


def _resident(shape, index_map):
    return pl.BlockSpec(shape, index_map, pipeline_mode=pl.Buffered(1))


def _t5_bucket(dist):
    n = np.asarray(dist, np.int32)
    nf = np.maximum(n, 1).astype(np.float32)
    large = MAX_EXACT + (np.log(nf / MAX_EXACT) / np.float32(math.log(MAX_DIST / MAX_EXACT))
                         * (N_BUCKETS - MAX_EXACT)).astype(np.int32)
    large = np.minimum(large, N_BUCKETS - 1)
    return np.where(n < MAX_EXACT, n, large).astype(np.int32)


def _stored_to_natural(d):
    runs = N_RES // d
    run = DIL_STEPS // runs
    j = np.arange(DIL_STEPS)
    return (j % run) * runs + j // run


def _prompt_bias_index():
    out = np.empty((len(DILATIONS), 2, DIL_STEPS, 2 * DIL_STEPS), np.int32)
    for di, d in enumerate(DILATIONS):
        nat = _stored_to_natural(d)
        qi = nat[:, None]
        kj = np.concatenate([nat, nat + DIL_STEPS])[None, :]
        steps = DIL_STEPS + qi - kj
        band = (steps >= 0) & (steps <= DIL_STEPS)
        bucket = _t5_bucket(d * np.clip(steps, 0, DIL_STEPS))
        out[di, 0] = np.where(band, bucket, MASK_BUCKET)
        out[di, 1] = np.where(band & (kj >= DIL_STEPS), bucket, MASK_BUCKET)
    return out


def _sample_bias_index(lb):
    back = lb - np.arange(lb)
    cached = np.stack([np.where((back % d == 0) & (back // d <= DIL_STEPS), _t5_bucket(back), MASK_BUCKET)
                       for d in DILATIONS])
    new = np.stack([_t5_bucket(d * np.zeros(1, np.int32)) for d in DILATIONS])
    return cached, new


def _bias_table_kernel(tab_ref, idx_ref, op_ref, os_ref):
    t = tab_ref[...]
    hi = t.astype(BF16)
    r1 = t - hi.astype(F32)
    mid = r1.astype(BF16)
    lo = (r1 - mid.astype(F32)).astype(BF16)
    rows = lax.broadcasted_iota(jnp.int32, (TABLE_ROWS, TABLE_CHUNK), 0)

    def lookup(o_ref, base):
        def chunk(c, carry):
            dst = pl.multiple_of(c * TABLE_CHUNK, TABLE_CHUNK)
            src = pl.ds(pl.multiple_of(base + dst, TABLE_CHUNK), TABLE_CHUNK)
            onehot = jnp.where(rows == idx_ref[:, src], 1.0, 0.0).astype(BF16)
            acc = jnp.dot(hi, onehot, preferred_element_type=F32)
            acc = acc + jnp.dot(mid, onehot, preferred_element_type=F32)
            acc = acc + jnp.dot(lo, onehot, preferred_element_type=F32)
            o_ref[:, pl.ds(dst, TABLE_CHUNK)] = acc
            return carry

        lax.fori_loop(0, o_ref.shape[1] // TABLE_CHUNK, chunk, 0, unroll=TABLE_UNROLL)

    lookup(op_ref, 0)
    lookup(os_ref, op_ref.shape[1])


def _bias_tables(rel_bias, lb):
    pidx = _prompt_bias_index()
    cached, new = _sample_bias_index(lb)
    n_p = pidx.size
    n_s = -(-(cached.size + LANES) // (TABLE_CHUNK * TABLE_UNROLL)) * TABLE_CHUNK * TABLE_UNROLL
    assert n_p % (TABLE_CHUNK * TABLE_UNROLL) == 0
    idx = np.full((1, n_p + n_s), MASK_BUCKET, np.int32)
    flat = np.concatenate([pidx.reshape(-1), cached.reshape(-1), new.reshape(-1)])
    idx[0, :flat.size] = flat
    tab = jnp.concatenate(
        [rel_bias.astype(F32).T,
         jnp.full((N_HEADS, 1), MASK_VALUE, F32),
         jnp.zeros((N_HEADS, TABLE_ROWS - N_BUCKETS - 1), F32)], axis=1)
    whole = lambda a: pl.BlockSpec(a, lambda i: (0, 0))
    prompt, samp = pl.pallas_call(
        _bias_table_kernel,
        out_shape=(jax.ShapeDtypeStruct((N_HEADS, n_p), F32), jax.ShapeDtypeStruct((N_HEADS, n_s), F32)),
        grid=(1,),
        in_specs=[whole((N_HEADS, TABLE_ROWS)), whole((1, n_p + n_s))],
        out_specs=(whole((N_HEADS, n_p)), whole((N_HEADS, n_s))),
        compiler_params=_cparams(("arbitrary",), 2 * (_nbytes((N_HEADS, n_p + n_s), F32)
                                                      + _nbytes((SUBLANES, n_p + n_s), jnp.int32))),
        name="bias_table",
    )(tab, jnp.asarray(idx))
    prompt = prompt.reshape(N_HEADS, len(DILATIONS), 2, DIL_STEPS, 2 * DIL_STEPS)
    return prompt, samp[:, :cached.size], samp[:, cached.size:cached.size + LANES]


def _rms(x, g):
    return x * lax.rsqrt(jnp.mean(x * x, axis=-1, keepdims=True) + EPS) * g


def _head_rms(p, g, gmat):
    sq = (p * p).astype(BF16)
    ms = jnp.concatenate([jnp.dot(sq[:, c:c + LANES], gmat, preferred_element_type=F32)
                          for c in range(0, p.shape[1], LANES)], axis=1)
    return p * lax.rsqrt(ms + EPS) * g


def _cat_groups(ref, rows):
    return jnp.concatenate([ref[g, rows, :] for g in range(N_GROUPS)], axis=1)


def _sub_tiles(tm, sub):
    sub = min(tm, sub)
    return [pl.ds(r, sub) for r in range(0, tm, sub)]


def _proj_kernel(*refs, normed, emit_w, n_cast, t_plane, tiles_per_seq, first_kept):
    i, j = pl.program_id(0), pl.program_id(1)
    if normed:
        x_ref, g_ref, w_ref, gain_ref, gmat_ref, o_ref, t_ref, *rest = refs
        n_ref = rest[-1]

        @pl.when(j == 0)
        def _():
            n_ref[...] = _rms(x_ref[...], g_ref[...]).astype(BF16)
    else:
        n_ref, w_ref, *rest = refs
        cast_in, (o_ref, t_ref, *rest) = rest[:n_cast], rest[n_cast:]
        for src, dst in zip(cast_in, rest[len(rest) - n_cast:]):
            dst[...] = src[...].astype(BF16)

    w = w_ref[...].astype(BF16)
    if emit_w:
        rest[0][...] = w
    y = jnp.dot(n_ref[...], w, preferred_element_type=F32)
    if normed:
        y = _head_rms(y, gain_ref[...], gmat_ref[...])
    for g in range(PROJ_COLS // LANES):
        o_ref[g] = y[:, g * LANES:(g + 1) * LANES]

    @pl.when(jnp.logical_and(j // PROJ_SPLIT == t_plane, i % tiles_per_seq >= first_kept))
    def _():
        t_ref[...] = y.T


def _proj(x, w_in, plane0, n_planes, t_plane, seq, kw, tm, g_mix=None, gains=None, gmat=None,
          emit_bf16_weights=False, side_casts=()):
    m, dm = x.shape
    da = D_ATTN
    assert seq % tm == 0 and kw % tm == 0 and m % seq == 0
    assert not emit_bf16_weights or m == tm, "weight copies are written once: one row tile"
    tps, first_kept, batch = seq // tm, (seq - kw) // tm, m // seq
    normed = gains is not None
    assert not (normed and side_casts)
    grp = PROJ_COLS // LANES
    n_cols = n_planes * PROJ_SPLIT
    n_steps = (m // tm) * n_cols
    w_spec = pl.BlockSpec((dm, PROJ_COLS), lambda i, j: (0, plane0 * PROJ_SPLIT + j))
    if normed:
        in_specs = [pl.BlockSpec((tm, dm), lambda i, j: (i, 0)), pl.BlockSpec((1, dm), lambda i, j: (0, 0)), w_spec,
                    pl.BlockSpec((None, 1, PROJ_COLS), lambda i, j: (j, 0, 0)),
                    _resident((LANES, LANES), lambda i, j: (0, 0))]
        args = [x, g_mix, w_in, gains.reshape(n_planes * PROJ_SPLIT, 1, PROJ_COLS), gmat]
    else:
        in_specs = [pl.BlockSpec((tm, dm), lambda i, j: (i, 0)), w_spec]
        args = [x, w_in]
    slab_specs = []
    for a in side_casts:
        assert a.shape[0] % (n_steps * BF16_ROWS) == 0, "whole packed bf16 row tiles per step"
        slab_specs.append(pl.BlockSpec((a.shape[0] // n_steps, a.shape[1]), lambda i, j: (i * n_cols + j, 0)))
    in_specs += slab_specs
    args += list(side_casts)

    def t_index(i, j):
        kept = i % tps >= first_kept
        half = jnp.where(kept, jnp.clip(j - t_plane * PROJ_SPLIT, 0, PROJ_SPLIT - 1), 0)
        return (i // tps, half, jnp.maximum(i % tps - first_kept, 0))

    out_specs = [pl.BlockSpec((None, grp, tm, LANES), lambda i, j: (j // PROJ_SPLIT, j % PROJ_SPLIT, i, 0)),
                 pl.BlockSpec((None, PROJ_COLS, tm), t_index)]
    out_shape = [jax.ShapeDtypeStruct((n_planes, N_GROUPS, m, LANES), F32),
                 jax.ShapeDtypeStruct((batch, da, kw), F32)]
    if emit_bf16_weights:
        out_specs.append(pl.BlockSpec((dm, PROJ_COLS), lambda i, j: (0, j)))
        out_shape.append(jax.ShapeDtypeStruct((dm, n_planes * da), BF16))
    if normed:
        out_specs.append(pl.BlockSpec((tm, dm), lambda i, j: (i, 0)))
        out_shape.append(jax.ShapeDtypeStruct((m, dm), BF16))
    out_specs += slab_specs
    out_shape += [jax.ShapeDtypeStruct(a.shape, BF16) for a in side_casts]
    vmem = (2 * (_nbytes((tm, dm), x.dtype) + _nbytes((dm, PROJ_COLS), w_in.dtype) + _nbytes((dm, PROJ_COLS), BF16)
                 + 2 * _nbytes((tm, PROJ_COLS), F32) + _nbytes((tm, dm), BF16))
            + 6 * _nbytes((tm, PROJ_COLS), F32)
            + 2 * sum(_nbytes(a.shape, F32) + _nbytes(a.shape, BF16) for a in side_casts) // n_steps)
    return pl.pallas_call(
        functools.partial(_proj_kernel, normed=normed, emit_w=emit_bf16_weights, n_cast=len(side_casts),
                          t_plane=t_plane, tiles_per_seq=tps, first_kept=first_kept),
        out_shape=tuple(out_shape),
        grid=(m // tm, n_cols),
        in_specs=in_specs,
        out_specs=tuple(out_specs),
        compiler_params=_cparams(("arbitrary", "arbitrary"), vmem),
        name="in_proj_qk" if normed else "in_proj_vhbc",
    )(*args)


def _gather_rows(ref, starts, run):
    parts = [ref[pl.ds(s, run), :] for s in starts]
    return parts[0] if len(parts) == 1 else jnp.concatenate(parts, axis=0)


def _scatter_rows(ref, starts, run, val):
    for i, s in enumerate(starts):
        ref[pl.ds(s, run), :] = val[i * run:(i + 1) * run]


def _sample_scores(row, q_ref, kn_ref, kt_ref, bias_ref, bnew_ref):
    lb = kt_ref.shape[1]
    nd = len(DILATIONS)
    seg = (lax.broadcasted_iota(jnp.int32, (N_HEADS, D_ATTN), 1) // HEAD_DIM
           == lax.broadcasted_iota(jnp.int32, (N_HEADS, D_ATTN), 0))
    qe = jnp.where(seg, _cat_groups(q_ref, row), 0.0).astype(BF16)
    kn = _cat_groups(kn_ref, row).astype(BF16).astype(F32)
    s_new = jnp.sum(qe.astype(F32) * kn, axis=1, keepdims=True)
    sn = [s_new + bnew_ref[:, di:di + 1] for di in range(nd)]
    starts = range(0, lb, SAMPLE_POS_CHUNK)
    chunks = [pl.ds(c, SAMPLE_POS_CHUNK) for c in starts]
    sc = []
    for c, ch in zip(starts, chunks):
        s = jnp.dot(qe, kt_ref[:, ch].astype(BF16), preferred_element_type=F32)
        sc.append([s + bias_ref[:, pl.ds(di * lb + c, SAMPLE_POS_CHUNK)] for di in range(nd)])
    m = functools.reduce(jnp.maximum, [jnp.max(x, axis=1, keepdims=True) for row_ in sc for x in row_] + sn)
    pn = functools.reduce(jnp.add, [jnp.exp(x - m) for x in sn])
    ps = [functools.reduce(jnp.add, [jnp.exp(x - m) for x in sd]) for sd in sc]
    return seg, chunks, ps, pn


def _sample_output(row, state, vn_ref, vt_ref, o_ref):
    seg, chunks, ps, pn = state
    den = pn
    o = pn * _cat_groups(vn_ref, row)
    for ch, p in zip(chunks, ps):
        den = den + jnp.sum(p, axis=1, keepdims=True)
        o = o + lax.dot_general(p.astype(BF16), vt_ref[:, ch].astype(BF16),
                                (((1,), (1,)), ((), ())), preferred_element_type=F32)
    out = jnp.sum(jnp.where(seg, o / den, 0.0), axis=0, keepdims=True)
    for g in range(N_GROUPS):
        o_ref[g, row, :] = out[:, g * LANES:(g + 1) * LANES]


def _attn_kernel(q_ref, k_ref, v_ref, bias_ref, sq_ref, skn_ref, svn_ref, ck_hbm, cv_hbm, sbias_ref,
                 sbnew_ref, o_ref, so_ref, qp, kp, vp, acc, m_sc, l_sc, tmp, kbuf, vbuf, sem, *, samples_per_step):
    sb = pl.program_id(2)
    step = (pl.program_id(0) * pl.num_programs(1) + pl.program_id(1)) * pl.num_programs(2) + sb
    n_samples = ck_hbm.shape[0]
    slot = sb % 2
    pslot = 1 - slot
    kcur, vcur = kp.at[slot], vp.at[slot]
    kprev, vprev = kp.at[pslot], vp.at[pslot]

    def copies(g, slot):
        return (pltpu.make_async_copy(ck_hbm.at[g], kbuf.at[slot], sem.at[0, slot]),
                pltpu.make_async_copy(cv_hbm.at[g], vbuf.at[slot], sem.at[1, slot]))

    @pl.when(step == 0)
    def _():
        for c in copies(0, 0):
            c.start()

    def sample(t):
        g = step * samples_per_step + t
        slot = t % 2
        row = pl.ds(g, 1)

        @pl.when(g + 1 < n_samples)
        def _():
            for c in copies(g + 1, 1 - slot):
                c.start()

        for c in copies(g, slot):
            c.wait()
        state = _sample_scores(row, sq_ref, skn_ref, kbuf.at[slot], sbias_ref, sbnew_ref)
        _sample_output(row, state, svn_ref, vbuf.at[slot], so_ref)

    quarter = SUPER // 4
    for src, dst in ((q_ref, qp), (k_ref, kcur), (v_ref, vcur)):
        for c in range(4):
            tmp[pl.ds(c * quarter, quarter), :] = src[pl.ds(c, quarter, stride=4), :]
        for c in range(4):
            for b in range(4):
                dst[pl.ds((4 * b + c) * DIL_STEPS, DIL_STEPS), :] = tmp[pl.ds(c * quarter + b, DIL_STEPS, stride=4), :]

    @pl.when(sb == 0)
    def _():
        kprev[...] = jnp.zeros(kprev.shape, F32)
        vprev[...] = jnp.zeros(vprev.shape, F32)

    head0 = lax.broadcasted_iota(jnp.int32, (DIL_STEPS, LANES), 1) < HEAD_DIM

    def block(di, starts, run, prev_ref_k, prev_ref_v, prev_starts, first, mode, out_rows=None):
        qb = _gather_rows(qp, starts, run)
        kb = jnp.concatenate([_gather_rows(prev_ref_k, prev_starts, run),
                              _gather_rows(kcur, starts, run)], axis=0).astype(BF16)
        vb = jnp.concatenate([_gather_rows(prev_ref_v, prev_starts, run),
                              _gather_rows(vcur, starts, run)], axis=0).astype(BF16)
        ms, ls, os_ = [], [], []
        for h in range(HEADS_PER_GROUP):
            keep = head0 if h == 0 else jnp.logical_not(head0)
            qh = jnp.where(keep, qb, 0.0).astype(BF16)
            s = lax.dot_general(qh, kb, (((1,), (1,)), ((), ())), preferred_element_type=F32)
            s = s + bias_ref[h, di, first]
            mh = jnp.max(s, axis=1, keepdims=True)
            p = jnp.exp(s - mh)
            ls.append(jnp.sum(p, axis=1, keepdims=True))
            os_.append(jnp.dot(p.astype(BF16), vb, preferred_element_type=F32))
            ms.append(mh)
        m_c = jnp.where(head0, ms[0], ms[1])
        l_c = jnp.where(head0, ls[0], ls[1])
        o_c = jnp.where(head0, os_[0], os_[1])
        if mode == "init":
            _scatter_rows(m_sc, starts, run, m_c)
            _scatter_rows(l_sc, starts, run, l_c)
            _scatter_rows(acc, starts, run, o_c)
            return
        m_o = _gather_rows(m_sc, starts, run)
        m_n = jnp.maximum(m_o, m_c)
        a_o = jnp.exp(m_o - m_n)
        a_c = jnp.exp(m_c - m_n)
        l_n = _gather_rows(l_sc, starts, run) * a_o + l_c * a_c
        o_n = _gather_rows(acc, starts, run) * a_o + o_c * a_c
        if mode == "merge":
            _scatter_rows(m_sc, starts, run, m_n)
            _scatter_rows(l_sc, starts, run, l_n)
            _scatter_rows(acc, starts, run, o_n)
        else:
            o_ref[out_rows, :] = o_n / l_n

    def run_dilation(di, mode):
        d = DILATIONS[di]
        runs = N_RES // d
        run = DIL_STEPS // runs
        nblk = SUPER // d // DIL_STEPS

        def body(it, carry):
            c = it // nblk
            n = it % nblk
            starts = [pl.multiple_of((d * b + c) * DIL_STEPS + run * n, SUBLANES) for b in range(runs)]
            pn = jnp.where(n > 0, n - 1, nblk - 1)
            prev_starts = [pl.multiple_of((d * b + c) * DIL_STEPS + run * pn, SUBLANES)
                           for b in range(runs)]
            pslot_n = jnp.where(n > 0, slot, pslot)
            first = jnp.logical_and(sb == 0, n == 0).astype(jnp.int32)
            out_rows = pl.ds(it, DIL_STEPS, stride=N_RES) if mode == "final" else None
            block(di, starts, run, kp.at[pslot_n], vp.at[pslot_n], prev_starts, first, mode, out_rows)
            return carry

        lax.fori_loop(0, d * nblk, body, 0, unroll=ATTN_UNROLL)

    phases = ("init", "merge", "final")
    for t in range(max(samples_per_step, len(phases))):
        if t < samples_per_step:
            sample(t)
        if t < len(phases):
            run_dilation(t, phases[t])


def _attn(qk, vhbc, bias, batch, seq, sqk, svhbc, cache_k, cache_v, sbias, sbias_new):
    nsb = seq // SUPER
    m = batch * seq
    bd, lb = cache_k.shape[0], cache_k.shape[1]
    da = D_ATTN
    n_steps = N_GROUPS * batch * nsb
    assert bd % (2 * n_steps) == 0 and lb % SAMPLE_POS_CHUNK == 0
    assert N_RES == 16, "the residue-major copy is written as two stride-4 passes"

    def feature_major(c):
        return jnp.transpose(c, (0, 2, 3, 1)).reshape(bd, da, lb)

    def plane(p):
        return pl.BlockSpec((None, None, SUPER, LANES), lambda g, b, s: (p, g, b * nsb + s, 0))

    def splane(p):
        return pl.BlockSpec((None, N_GROUPS, bd, LANES), lambda g, b, s: (p, 0, 0, 0))

    fixed2 = lambda g, b, s: (0, 0)
    bias_spec = pl.BlockSpec((HEADS_PER_GROUP,) + bias.shape[1:], lambda g, b, s: (g, 0, 0, 0, 0),
                             pipeline_mode=pl.Buffered(1))
    hbm = pl.BlockSpec(memory_space=pl.ANY)
    blk_bytes = _nbytes((SUPER, LANES), F32)
    return pl.pallas_call(
        functools.partial(_attn_kernel, samples_per_step=bd // n_steps),
        out_shape=(jax.ShapeDtypeStruct((N_GROUPS, m, LANES), F32),
                   jax.ShapeDtypeStruct((N_GROUPS, bd, LANES), F32)),
        grid=(N_GROUPS, batch, nsb),
        in_specs=[plane(0), plane(1), plane(0), bias_spec,
                  splane(0), splane(1), splane(0), hbm, hbm,
                  pl.BlockSpec(sbias.shape, fixed2), pl.BlockSpec(sbias_new.shape, fixed2)],
        out_specs=(pl.BlockSpec((None, SUPER, LANES), lambda g, b, s: (g, b * nsb + s, 0)),
                   pl.BlockSpec((N_GROUPS, bd, LANES), lambda g, b, s: (0, 0, 0))),
        scratch_shapes=[pltpu.VMEM((SUPER, LANES), F32),
                        pltpu.VMEM((2, SUPER, LANES), F32), pltpu.VMEM((2, SUPER, LANES), F32),
                        pltpu.VMEM((SUPER, LANES), F32), pltpu.VMEM((SUPER, LANES), F32),
                        pltpu.VMEM((SUPER, LANES), F32), pltpu.VMEM((SUPER, LANES), F32),
                        pltpu.VMEM((2, da, lb), F32), pltpu.VMEM((2, da, lb), F32),
                        pltpu.SemaphoreType.DMA((2, 2))],
        compiler_params=_cparams(("arbitrary", "arbitrary", "arbitrary"),
                                 (2 * 4 + 9) * blk_bytes
                                 + _nbytes((HEADS_PER_GROUP,) + bias.shape[1:], F32)
                                 + 4 * _nbytes((da, lb), F32) + 2 * 4 * _nbytes((bd, da), F32)
                                 + 2 * _nbytes(sbias.shape, F32)),
        name="attn",
    )(qk, qk, vhbc, bias, sqk, sqk, svhbc, feature_major(cache_k), feature_major(cache_v), sbias, sbias_new)


def _mix_tail(x, a, conv, ga_ref, gc_ref, w_ref):
    cat = jnp.concatenate([_rms(a, ga_ref[...]), _rms(conv, gc_ref[...])], axis=1).astype(BF16)
    return x + jnp.dot(cat, w_ref[...], preferred_element_type=F32)


def _mix_prompt_kernel(x_ref, a_ref, h_ref, b_ref, c_ref, hh_ref, ch_ref, cw_ref, ga_ref, gc_ref, gf_ref, w_ref,
                       o_ref, n_ref, ut_ref, *, tiles_per_seq):
    tm = x_ref.shape[0]
    seq_start = pl.program_id(0) % tiles_per_seq == 0
    all8 = pl.ds(0, SUBLANES)
    halo = jnp.where(seq_start, 0.0, _cat_groups(hh_ref, all8) * _cat_groups(ch_ref, all8))
    for rows in _sub_tiles(tm, MIX_SUB_ROWS):
        u = _cat_groups(h_ref, rows) * _cat_groups(c_ref, rows)
        rid = lax.broadcasted_iota(jnp.int32, u.shape, 0)
        u1 = jnp.where(rid == 0, halo[7:8], pltpu.roll(u, 1, axis=0))
        u2 = jnp.where(rid == 0, halo[6:7], jnp.where(rid == 1, halo[7:8], pltpu.roll(u, 2, axis=0)))
        cy = cw_ref[0:1] * u2 + cw_ref[1:2] * u1 + cw_ref[2:3] * u
        conv = _cat_groups(b_ref, rows) * cy
        h1 = _mix_tail(x_ref[rows, :], _cat_groups(a_ref, rows), conv, ga_ref, gc_ref, w_ref)
        o_ref[rows, :] = h1
        n_ref[rows, :] = _rms(h1, gf_ref[...]).astype(BF16)
        halo = u[u.shape[0] - SUBLANES:]
    ut_ref[...] = halo


def _mix_sample_kernel(x_ref, a_ref, h_ref, b_ref, c_ref, buf_ref, cw_ref, ga_ref, gc_ref, gf_ref, w_ref,
                       o_ref, n_ref, u_ref):
    every = pl.ds(0, x_ref.shape[0])
    u = _cat_groups(h_ref, every) * _cat_groups(c_ref, every)
    dc = u.shape[1]
    cy = cw_ref[0:1] * buf_ref[:, :dc] + cw_ref[1:2] * buf_ref[:, dc:] + cw_ref[2:3] * u
    conv = _cat_groups(b_ref, every) * cy
    h1 = _mix_tail(x_ref[...], _cat_groups(a_ref, every), conv, ga_ref, gc_ref, w_ref)
    o_ref[...] = h1
    n_ref[...] = _rms(h1, gf_ref[...]).astype(BF16)
    u_ref[...] = u


def _mix_out_prompt(x, attn, vhbc, conv_w, ga, gc, gf, w_out, seq, tm):
    m, dm = x.shape
    dc = conv_w.shape[1]
    fixed = lambda i: (0, 0)
    halo_blocks = tm // SUBLANES

    def plane(p):
        return pl.BlockSpec((None, N_GROUPS, tm, LANES), lambda i: (p, 0, i, 0))

    def halo(p):
        return pl.BlockSpec((None, N_GROUPS, SUBLANES, LANES),
                            lambda i: (p, 0, jnp.maximum(i * halo_blocks - 1, 0), 0))

    tile = _nbytes((tm, dc), F32)
    return pl.pallas_call(
        functools.partial(_mix_prompt_kernel, tiles_per_seq=seq // tm),
        out_shape=(jax.ShapeDtypeStruct((m, dm), F32), jax.ShapeDtypeStruct((m, dm), BF16),
                   jax.ShapeDtypeStruct((m // tm, SUBLANES, dc), F32)),
        grid=(m // tm,),
        in_specs=[pl.BlockSpec((tm, dm), lambda i: (i, 0)),
                  pl.BlockSpec((N_GROUPS, tm, LANES), lambda i: (0, i, 0)),
                  plane(1), plane(2), plane(3), halo(1), halo(3),
                  pl.BlockSpec((CONV_W, dc), fixed), pl.BlockSpec((1, D_ATTN), fixed),
                  pl.BlockSpec((1, dc), fixed), pl.BlockSpec((1, dm), fixed), _resident(w_out.shape, fixed)],
        out_specs=(pl.BlockSpec((tm, dm), lambda i: (i, 0)), pl.BlockSpec((tm, dm), lambda i: (i, 0)),
                   pl.BlockSpec((None, SUBLANES, dc), lambda i: (i, 0, 0))),
        compiler_params=_cparams(("arbitrary",), 2 * 9 * tile + _nbytes(w_out.shape, BF16)
                                 + 4 * _nbytes((min(tm, MIX_SUB_ROWS), dm), F32)),
        name="mix_out_prompt",
    )(x, attn, vhbc, vhbc, vhbc, vhbc, vhbc, conv_w, ga, gc, gf, w_out)


def _mix_out_sample(x, attn, vhbc, buf, conv_w, ga, gc, gf, w_out):
    m, dm = x.shape
    dc = conv_w.shape[1]
    full = lambda a: pl.BlockSpec(a.shape, lambda i: (0,) * a.ndim)

    def plane(p):
        return pl.BlockSpec((None, N_GROUPS, m, LANES), lambda i: (p, 0, 0, 0))

    return pl.pallas_call(
        _mix_sample_kernel,
        out_shape=(jax.ShapeDtypeStruct((m, dm), F32), jax.ShapeDtypeStruct((m, dm), BF16),
                   jax.ShapeDtypeStruct((m, dc), F32)),
        grid=(1,),
        in_specs=[full(x), full(attn), plane(1), plane(2), plane(3), full(buf), full(conv_w), full(ga),
                  full(gc), full(gf), full(w_out)],
        out_specs=(pl.BlockSpec((m, dm), lambda i: (0, 0)), pl.BlockSpec((m, dm), lambda i: (0, 0)),
                   pl.BlockSpec((m, dc), lambda i: (0, 0))),
        compiler_params=_cparams(("arbitrary",), 2 * (11 * _nbytes((m, dc), F32) + _nbytes(w_out.shape, BF16))),
        name="mix_out_sample",
    )(x, attn, vhbc, vhbc, vhbc, buf, conv_w, ga, gc, gf, w_out)


def _ffn_kernel(n_ref, wg_ref, wu_ref, wd_ref, o_ref, *wcopy_refs):
    @pl.when(pl.program_id(1) == 0)
    def _():
        o_ref[...] = jnp.zeros(o_ref.shape, F32)

    n = n_ref[...]
    acc = None
    for c in range(0, wg_ref.shape[1], FFN_SUB_COLS):
        cols = pl.ds(c, FFN_SUB_COLS)
        wg, wu, wd = wg_ref[:, cols].astype(BF16), wu_ref[:, cols].astype(BF16), wd_ref[cols, :].astype(BF16)
        if wcopy_refs:
            wcopy_refs[0][:, cols], wcopy_refs[1][:, cols], wcopy_refs[2][cols, :] = wg, wu, wd
        gate = jnp.dot(n, wg, preferred_element_type=F32)
        up = jnp.dot(n, wu, preferred_element_type=F32)
        act = (gate / (1.0 + jnp.exp(-gate)) * up).astype(BF16)
        part = jnp.dot(act, wd, preferred_element_type=F32)
        acc = part if acc is None else acc + part
    o_ref[...] += acc


def _ffn(n, wg, wu, wd, tm, tf, emit_bf16_weights=False):
    m, dm = n.shape
    dff = wg.shape[1]
    row = lambda i, f: (i, 0)
    w_specs = [pl.BlockSpec((dm, tf), lambda i, f: (0, f)), pl.BlockSpec((dm, tf), lambda i, f: (0, f)),
               pl.BlockSpec((tf, dm), lambda i, f: (f, 0))]
    out_shape = [jax.ShapeDtypeStruct((m, dm), F32)]
    out_specs = [pl.BlockSpec((tm, dm), row)]
    if emit_bf16_weights:
        assert m == tm, "weight copies are written once: one row tile"
        out_shape += [jax.ShapeDtypeStruct(w.shape, BF16) for w in (wg, wu, wd)]
        out_specs += w_specs
    out = pl.pallas_call(
        _ffn_kernel,
        out_shape=tuple(out_shape),
        grid=(m // tm, dff // tf),
        in_specs=[pl.BlockSpec((tm, dm), row)] + w_specs,
        out_specs=tuple(out_specs),
        compiler_params=_cparams(("arbitrary", "arbitrary"),
                                 2 * (_nbytes((tm, dm), F32) + _nbytes((tm, dm), BF16))
                                 + 2 * 3 * (_nbytes((dm, tf), wg.dtype)
                                            + (_nbytes((dm, tf), BF16) if emit_bf16_weights else 0))
                                 + 4 * _nbytes((tm, tf), F32)),
        name="ffn",
    )(n, wg, wu, wd)
    return out if emit_bf16_weights else out[0]


def _ple_kernel(h_ref, d_ref, p_ref, g_ref, wg_ref, wp_ref, o_ref):
    for rows in _sub_tiles(h_ref.shape[0], PLE_SUB_ROWS):
        x = h_ref[rows, :] + d_ref[rows, :]
        z = jnp.dot(_rms(x, g_ref[...]).astype(BF16), wg_ref[...], preferred_element_type=F32)
        e = jnp.dot(p_ref[rows, :].astype(BF16), wp_ref[...], preferred_element_type=F32)
        o_ref[rows, :] = x + e / (1.0 + jnp.exp(-z))


def _ple(h, d, p, g, wg, wp, tm):
    m, dm = h.shape
    dp = p.shape[1]
    row = lambda i: (i, 0)
    fixed = lambda i: (0, 0)
    return pl.pallas_call(
        _ple_kernel,
        out_shape=jax.ShapeDtypeStruct((m, dm), F32),
        grid=(m // tm,),
        in_specs=[pl.BlockSpec((tm, dm), row), pl.BlockSpec((tm, dm), row), pl.BlockSpec((tm, dp), row),
                  pl.BlockSpec((1, dm), fixed), _resident(wg.shape, fixed), _resident(wp.shape, fixed)],
        out_specs=pl.BlockSpec((tm, dm), row),
        compiler_params=_cparams(("arbitrary",),
                                 2 * (3 * _nbytes((tm, dm), F32) + _nbytes((tm, dp), F32))
                                 + _nbytes(wg.shape, BF16) + _nbytes(wp.shape, BF16)
                                 + 6 * _nbytes((min(tm, PLE_SUB_ROWS), dm), F32)),
        name="ple",
    )(h, d, p, g, wg, wp)


def _tiles(m):
    return dict(proj=min(m, 1024), proj_n=min(m, 2048), mix=min(m, 512), ffn=min(m, 1024), ffn_cols=512,
                ple=min(m, 512))


def _window_rows(t, batch, kw):
    return jnp.transpose(t.reshape(batch, N_HEADS, HEAD_DIM, kw), (0, 3, 1, 2))


def kernel(x_prompt, x_sample, p_prompt, p_sample, cache_k, cache_v, state_conv, rel_bias, g_mix, w_in,
           q_norm, k_norm, conv_w, g_attn_out, g_conv_out, w_out, g_ffn, w_gate, w_up, w_down, g_ple,
           w_ple_gate, w_ple_proj):
    depth = g_mix.shape[0]
    batch, seq, dm = x_prompt.shape
    bd, dec_seq, _ = x_sample.shape
    dc = conv_w.shape[2]
    assert depth == 1 and dec_seq == 1, "single layer, one new position per sample"
    assert seq % SUPER == 0 and dm == D_ATTN + dc and dc == D_ATTN
    assert w_gate.shape[2] % 512 == 0

    bias_p, bias_s, bias_s_new = _bias_tables(rel_bias, cache_k.shape[2])
    gmat = jnp.asarray(np.kron(np.eye(HEADS_PER_GROUP, dtype=np.float32),
                               np.full((HEAD_DIM, HEAD_DIM), 1.0 / HEAD_DIM, np.float32)), BF16)

    i = 0
    row2 = lambda a: a.reshape(1, -1).astype(F32)
    g_mix_i, g_ffn_i, g_ple_i = row2(g_mix[i]), row2(g_ffn[i]), row2(g_ple[i])
    qk_gains = jnp.stack([row2(jnp.tile(q_norm[i], N_HEADS) * SCALE), row2(jnp.tile(k_norm[i], N_HEADS))])
    ga, gc = row2(g_attn_out[i]), row2(g_conv_out[i])
    cw = conv_w[i].astype(F32)
    wpp_i = w_ple_proj[i].astype(BF16)

    mp = batch * seq
    tp, ts = _tiles(mp), _tiles(bd)
    kw = min(SUPER, seq)
    xp = x_prompt.reshape(mp, dm)
    xs = x_sample.reshape(bd, dm)
    qks, kts, w_qk, nrm_s = _proj(xs, w_in[i], 0, 2, 1, bd, bd, ts["proj"], g_mix_i, qk_gains, gmat,
                                  emit_bf16_weights=True)
    vhbcs, vts, w_vhbc = _proj(nrm_s, w_in[i], 2, 4, 0, bd, bd, ts["proj_n"], emit_bf16_weights=True)
    qk, kt, nrm = _proj(xp, w_qk, 0, 2, 1, seq, kw, tp["proj"], g_mix_i, qk_gains, gmat)
    vhbc, vt, wg_i, wu_i, wd_i, w_out_i, wpg_i = _proj(
        nrm, w_vhbc, 0, 4, 0, seq, kw, tp["proj_n"],
        side_casts=(w_gate[i], w_up[i], w_down[i], w_out[i], w_ple_gate[i]))
    attn, attn_s = _attn(qk, vhbc, bias_p, batch, seq, qks, vhbcs, cache_k[i], cache_v[i], bias_s, bias_s_new)

    buf = state_conv[i].astype(F32)
    hs, ns, us = _mix_out_sample(xs, attn_s, vhbcs, buf.reshape(bd, (CONV_W - 1) * dc), cw, ga, gc, g_ffn_i,
                                 w_out_i)
    ds = _ffn(ns, wg_i, wu_i, wd_i, ts["ffn"], ts["ffn_cols"])
    hs = _ple(hs, ds, p_sample[i].reshape(bd, -1), g_ple_i, wpg_i, wpp_i, ts["ple"])

    h, n, u_tail = _mix_out_prompt(xp, attn, vhbc, cw, ga, gc, g_ffn_i, w_out_i, seq, tp["mix"])
    d = _ffn(n, wg_i, wu_i, wd_i, tp["ffn"], tp["ffn_cols"])
    h = _ple(h, d, p_prompt[i].reshape(mp, -1), g_ple_i, wpg_i, wpp_i, tp["ple"])
    y_prompt = h.reshape(batch, seq, dm)
    k_prompt = _window_rows(kt, batch, kw)[None]
    v_prompt = _window_rows(vt, batch, kw)[None]
    tiles_per_seq = seq // tp["mix"]
    conv_prompt = u_tail.reshape(batch, tiles_per_seq, SUBLANES, dc)[None, :, -1, SUBLANES - (CONV_W - 1):]

    y_sample = hs.reshape(bd, dec_seq, dm)
    k_sample = _window_rows(kts, 1, bd).reshape(1, bd, dec_seq, N_HEADS, HEAD_DIM)
    v_sample = _window_rows(vts, 1, bd).reshape(1, bd, dec_seq, N_HEADS, HEAD_DIM)
    conv_sample = jnp.concatenate([buf[:, 1:], us[:, None, :]], axis=1)[None]

    return (y_prompt, y_sample, k_prompt, v_prompt, conv_prompt, k_sample, v_sample, conv_sample)
```

```python
import functools
import math

import numpy as np
import jax
import jax.numpy as jnp
from jax import lax
from jax.experimental import pallas as pl
from jax.experimental.pallas import tpu as pltpu

HEAD_DIM = 64
N_HEADS = 16
D_ATTN = N_HEADS * HEAD_DIM
CONV_W = 3
DIL_STEPS = 128
DILATIONS = (1, 4, 16)
N_BUCKETS = 32
MAX_EXACT = N_BUCKETS // 2
MAX_DIST = DIL_STEPS * max(DILATIONS)
EPS = 1e-6
SCALE = HEAD_DIM ** -0.5

LANES = 128
SUBLANES = 8
BF16_ROWS = 16
HEADS_PER_GROUP = LANES // HEAD_DIM
N_GROUPS = N_HEADS // HEADS_PER_GROUP
SUPER = DIL_STEPS * max(DILATIONS)
N_RES = max(DILATIONS)
MASK_VALUE = -1e30
MASK_BUCKET = N_BUCKETS
TABLE_ROWS = -(-(N_BUCKETS + 1) // BF16_ROWS) * BF16_ROWS
TABLE_CHUNK = 2048
TABLE_UNROLL = 4
SAMPLE_POS_CHUNK = 512
ATTN_UNROLL = 16
PROJ_COLS = 512
PROJ_SPLIT = D_ATTN // PROJ_COLS
MIX_SUB_ROWS = 256
PLE_SUB_ROWS = 512
FFN_SUB_COLS = 256
VMEM_SLACK_BYTES = 8 * 1024 * 1024

F32 = jnp.float32
BF16 = jnp.bfloat16


def _cparams(sem, vmem_bytes):
    return pltpu.CompilerParams(dimension_semantics=sem,
                                vmem_limit_bytes=int(vmem_bytes + VMEM_SLACK_BYTES))


def _nbytes(shape, dtype):
    return int(np.prod(shape)) * jnp.dtype(dtype).itemsize


def _resident(shape, index_map):
    return pl.BlockSpec(shape, index_map, pipeline_mode=pl.Buffered(1))


def _t5_bucket(dist):
    n = np.asarray(dist, np.int32)
    nf = np.maximum(n, 1).astype(np.float32)
    large = MAX_EXACT + (np.log(nf / MAX_EXACT) / np.float32(math.log(MAX_DIST / MAX_EXACT))
                         * (N_BUCKETS - MAX_EXACT)).astype(np.int32)
    large = np.minimum(large, N_BUCKETS - 1)
    return np.where(n < MAX_EXACT, n, large).astype(np.int32)


def _stored_to_natural(d):
    runs = N_RES // d
    run = DIL_STEPS // runs
    j = np.arange(DIL_STEPS)
    return (j % run) * runs + j // run


def _prompt_bias_index():
    out = np.empty((len(DILATIONS), 2, DIL_STEPS, 2 * DIL_STEPS), np.int32)
    for di, d in enumerate(DILATIONS):
        nat = _stored_to_natural(d)
        qi = nat[:, None]
        kj = np.concatenate([nat, nat + DIL_STEPS])[None, :]
        steps = DIL_STEPS + qi - kj
        band = (steps >= 0) & (steps <= DIL_STEPS)
        bucket = _t5_bucket(d * np.clip(steps, 0, DIL_STEPS))
        out[di, 0] = np.where(band, bucket, MASK_BUCKET)
        out[di, 1] = np.where(band & (kj >= DIL_STEPS), bucket, MASK_BUCKET)
    return out


def _sample_bias_index(lb):
    back = lb - np.arange(lb)
    cached = np.stack([np.where((back % d == 0) & (back // d <= DIL_STEPS), _t5_bucket(back), MASK_BUCKET)
                       for d in DILATIONS])
    new = np.stack([_t5_bucket(d * np.zeros(1, np.int32)) for d in DILATIONS])
    return cached, new


def _bias_table_kernel(tab_ref, idx_ref, op_ref, os_ref):
    t = tab_ref[...]
    hi = t.astype(BF16)
    r1 = t - hi.astype(F32)
    mid = r1.astype(BF16)
    lo = (r1 - mid.astype(F32)).astype(BF16)
    rows = lax.broadcasted_iota(jnp.int32, (TABLE_ROWS, TABLE_CHUNK), 0)

    def lookup(o_ref, base):
        def chunk(c, carry):
            dst = pl.multiple_of(c * TABLE_CHUNK, TABLE_CHUNK)
            src = pl.ds(pl.multiple_of(base + dst, TABLE_CHUNK), TABLE_CHUNK)
            onehot = jnp.where(rows == idx_ref[:, src], 1.0, 0.0).astype(BF16)
            acc = jnp.dot(hi, onehot, preferred_element_type=F32)
            acc = acc + jnp.dot(mid, onehot, preferred_element_type=F32)
            acc = acc + jnp.dot(lo, onehot, preferred_element_type=F32)
            o_ref[:, pl.ds(dst, TABLE_CHUNK)] = acc
            return carry

        lax.fori_loop(0, o_ref.shape[1] // TABLE_CHUNK, chunk, 0, unroll=TABLE_UNROLL)

    lookup(op_ref, 0)
    lookup(os_ref, op_ref.shape[1])


def _bias_tables(rel_bias, lb):
    pidx = _prompt_bias_index()
    cached, new = _sample_bias_index(lb)
    n_p = pidx.size
    n_s = -(-(cached.size + LANES) // (TABLE_CHUNK * TABLE_UNROLL)) * TABLE_CHUNK * TABLE_UNROLL
    assert n_p % (TABLE_CHUNK * TABLE_UNROLL) == 0
    idx = np.full((1, n_p + n_s), MASK_BUCKET, np.int32)
    flat = np.concatenate([pidx.reshape(-1), cached.reshape(-1), new.reshape(-1)])
    idx[0, :flat.size] = flat
    tab = jnp.concatenate(
        [rel_bias.astype(F32).T,
         jnp.full((N_HEADS, 1), MASK_VALUE, F32),
         jnp.zeros((N_HEADS, TABLE_ROWS - N_BUCKETS - 1), F32)], axis=1)
    whole = lambda a: pl.BlockSpec(a, lambda i: (0, 0))
    prompt, samp = pl.pallas_call(
        _bias_table_kernel,
        out_shape=(jax.ShapeDtypeStruct((N_HEADS, n_p), F32), jax.ShapeDtypeStruct((N_HEADS, n_s), F32)),
        grid=(1,),
        in_specs=[whole((N_HEADS, TABLE_ROWS)), whole((1, n_p + n_s))],
        out_specs=(whole((N_HEADS, n_p)), whole((N_HEADS, n_s))),
        compiler_params=_cparams(("arbitrary",), 2 * (_nbytes((N_HEADS, n_p + n_s), F32)
                                                      + _nbytes((SUBLANES, n_p + n_s), jnp.int32))),
        name="bias_table",
    )(tab, jnp.asarray(idx))
    prompt = prompt.reshape(N_HEADS, len(DILATIONS), 2, DIL_STEPS, 2 * DIL_STEPS)
    return prompt, samp[:, :cached.size], samp[:, cached.size:cached.size + LANES]


def _rms(x, g):
    return x * lax.rsqrt(jnp.mean(x * x, axis=-1, keepdims=True) + EPS) * g


def _head_rms(p, g, gmat):
    sq = (p * p).astype(BF16)
    ms = jnp.concatenate([jnp.dot(sq[:, c:c + LANES], gmat, preferred_element_type=F32)
                          for c in range(0, p.shape[1], LANES)], axis=1)
    return p * lax.rsqrt(ms + EPS) * g


def _cat_groups(ref, rows):
    return jnp.concatenate([ref[g, rows, :] for g in range(N_GROUPS)], axis=1)


def _sub_tiles(tm, sub):
    sub = min(tm, sub)
    return [pl.ds(r, sub) for r in range(0, tm, sub)]


def _proj_kernel(*refs, normed, emit_w, n_cast, t_plane, tiles_per_seq, first_kept):
    i, j = pl.program_id(0), pl.program_id(1)
    if normed:
        x_ref, g_ref, w_ref, gain_ref, gmat_ref, o_ref, t_ref, *rest = refs
        n_ref = rest[-1]

        @pl.when(j == 0)
        def _():
            n_ref[...] = _rms(x_ref[...], g_ref[...]).astype(BF16)
    else:
        n_ref, w_ref, *rest = refs
        cast_in, (o_ref, t_ref, *rest) = rest[:n_cast], rest[n_cast:]
        for src, dst in zip(cast_in, rest[len(rest) - n_cast:]):
            dst[...] = src[...].astype(BF16)

    w = w_ref[...].astype(BF16)
    if emit_w:
        rest[0][...] = w
    y = jnp.dot(n_ref[...], w, preferred_element_type=F32)
    if normed:
        y = _head_rms(y, gain_ref[...], gmat_ref[...])
    for g in range(PROJ_COLS // LANES):
        o_ref[g] = y[:, g * LANES:(g + 1) * LANES]

    @pl.when(jnp.logical_and(j // PROJ_SPLIT == t_plane, i % tiles_per_seq >= first_kept))
    def _():
        t_ref[...] = y.T


def _proj(x, w_in, plane0, n_planes, t_plane, seq, kw, tm, g_mix=None, gains=None, gmat=None,
          emit_bf16_weights=False, side_casts=()):
    m, dm = x.shape
    da = D_ATTN
    assert seq % tm == 0 and kw % tm == 0 and m % seq == 0
    assert not emit_bf16_weights or m == tm, "weight copies are written once: one row tile"
    tps, first_kept, batch = seq // tm, (seq - kw) // tm, m // seq
    normed = gains is not None
    assert not (normed and side_casts)
    grp = PROJ_COLS // LANES
    n_cols = n_planes * PROJ_SPLIT
    n_steps = (m // tm) * n_cols
    w_spec = pl.BlockSpec((dm, PROJ_COLS), lambda i, j: (0, plane0 * PROJ_SPLIT + j))
    if normed:
        in_specs = [pl.BlockSpec((tm, dm), lambda i, j: (i, 0)), pl.BlockSpec((1, dm), lambda i, j: (0, 0)), w_spec,
                    pl.BlockSpec((None, 1, PROJ_COLS), lambda i, j: (j, 0, 0)),
                    _resident((LANES, LANES), lambda i, j: (0, 0))]
        args = [x, g_mix, w_in, gains.reshape(n_planes * PROJ_SPLIT, 1, PROJ_COLS), gmat]
    else:
        in_specs = [pl.BlockSpec((tm, dm), lambda i, j: (i, 0)), w_spec]
        args = [x, w_in]
    slab_specs = []
    for a in side_casts:
        assert a.shape[0] % (n_steps * BF16_ROWS) == 0, "whole packed bf16 row tiles per step"
        slab_specs.append(pl.BlockSpec((a.shape[0] // n_steps, a.shape[1]), lambda i, j: (i * n_cols + j, 0)))
    in_specs += slab_specs
    args += list(side_casts)

    def t_index(i, j):
        kept = i % tps >= first_kept
        half = jnp.where(kept, jnp.clip(j - t_plane * PROJ_SPLIT, 0, PROJ_SPLIT - 1), 0)
        return (i // tps, half, jnp.maximum(i % tps - first_kept, 0))

    out_specs = [pl.BlockSpec((None, grp, tm, LANES), lambda i, j: (j // PROJ_SPLIT, j % PROJ_SPLIT, i, 0)),
                 pl.BlockSpec((None, PROJ_COLS, tm), t_index)]
    out_shape = [jax.ShapeDtypeStruct((n_planes, N_GROUPS, m, LANES), F32),
                 jax.ShapeDtypeStruct((batch, da, kw), F32)]
    if emit_bf16_weights:
        out_specs.append(pl.BlockSpec((dm, PROJ_COLS), lambda i, j: (0, j)))
        out_shape.append(jax.ShapeDtypeStruct((dm, n_planes * da), BF16))
    if normed:
        out_specs.append(pl.BlockSpec((tm, dm), lambda i, j: (i, 0)))
        out_shape.append(jax.ShapeDtypeStruct((m, dm), BF16))
    out_specs += slab_specs
    out_shape += [jax.ShapeDtypeStruct(a.shape, BF16) for a in side_casts]
    vmem = (2 * (_nbytes((tm, dm), x.dtype) + _nbytes((dm, PROJ_COLS), w_in.dtype) + _nbytes((dm, PROJ_COLS), BF16)
                 + 2 * _nbytes((tm, PROJ_COLS), F32) + _nbytes((tm, dm), BF16))
            + 6 * _nbytes((tm, PROJ_COLS), F32)
            + 2 * sum(_nbytes(a.shape, F32) + _nbytes(a.shape, BF16) for a in side_casts) // n_steps)
    return pl.pallas_call(
        functools.partial(_proj_kernel, normed=normed, emit_w=emit_bf16_weights, n_cast=len(side_casts),
                          t_plane=t_plane, tiles_per_seq=tps, first_kept=first_kept),
        out_shape=tuple(out_shape),
        grid=(m // tm, n_cols),
        in_specs=in_specs,
        out_specs=tuple(out_specs),
        compiler_params=_cparams(("arbitrary", "arbitrary"), vmem),
        name="in_proj_qk" if normed else "in_proj_vhbc",
    )(*args)


def _gather_rows(ref, starts, run):
    parts = [ref[pl.ds(s, run), :] for s in starts]
    return parts[0] if len(parts) == 1 else jnp.concatenate(parts, axis=0)


def _scatter_rows(ref, starts, run, val):
    for i, s in enumerate(starts):
        ref[pl.ds(s, run), :] = val[i * run:(i + 1) * run]


def _sample_scores(row, q_ref, kn_ref, kt_ref, bias_ref, bnew_ref):
    lb = kt_ref.shape[1]
    nd = len(DILATIONS)
    seg = (lax.broadcasted_iota(jnp.int32, (N_HEADS, D_ATTN), 1) // HEAD_DIM
           == lax.broadcasted_iota(jnp.int32, (N_HEADS, D_ATTN), 0))
    qe = jnp.where(seg, _cat_groups(q_ref, row), 0.0).astype(BF16)
    kn = _cat_groups(kn_ref, row).astype(BF16).astype(F32)
    s_new = jnp.sum(qe.astype(F32) * kn, axis=1, keepdims=True)
    sn = [s_new + bnew_ref[:, di:di + 1] for di in range(nd)]
    starts = range(0, lb, SAMPLE_POS_CHUNK)
    chunks = [pl.ds(c, SAMPLE_POS_CHUNK) for c in starts]
    sc = []
    for c, ch in zip(starts, chunks):
        s = jnp.dot(qe, kt_ref[:, ch].astype(BF16), preferred_element_type=F32)
        sc.append([s + bias_ref[:, pl.ds(di * lb + c, SAMPLE_POS_CHUNK)] for di in range(nd)])
    m = functools.reduce(jnp.maximum, [jnp.max(x, axis=1, keepdims=True) for row_ in sc for x in row_] + sn)
    pn = functools.reduce(jnp.add, [jnp.exp(x - m) for x in sn])
    ps = [functools.reduce(jnp.add, [jnp.exp(x - m) for x in sd]) for sd in sc]
    return seg, chunks, ps, pn


def _sample_output(row, state, vn_ref, vt_ref, o_ref):
    seg, chunks, ps, pn = state
    den = pn
    o = pn * _cat_groups(vn_ref, row)
    for ch, p in zip(chunks, ps):
        den = den + jnp.sum(p, axis=1, keepdims=True)
        o = o + lax.dot_general(p.astype(BF16), vt_ref[:, ch].astype(BF16),
                                (((1,), (1,)), ((), ())), preferred_element_type=F32)
    out = jnp.sum(jnp.where(seg, o / den, 0.0), axis=0, keepdims=True)
    for g in range(N_GROUPS):
        o_ref[g, row, :] = out[:, g * LANES:(g + 1) * LANES]


def _attn_kernel(q_ref, k_ref, v_ref, bias_ref, sq_ref, skn_ref, svn_ref, ck_hbm, cv_hbm, sbias_ref,
                 sbnew_ref, o_ref, so_ref, qp, kp, vp, acc, m_sc, l_sc, tmp, kbuf, vbuf, sem, *, samples_per_step):
    sb = pl.program_id(2)
    step = (pl.program_id(0) * pl.num_programs(1) + pl.program_id(1)) * pl.num_programs(2) + sb
    n_samples = ck_hbm.shape[0]
    slot = sb % 2
    pslot = 1 - slot
    kcur, vcur = kp.at[slot], vp.at[slot]
    kprev, vprev = kp.at[pslot], vp.at[pslot]

    def copies(g, slot):
        return (pltpu.make_async_copy(ck_hbm.at[g], kbuf.at[slot], sem.at[0, slot]),
                pltpu.make_async_copy(cv_hbm.at[g], vbuf.at[slot], sem.at[1, slot]))

    @pl.when(step == 0)
    def _():
        for c in copies(0, 0):
            c.start()

    def sample(t):
        g = step * samples_per_step + t
        slot = t % 2
        row = pl.ds(g, 1)

        @pl.when(g + 1 < n_samples)
        def _():
            for c in copies(g + 1, 1 - slot):
                c.start()

        for c in copies(g, slot):
            c.wait()
        state = _sample_scores(row, sq_ref, skn_ref, kbuf.at[slot], sbias_ref, sbnew_ref)
        _sample_output(row, state, svn_ref, vbuf.at[slot], so_ref)

    quarter = SUPER // 4
    for src, dst in ((q_ref, qp), (k_ref, kcur), (v_ref, vcur)):
        for c in range(4):
            tmp[pl.ds(c * quarter, quarter), :] = src[pl.ds(c, quarter, stride=4), :]
        for c in range(4):
            for b in range(4):
                dst[pl.ds((4 * b + c) * DIL_STEPS, DIL_STEPS), :] = tmp[pl.ds(c * quarter + b, DIL_STEPS, stride=4), :]

    @pl.when(sb == 0)
    def _():
        kprev[...] = jnp.zeros(kprev.shape, F32)
        vprev[...] = jnp.zeros(vprev.shape, F32)

    head0 = lax.broadcasted_iota(jnp.int32, (DIL_STEPS, LANES), 1) < HEAD_DIM

    def block(di, starts, run, prev_ref_k, prev_ref_v, prev_starts, first, mode, out_rows=None):
        qb = _gather_rows(qp, starts, run)
        kb = jnp.concatenate([_gather_rows(prev_ref_k, prev_starts, run),
                              _gather_rows(kcur, starts, run)], axis=0).astype(BF16)
        vb = jnp.concatenate([_gather_rows(prev_ref_v, prev_starts, run),
                              _gather_rows(vcur, starts, run)], axis=0).astype(BF16)
        ms, ls, os_ = [], [], []
        for h in range(HEADS_PER_GROUP):
            keep = head0 if h == 0 else jnp.logical_not(head0)
            qh = jnp.where(keep, qb, 0.0).astype(BF16)
            s = lax.dot_general(qh, kb, (((1,), (1,)), ((), ())), preferred_element_type=F32)
            s = s + bias_ref[h, di, first]
            mh = jnp.max(s, axis=1, keepdims=True)
            p = jnp.exp(s - mh)
            ls.append(jnp.sum(p, axis=1, keepdims=True))
            os_.append(jnp.dot(p.astype(BF16), vb, preferred_element_type=F32))
            ms.append(mh)
        m_c = jnp.where(head0, ms[0], ms[1])
        l_c = jnp.where(head0, ls[0], ls[1])
        o_c = jnp.where(head0, os_[0], os_[1])
        if mode == "init":
            _scatter_rows(m_sc, starts, run, m_c)
            _scatter_rows(l_sc, starts, run, l_c)
            _scatter_rows(acc, starts, run, o_c)
            return
        m_o = _gather_rows(m_sc, starts, run)
        m_n = jnp.maximum(m_o, m_c)
        a_o = jnp.exp(m_o - m_n)
        a_c = jnp.exp(m_c - m_n)
        l_n = _gather_rows(l_sc, starts, run) * a_o + l_c * a_c
        o_n = _gather_rows(acc, starts, run) * a_o + o_c * a_c
        if mode == "merge":
            _scatter_rows(m_sc, starts, run, m_n)
            _scatter_rows(l_sc, starts, run, l_n)
            _scatter_rows(acc, starts, run, o_n)
        else:
            o_ref[out_rows, :] = o_n / l_n

    def run_dilation(di, mode):
        d = DILATIONS[di]
        runs = N_RES // d
        run = DIL_STEPS // runs
        nblk = SUPER // d // DIL_STEPS

        def body(it, carry):
            c = it // nblk
            n = it % nblk
            starts = [pl.multiple_of((d * b + c) * DIL_STEPS + run * n, SUBLANES) for b in range(runs)]
            pn = jnp.where(n > 0, n - 1, nblk - 1)
            prev_starts = [pl.multiple_of((d * b + c) * DIL_STEPS + run * pn, SUBLANES)
                           for b in range(runs)]
            pslot_n = jnp.where(n > 0, slot, pslot)
            first = jnp.logical_and(sb == 0, n == 0).astype(jnp.int32)
            out_rows = pl.ds(it, DIL_STEPS, stride=N_RES) if mode == "final" else None
            block(di, starts, run, kp.at[pslot_n], vp.at[pslot_n], prev_starts, first, mode, out_rows)
            return carry

        lax.fori_loop(0, d * nblk, body, 0, unroll=ATTN_UNROLL)

    phases = ("init", "merge", "final")
    for t in range(max(samples_per_step, len(phases))):
        if t < samples_per_step:
            sample(t)
        if t < len(phases):
            run_dilation(t, phases[t])


def _attn(qk, vhbc, bias, batch, seq, sqk, svhbc, cache_k, cache_v, sbias, sbias_new):
    nsb = seq // SUPER
    m = batch * seq
    bd, lb = cache_k.shape[0], cache_k.shape[1]
    da = D_ATTN
    n_steps = N_GROUPS * batch * nsb
    assert bd % (2 * n_steps) == 0 and lb % SAMPLE_POS_CHUNK == 0
    assert N_RES == 16, "the residue-major copy is written as two stride-4 passes"

    def feature_major(c):
        return jnp.transpose(c, (0, 2, 3, 1)).reshape(bd, da, lb)

    def plane(p):
        return pl.BlockSpec((None, None, SUPER, LANES), lambda g, b, s: (p, g, b * nsb + s, 0))

    def splane(p):
        return pl.BlockSpec((None, N_GROUPS, bd, LANES), lambda g, b, s: (p, 0, 0, 0))

    fixed2 = lambda g, b, s: (0, 0)
    bias_spec = pl.BlockSpec((HEADS_PER_GROUP,) + bias.shape[1:], lambda g, b, s: (g, 0, 0, 0, 0),
                             pipeline_mode=pl.Buffered(1))
    hbm = pl.BlockSpec(memory_space=pl.ANY)
    blk_bytes = _nbytes((SUPER, LANES), F32)
    return pl.pallas_call(
        functools.partial(_attn_kernel, samples_per_step=bd // n_steps),
        out_shape=(jax.ShapeDtypeStruct((N_GROUPS, m, LANES), F32),
                   jax.ShapeDtypeStruct((N_GROUPS, bd, LANES), F32)),
        grid=(N_GROUPS, batch, nsb),
        in_specs=[plane(0), plane(1), plane(0), bias_spec,
                  splane(0), splane(1), splane(0), hbm, hbm,
                  pl.BlockSpec(sbias.shape, fixed2), pl.BlockSpec(sbias_new.shape, fixed2)],
        out_specs=(pl.BlockSpec((None, SUPER, LANES), lambda g, b, s: (g, b * nsb + s, 0)),
                   pl.BlockSpec((N_GROUPS, bd, LANES), lambda g, b, s: (0, 0, 0))),
        scratch_shapes=[pltpu.VMEM((SUPER, LANES), F32),
                        pltpu.VMEM((2, SUPER, LANES), F32), pltpu.VMEM((2, SUPER, LANES), F32),
                        pltpu.VMEM((SUPER, LANES), F32), pltpu.VMEM((SUPER, LANES), F32),
                        pltpu.VMEM((SUPER, LANES), F32), pltpu.VMEM((SUPER, LANES), F32),
                        pltpu.VMEM((2, da, lb), F32), pltpu.VMEM((2, da, lb), F32),
                        pltpu.SemaphoreType.DMA((2, 2))],
        compiler_params=_cparams(("arbitrary", "arbitrary", "arbitrary"),
                                 (2 * 4 + 9) * blk_bytes
                                 + _nbytes((HEADS_PER_GROUP,) + bias.shape[1:], F32)
                                 + 4 * _nbytes((da, lb), F32) + 2 * 4 * _nbytes((bd, da), F32)
                                 + 2 * _nbytes(sbias.shape, F32)),
        name="attn",
    )(qk, qk, vhbc, bias, sqk, sqk, svhbc, feature_major(cache_k), feature_major(cache_v), sbias, sbias_new)


def _mix_tail(x, a, conv, ga_ref, gc_ref, w_ref):
    cat = jnp.concatenate([_rms(a, ga_ref[...]), _rms(conv, gc_ref[...])], axis=1).astype(BF16)
    return x + jnp.dot(cat, w_ref[...], preferred_element_type=F32)


def _mix_prompt_kernel(x_ref, a_ref, h_ref, b_ref, c_ref, hh_ref, ch_ref, cw_ref, ga_ref, gc_ref, gf_ref, w_ref,
                       o_ref, n_ref, ut_ref, *, tiles_per_seq):
    tm = x_ref.shape[0]
    seq_start = pl.program_id(0) % tiles_per_seq == 0
    all8 = pl.ds(0, SUBLANES)
    halo = jnp.where(seq_start, 0.0, _cat_groups(hh_ref, all8) * _cat_groups(ch_ref, all8))
    for rows in _sub_tiles(tm, MIX_SUB_ROWS):
        u = _cat_groups(h_ref, rows) * _cat_groups(c_ref, rows)
        rid = lax.broadcasted_iota(jnp.int32, u.shape, 0)
        u1 = jnp.where(rid == 0, halo[7:8], pltpu.roll(u, 1, axis=0))
        u2 = jnp.where(rid == 0, halo[6:7], jnp.where(rid == 1, halo[7:8], pltpu.roll(u, 2, axis=0)))
        cy = cw_ref[0:1] * u2 + cw_ref[1:2] * u1 + cw_ref[2:3] * u
        conv = _cat_groups(b_ref, rows) * cy
        h1 = _mix_tail(x_ref[rows, :], _cat_groups(a_ref, rows), conv, ga_ref, gc_ref, w_ref)
        o_ref[rows, :] = h1
        n_ref[rows, :] = _rms(h1, gf_ref[...]).astype(BF16)
        halo = u[u.shape[0] - SUBLANES:]
    ut_ref[...] = halo


def _mix_sample_kernel(x_ref, a_ref, h_ref, b_ref, c_ref, buf_ref, cw_ref, ga_ref, gc_ref, gf_ref, w_ref,
                       o_ref, n_ref, u_ref):
    every = pl.ds(0, x_ref.shape[0])
    u = _cat_groups(h_ref, every) * _cat_groups(c_ref, every)
    dc = u.shape[1]
    cy = cw_ref[0:1] * buf_ref[:, :dc] + cw_ref[1:2] * buf_ref[:, dc:] + cw_ref[2:3] * u
    conv = _cat_groups(b_ref, every) * cy
    h1 = _mix_tail(x_ref[...], _cat_groups(a_ref, every), conv, ga_ref, gc_ref, w_ref)
    o_ref[...] = h1
    n_ref[...] = _rms(h1, gf_ref[...]).astype(BF16)
    u_ref[...] = u


def _mix_out_prompt(x, attn, vhbc, conv_w, ga, gc, gf, w_out, seq, tm):
    m, dm = x.shape
    dc = conv_w.shape[1]
    fixed = lambda i: (0, 0)
    halo_blocks = tm // SUBLANES

    def plane(p):
        return pl.BlockSpec((None, N_GROUPS, tm, LANES), lambda i: (p, 0, i, 0))

    def halo(p):
        return pl.BlockSpec((None, N_GROUPS, SUBLANES, LANES),
                            lambda i: (p, 0, jnp.maximum(i * halo_blocks - 1, 0), 0))

    tile = _nbytes((tm, dc), F32)
    return pl.pallas_call(
        functools.partial(_mix_prompt_kernel, tiles_per_seq=seq // tm),
        out_shape=(jax.ShapeDtypeStruct((m, dm), F32), jax.ShapeDtypeStruct((m, dm), BF16),
                   jax.ShapeDtypeStruct((m // tm, SUBLANES, dc), F32)),
        grid=(m // tm,),
        in_specs=[pl.BlockSpec((tm, dm), lambda i: (i, 0)),
                  pl.BlockSpec((N_GROUPS, tm, LANES), lambda i: (0, i, 0)),
                  plane(1), plane(2), plane(3), halo(1), halo(3),
                  pl.BlockSpec((CONV_W, dc), fixed), pl.BlockSpec((1, D_ATTN), fixed),
                  pl.BlockSpec((1, dc), fixed), pl.BlockSpec((1, dm), fixed), _resident(w_out.shape, fixed)],
        out_specs=(pl.BlockSpec((tm, dm), lambda i: (i, 0)), pl.BlockSpec((tm, dm), lambda i: (i, 0)),
                   pl.BlockSpec((None, SUBLANES, dc), lambda i: (i, 0, 0))),
        compiler_params=_cparams(("arbitrary",), 2 * 9 * tile + _nbytes(w_out.shape, BF16)
                                 + 4 * _nbytes((min(tm, MIX_SUB_ROWS), dm), F32)),
        name="mix_out_prompt",
    )(x, attn, vhbc, vhbc, vhbc, vhbc, vhbc, conv_w, ga, gc, gf, w_out)


def _mix_out_sample(x, attn, vhbc, buf, conv_w, ga, gc, gf, w_out):
    m, dm = x.shape
    dc = conv_w.shape[1]
    full = lambda a: pl.BlockSpec(a.shape, lambda i: (0,) * a.ndim)

    def plane(p):
        return pl.BlockSpec((None, N_GROUPS, m, LANES), lambda i: (p, 0, 0, 0))

    return pl.pallas_call(
        _mix_sample_kernel,
        out_shape=(jax.ShapeDtypeStruct((m, dm), F32), jax.ShapeDtypeStruct((m, dm), BF16),
                   jax.ShapeDtypeStruct((m, dc), F32)),
        grid=(1,),
        in_specs=[full(x), full(attn), plane(1), plane(2), plane(3), full(buf), full(conv_w), full(ga),
                  full(gc), full(gf), full(w_out)],
        out_specs=(pl.BlockSpec((m, dm), lambda i: (0, 0)), pl.BlockSpec((m, dm), lambda i: (0, 0)),
                   pl.BlockSpec((m, dc), lambda i: (0, 0))),
        compiler_params=_cparams(("arbitrary",), 2 * (11 * _nbytes((m, dc), F32) + _nbytes(w_out.shape, BF16))),
        name="mix_out_sample",
    )(x, attn, vhbc, vhbc, vhbc, buf, conv_w, ga, gc, gf, w_out)


def _ffn_kernel(n_ref, wg_ref, wu_ref, wd_ref, o_ref):
    @pl.when(pl.program_id(1) == 0)
    def _():
        o_ref[...] = jnp.zeros(o_ref.shape, F32)

    n = n_ref[...]
    acc = None
    for c in range(0, wg_ref.shape[1], FFN_SUB_COLS):
        cols = pl.ds(c, FFN_SUB_COLS)
        gate = jnp.dot(n, wg_ref[:, cols], preferred_element_type=F32)
        up = jnp.dot(n, wu_ref[:, cols], preferred_element_type=F32)
        act = (gate / (1.0 + jnp.exp(-gate)) * up).astype(BF16)
        part = jnp.dot(act, wd_ref[cols, :], preferred_element_type=F32)
        acc = part if acc is None else acc + part
    o_ref[...] += acc


def _ffn(n, wg, wu, wd, tm, tf):
    m, dm = n.shape
    dff = wg.shape[1]
    row = lambda i, f: (i, 0)
    return pl.pallas_call(
        _ffn_kernel,
        out_shape=jax.ShapeDtypeStruct((m, dm), F32),
        grid=(m // tm, dff // tf),
        in_specs=[pl.BlockSpec((tm, dm), row),
                  pl.BlockSpec((dm, tf), lambda i, f: (0, f)), pl.BlockSpec((dm, tf), lambda i, f: (0, f)),
                  pl.BlockSpec((tf, dm), lambda i, f: (f, 0))],
        out_specs=pl.BlockSpec((tm, dm), row),
        compiler_params=_cparams(("arbitrary", "arbitrary"),
                                 2 * (_nbytes((tm, dm), F32) + _nbytes((tm, dm), BF16))
                                 + 2 * 3 * _nbytes((dm, tf), BF16) + 4 * _nbytes((tm, tf), F32)),
        name="ffn",
    )(n, wg, wu, wd)


def _ple_kernel(h_ref, d_ref, p_ref, g_ref, wg_ref, wp_ref, o_ref):
    for rows in _sub_tiles(h_ref.shape[0], PLE_SUB_ROWS):
        x = h_ref[rows, :] + d_ref[rows, :]
        z = jnp.dot(_rms(x, g_ref[...]).astype(BF16), wg_ref[...], preferred_element_type=F32)
        e = jnp.dot(p_ref[rows, :].astype(BF16), wp_ref[...], preferred_element_type=F32)
        o_ref[rows, :] = x + e / (1.0 + jnp.exp(-z))


def _ple(h, d, p, g, wg, wp, tm):
    m, dm = h.shape
    dp = p.shape[1]
    row = lambda i: (i, 0)
    fixed = lambda i: (0, 0)
    return pl.pallas_call(
        _ple_kernel,
        out_shape=jax.ShapeDtypeStruct((m, dm), F32),
        grid=(m // tm,),
        in_specs=[pl.BlockSpec((tm, dm), row), pl.BlockSpec((tm, dm), row), pl.BlockSpec((tm, dp), row),
                  pl.BlockSpec((1, dm), fixed), _resident(wg.shape, fixed), _resident(wp.shape, fixed)],
        out_specs=pl.BlockSpec((tm, dm), row),
        compiler_params=_cparams(("arbitrary",),
                                 2 * (3 * _nbytes((tm, dm), F32) + _nbytes((tm, dp), F32))
                                 + _nbytes(wg.shape, BF16) + _nbytes(wp.shape, BF16)
                                 + 6 * _nbytes((min(tm, PLE_SUB_ROWS), dm), F32)),
        name="ple",
    )(h, d, p, g, wg, wp)


def _tiles(m):
    return dict(proj=min(m, 1024), proj_n=min(m, 2048), mix=min(m, 512), ffn=min(m, 1024), ffn_cols=512,
                ple=min(m, 512))


def _window_rows(t, batch, kw):
    return jnp.transpose(t.reshape(batch, N_HEADS, HEAD_DIM, kw), (0, 3, 1, 2))


def kernel(x_prompt, x_sample, p_prompt, p_sample, cache_k, cache_v, state_conv, rel_bias, g_mix, w_in,
           q_norm, k_norm, conv_w, g_attn_out, g_conv_out, w_out, g_ffn, w_gate, w_up, w_down, g_ple,
           w_ple_gate, w_ple_proj):
    depth = g_mix.shape[0]
    batch, seq, dm = x_prompt.shape
    bd, dec_seq, _ = x_sample.shape
    dc = conv_w.shape[2]
    assert depth == 1 and dec_seq == 1, "single layer, one new position per sample"
    assert seq % SUPER == 0 and dm == D_ATTN + dc and dc == D_ATTN
    assert w_gate.shape[2] % 512 == 0

    bias_p, bias_s, bias_s_new = _bias_tables(rel_bias, cache_k.shape[2])
    gmat = jnp.asarray(np.kron(np.eye(HEADS_PER_GROUP, dtype=np.float32),
                               np.full((HEAD_DIM, HEAD_DIM), 1.0 / HEAD_DIM, np.float32)), BF16)

    i = 0
    row2 = lambda a: a.reshape(1, -1).astype(F32)
    g_mix_i, g_ffn_i, g_ple_i = row2(g_mix[i]), row2(g_ffn[i]), row2(g_ple[i])
    qk_gains = jnp.stack([row2(jnp.tile(q_norm[i], N_HEADS) * SCALE), row2(jnp.tile(k_norm[i], N_HEADS))])
    ga, gc = row2(g_attn_out[i]), row2(g_conv_out[i])
    cw = conv_w[i].astype(F32)
    wpp_i = w_ple_proj[i].astype(BF16)

    mp = batch * seq
    tp, ts = _tiles(mp), _tiles(bd)
    kw = min(SUPER, seq)
    xp = x_prompt.reshape(mp, dm)
    xs = x_sample.reshape(bd, dm)
    qks, kts, w_qk, nrm_s = _proj(xs, w_in[i], 0, 2, 1, bd, bd, ts["proj"], g_mix_i, qk_gains, gmat,
                                  emit_bf16_weights=True)
    vhbcs, vts, w_vhbc = _proj(nrm_s, w_in[i], 2, 4, 0, bd, bd, ts["proj_n"], emit_bf16_weights=True)
    qk, kt, nrm = _proj(xp, w_qk, 0, 2, 1, seq, kw, tp["proj"], g_mix_i, qk_gains, gmat)
    vhbc, vt, wg_i, wu_i, wd_i, w_out_i, wpg_i = _proj(
        nrm, w_vhbc, 0, 4, 0, seq, kw, tp["proj_n"],
        side_casts=(w_gate[i], w_up[i], w_down[i], w_out[i], w_ple_gate[i]))
    attn, attn_s = _attn(qk, vhbc, bias_p, batch, seq, qks, vhbcs, cache_k[i], cache_v[i], bias_s, bias_s_new)

    buf = state_conv[i].astype(F32)
    hs, ns, us = _mix_out_sample(xs, attn_s, vhbcs, buf.reshape(bd, (CONV_W - 1) * dc), cw, ga, gc, g_ffn_i,
                                 w_out_i)
    ds = _ffn(ns, wg_i, wu_i, wd_i, ts["ffn"], ts["ffn_cols"])
    hs = _ple(hs, ds, p_sample[i].reshape(bd, -1), g_ple_i, wpg_i, wpp_i, ts["ple"])

    h, n, u_tail = _mix_out_prompt(xp, attn, vhbc, cw, ga, gc, g_ffn_i, w_out_i, seq, tp["mix"])
    d = _ffn(n, wg_i, wu_i, wd_i, tp["ffn"], tp["ffn_cols"])
    h = _ple(h, d, p_prompt[i].reshape(mp, -1), g_ple_i, wpg_i, wpp_i, tp["ple"])
    y_prompt = h.reshape(batch, seq, dm)
    k_prompt = _window_rows(kt, batch, kw)[None]
    v_prompt = _window_rows(vt, batch, kw)[None]
    tiles_per_seq = seq // tp["mix"]
    conv_prompt = u_tail.reshape(batch, tiles_per_seq, SUBLANES, dc)[None, :, -1, SUBLANES - (CONV_W - 1):]

    y_sample = hs.reshape(bd, dec_seq, dm)
    k_sample = _window_rows(kts, 1, bd).reshape(1, bd, dec_seq, N_HEADS, HEAD_DIM)
    v_sample = _window_rows(vts, 1, bd).reshape(1, bd, dec_seq, N_HEADS, HEAD_DIM)
    conv_sample = jnp.concatenate([buf[:, 1:], us[:, None, :]], axis=1)[None]

    return (y_prompt, y_sample, k_prompt, v_prompt, conv_prompt, k_sample, v_sample, conv_sample)
```

```python
import functools
import math

import numpy as np
import jax
import jax.numpy as jnp
from jax import lax
from jax.experimental import pallas as pl
from jax.experimental.pallas import tpu as pltpu

HEAD_DIM = 64
N_HEADS = 16
D_ATTN = N_HEADS * HEAD_DIM
CONV_W = 3
DIL_STEPS = 128
DILATIONS = (1, 4, 16)
N_BUCKETS = 32
MAX_EXACT = N_BUCKETS // 2
MAX_DIST = DIL_STEPS * max(DILATIONS)
EPS = 1e-6
SCALE = HEAD_DIM ** -0.5

LANES = 128
SUBLANES = 8
BF16_ROWS = 16
HEADS_PER_GROUP = LANES // HEAD_DIM
N_GROUPS = N_HEADS // HEADS_PER_GROUP
SUPER = DIL_STEPS * max(DILATIONS)
N_RES = max(DILATIONS)
MASK_VALUE = -1e30
MASK_BUCKET = N_BUCKETS
TABLE_ROWS = -(-(N_BUCKETS + 1) // BF16_ROWS) * BF16_ROWS
TABLE_CHUNK = 2048
TABLE_UNROLL = 4
SAMPLE_POS_CHUNK = 512
ATTN_UNROLL = 16
PROJ_COLS = 512
PROJ_SPLIT = D_ATTN // PROJ_COLS
MIX_SUB_ROWS = 256
PLE_SUB_ROWS = 512
FFN_SUB_COLS = 256
VMEM_SLACK_BYTES = 8 * 1024 * 1024

F32 = jnp.float32
BF16 = jnp.bfloat16


def _cparams(sem, vmem_bytes):
    return pltpu.CompilerParams(dimension_semantics=sem,
                                vmem_limit_bytes=int(vmem_bytes + VMEM_SLACK_BYTES))


def _nbytes(shape, dtype):
    return int(np.prod(shape)) * jnp.dtype(dtype).itemsize


def _resident(shape, index_map):
    return pl.BlockSpec(shape, index_map, pipeline_mode=pl.Buffered(1))


def _t5_bucket(dist):
    n = np.asarray(dist, np.int32)
    nf = np.maximum(n, 1).astype(np.float32)
    large = MAX_EXACT + (np.log(nf / MAX_EXACT) / np.float32(math.log(MAX_DIST / MAX_EXACT))
                         * (N_BUCKETS - MAX_EXACT)).astype(np.int32)
    large = np.minimum(large, N_BUCKETS - 1)
    return np.where(n < MAX_EXACT, n, large).astype(np.int32)


def _stored_to_natural(d):
    runs = N_RES // d
    run = DIL_STEPS // runs
    j = np.arange(DIL_STEPS)
    return (j % run) * runs + j // run


def _prompt_bias_index():
    out = np.empty((len(DILATIONS), 2, DIL_STEPS, 2 * DIL_STEPS), np.int32)
    for di, d in enumerate(DILATIONS):
        nat = _stored_to_natural(d)
        qi = nat[:, None]
        kj = np.concatenate([nat, nat + DIL_STEPS])[None, :]
        steps = DIL_STEPS + qi - kj
        band = (steps >= 0) & (steps <= DIL_STEPS)
        bucket = _t5_bucket(d * np.clip(steps, 0, DIL_STEPS))
        out[di, 0] = np.where(band, bucket, MASK_BUCKET)
        out[di, 1] = np.where(band & (kj >= DIL_STEPS), bucket, MASK_BUCKET)
    return out


def _sample_bias_index(lb):
    back = lb - np.arange(lb)
    cached = np.stack([np.where((back % d == 0) & (back // d <= DIL_STEPS), _t5_bucket(back), MASK_BUCKET)
                       for d in DILATIONS])
    new = np.stack([_t5_bucket(d * np.zeros(1, np.int32)) for d in DILATIONS])
    return cached, new


def _bias_table_kernel(tab_ref, idx_ref, op_ref, os_ref):
    t = tab_ref[...]
    hi = t.astype(BF16)
    r1 = t - hi.astype(F32)
    mid = r1.astype(BF16)
    lo = (r1 - mid.astype(F32)).astype(BF16)
    rows = lax.broadcasted_iota(jnp.int32, (TABLE_ROWS, TABLE_CHUNK), 0)

    def lookup(o_ref, base):
        def chunk(c, carry):
            dst = pl.multiple_of(c * TABLE_CHUNK, TABLE_CHUNK)
            src = pl.ds(pl.multiple_of(base + dst, TABLE_CHUNK), TABLE_CHUNK)
            onehot = jnp.where(rows == idx_ref[:, src], 1.0, 0.0).astype(BF16)
            acc = jnp.dot(hi, onehot, preferred_element_type=F32)
            acc = acc + jnp.dot(mid, onehot, preferred_element_type=F32)
            acc = acc + jnp.dot(lo, onehot, preferred_element_type=F32)
            o_ref[:, pl.ds(dst, TABLE_CHUNK)] = acc
            return carry

        lax.fori_loop(0, o_ref.shape[1] // TABLE_CHUNK, chunk, 0, unroll=TABLE_UNROLL)

    lookup(op_ref, 0)
    lookup(os_ref, op_ref.shape[1])


def _bias_tables(rel_bias, lb):
    pidx = _prompt_bias_index()
    cached, new = _sample_bias_index(lb)
    n_p = pidx.size
    n_s = -(-(cached.size + LANES) // (TABLE_CHUNK * TABLE_UNROLL)) * TABLE_CHUNK * TABLE_UNROLL
    assert n_p % (TABLE_CHUNK * TABLE_UNROLL) == 0
    idx = np.full((1, n_p + n_s), MASK_BUCKET, np.int32)
    flat = np.concatenate([pidx.reshape(-1), cached.reshape(-1), new.reshape(-1)])
    idx[0, :flat.size] = flat
    tab = jnp.concatenate(
        [rel_bias.astype(F32).T,
         jnp.full((N_HEADS, 1), MASK_VALUE, F32),
         jnp.zeros((N_HEADS, TABLE_ROWS - N_BUCKETS - 1), F32)], axis=1)
    whole = lambda a: pl.BlockSpec(a, lambda i: (0, 0))
    prompt, samp = pl.pallas_call(
        _bias_table_kernel,
        out_shape=(jax.ShapeDtypeStruct((N_HEADS, n_p), F32), jax.ShapeDtypeStruct((N_HEADS, n_s), F32)),
        grid=(1,),
        in_specs=[whole((N_HEADS, TABLE_ROWS)), whole((1, n_p + n_s))],
        out_specs=(whole((N_HEADS, n_p)), whole((N_HEADS, n_s))),
        compiler_params=_cparams(("arbitrary",), 2 * (_nbytes((N_HEADS, n_p + n_s), F32)
                                                      + _nbytes((SUBLANES, n_p + n_s), jnp.int32))),
        name="bias_table",
    )(tab, jnp.asarray(idx))
    prompt = prompt.reshape(N_HEADS, len(DILATIONS), 2, DIL_STEPS, 2 * DIL_STEPS)
    return prompt, samp[:, :cached.size], samp[:, cached.size:cached.size + LANES]


def _rms(x, g):
    return x * lax.rsqrt(jnp.mean(x * x, axis=-1, keepdims=True) + EPS) * g


def _head_rms(p, g, gmat):
    sq = (p * p).astype(BF16)
    ms = jnp.concatenate([jnp.dot(sq[:, c:c + LANES], gmat, preferred_element_type=F32)
                          for c in range(0, p.shape[1], LANES)], axis=1)
    return p * lax.rsqrt(ms + EPS) * g


def _cat_groups(ref, rows):
    return jnp.concatenate([ref[g, rows, :] for g in range(N_GROUPS)], axis=1)


def _sub_tiles(tm, sub):
    sub = min(tm, sub)
    return [pl.ds(r, sub) for r in range(0, tm, sub)]


def _proj_kernel(*refs, normed, emit_w, n_cast, t_plane, tiles_per_seq, first_kept):
    i, j = pl.program_id(0), pl.program_id(1)
    if normed:
        x_ref, g_ref, w_ref, gain_ref, gmat_ref, o_ref, t_ref, *rest = refs
        n_ref = rest[-1]

        @pl.when(j == 0)
        def _():
            n_ref[...] = _rms(x_ref[...], g_ref[...]).astype(BF16)
    else:
        n_ref, w_ref, *rest = refs
        cast_in, (o_ref, t_ref, *rest) = rest[:n_cast], rest[n_cast:]
        for src, dst in zip(cast_in, rest[len(rest) - n_cast:]):
            dst[...] = src[...].astype(BF16)

    w = w_ref[...].astype(BF16)
    if emit_w:
        rest[0][...] = w
    y = jnp.dot(n_ref[...], w, preferred_element_type=F32)
    if normed:
        y = _head_rms(y, gain_ref[...], gmat_ref[...])
    for g in range(PROJ_COLS // LANES):
        o_ref[g] = y[:, g * LANES:(g + 1) * LANES]

    @pl.when(jnp.logical_and(j // PROJ_SPLIT == t_plane, i % tiles_per_seq >= first_kept))
    def _():
        t_ref[...] = y.T


def _proj(x, w_in, plane0, n_planes, t_plane, seq, kw, tm, g_mix=None, gains=None, gmat=None,
          emit_bf16_weights=False, side_casts=()):
    m, dm = x.shape
    da = D_ATTN
    assert seq % tm == 0 and kw % tm == 0 and m % seq == 0
    assert not emit_bf16_weights or m == tm, "weight copies are written once: one row tile"
    tps, first_kept, batch = seq // tm, (seq - kw) // tm, m // seq
    normed = gains is not None
    assert not (normed and side_casts)
    grp = PROJ_COLS // LANES
    n_cols = n_planes * PROJ_SPLIT
    n_steps = (m // tm) * n_cols
    w_spec = pl.BlockSpec((dm, PROJ_COLS), lambda i, j: (0, plane0 * PROJ_SPLIT + j))
    if normed:
        in_specs = [pl.BlockSpec((tm, dm), lambda i, j: (i, 0)), pl.BlockSpec((1, dm), lambda i, j: (0, 0)), w_spec,
                    pl.BlockSpec((None, 1, PROJ_COLS), lambda i, j: (j, 0, 0)),
                    _resident((LANES, LANES), lambda i, j: (0, 0))]
        args = [x, g_mix, w_in, gains.reshape(n_planes * PROJ_SPLIT, 1, PROJ_COLS), gmat]
    else:
        in_specs = [pl.BlockSpec((tm, dm), lambda i, j: (i, 0)), w_spec]
        args = [x, w_in]
    slab_specs = []
    for a in side_casts:
        assert a.shape[0] % (n_steps * BF16_ROWS) == 0, "whole packed bf16 row tiles per step"
        slab_specs.append(pl.BlockSpec((a.shape[0] // n_steps, a.shape[1]), lambda i, j: (i * n_cols + j, 0)))
    in_specs += slab_specs
    args += list(side_casts)

    def t_index(i, j):
        kept = i % tps >= first_kept
        half = jnp.where(kept, jnp.clip(j - t_plane * PROJ_SPLIT, 0, PROJ_SPLIT - 1), 0)
        return (i // tps, half, jnp.maximum(i % tps - first_kept, 0))

    out_specs = [pl.BlockSpec((None, grp, tm, LANES), lambda i, j: (j // PROJ_SPLIT, j % PROJ_SPLIT, i, 0)),
                 pl.BlockSpec((None, PROJ_COLS, tm), t_index)]
    out_shape = [jax.ShapeDtypeStruct((n_planes, N_GROUPS, m, LANES), F32),
                 jax.ShapeDtypeStruct((batch, da, kw), F32)]
    if emit_bf16_weights:
        out_specs.append(pl.BlockSpec((dm, PROJ_COLS), lambda i, j: (0, j)))
        out_shape.append(jax.ShapeDtypeStruct((dm, n_planes * da), BF16))
    if normed:
        out_specs.append(pl.BlockSpec((tm, dm), lambda i, j: (i, 0)))
        out_shape.append(jax.ShapeDtypeStruct((m, dm), BF16))
    out_specs += slab_specs
    out_shape += [jax.ShapeDtypeStruct(a.shape, BF16) for a in side_casts]
    vmem = (2 * (_nbytes((tm, dm), x.dtype) + _nbytes((dm, PROJ_COLS), w_in.dtype) + _nbytes((dm, PROJ_COLS), BF16)
                 + 2 * _nbytes((tm, PROJ_COLS), F32) + _nbytes((tm, dm), BF16))
            + 6 * _nbytes((tm, PROJ_COLS), F32)
            + 2 * sum(_nbytes(a.shape, F32) + _nbytes(a.shape, BF16) for a in side_casts) // n_steps)
    return pl.pallas_call(
        functools.partial(_proj_kernel, normed=normed, emit_w=emit_bf16_weights, n_cast=len(side_casts),
                          t_plane=t_plane, tiles_per_seq=tps, first_kept=first_kept),
        out_shape=tuple(out_shape),
        grid=(m // tm, n_cols),
        in_specs=in_specs,
        out_specs=tuple(out_specs),
        compiler_params=_cparams(("arbitrary", "arbitrary"), vmem),
        name="in_proj_qk" if normed else "in_proj_vhbc",
    )(*args)


def _gather_rows(ref, starts, run):
    parts = [ref[pl.ds(s, run), :] for s in starts]
    return parts[0] if len(parts) == 1 else jnp.concatenate(parts, axis=0)


def _scatter_rows(ref, starts, run, val):
    for i, s in enumerate(starts):
        ref[pl.ds(s, run), :] = val[i * run:(i + 1) * run]


def _sample_scores(row, q_ref, kn_ref, kt_ref, bias_ref, bnew_ref):
    lb = kt_ref.shape[1]
    nd = len(DILATIONS)
    seg = (lax.broadcasted_iota(jnp.int32, (N_HEADS, D_ATTN), 1) // HEAD_DIM
           == lax.broadcasted_iota(jnp.int32, (N_HEADS, D_ATTN), 0))
    qe = jnp.where(seg, _cat_groups(q_ref, row), 0.0).astype(BF16)
    kn = _cat_groups(kn_ref, row).astype(BF16).astype(F32)
    s_new = jnp.sum(qe.astype(F32) * kn, axis=1, keepdims=True)
    sn = [s_new + bnew_ref[:, di:di + 1] for di in range(nd)]
    starts = range(0, lb, SAMPLE_POS_CHUNK)
    chunks = [pl.ds(c, SAMPLE_POS_CHUNK) for c in starts]
    sc = []
    for c, ch in zip(starts, chunks):
        s = jnp.dot(qe, kt_ref[:, ch].astype(BF16), preferred_element_type=F32)
        sc.append([s + bias_ref[:, pl.ds(di * lb + c, SAMPLE_POS_CHUNK)] for di in range(nd)])
    m = functools.reduce(jnp.maximum, [jnp.max(x, axis=1, keepdims=True) for row_ in sc for x in row_] + sn)
    pn = functools.reduce(jnp.add, [jnp.exp(x - m) for x in sn])
    ps = [functools.reduce(jnp.add, [jnp.exp(x - m) for x in sd]) for sd in sc]
    return seg, chunks, ps, pn


def _sample_output(row, state, vn_ref, vt_ref, o_ref):
    seg, chunks, ps, pn = state
    den = pn
    o = pn * _cat_groups(vn_ref, row)
    for ch, p in zip(chunks, ps):
        den = den + jnp.sum(p, axis=1, keepdims=True)
        o = o + lax.dot_general(p.astype(BF16), vt_ref[:, ch].astype(BF16),
                                (((1,), (1,)), ((), ())), preferred_element_type=F32)
    out = jnp.sum(jnp.where(seg, o / den, 0.0), axis=0, keepdims=True)
    for g in range(N_GROUPS):
        o_ref[g, row, :] = out[:, g * LANES:(g + 1) * LANES]


def _attn_kernel(q_ref, k_ref, v_ref, bias_ref, sq_ref, skn_ref, svn_ref, ck_hbm, cv_hbm, sbias_ref,
                 sbnew_ref, o_ref, so_ref, qp, kp, vp, acc, m_sc, l_sc, tmp, kbuf, vbuf, sem, *, samples_per_step):
    sb = pl.program_id(2)
    step = (pl.program_id(0) * pl.num_programs(1) + pl.program_id(1)) * pl.num_programs(2) + sb
    n_samples = ck_hbm.shape[0]
    slot = sb % 2
    pslot = 1 - slot
    kcur, vcur = kp.at[slot], vp.at[slot]
    kprev, vprev = kp.at[pslot], vp.at[pslot]

    def copies(g, slot):
        return (pltpu.make_async_copy(ck_hbm.at[g], kbuf.at[slot], sem.at[0, slot]),
                pltpu.make_async_copy(cv_hbm.at[g], vbuf.at[slot], sem.at[1, slot]))

    @pl.when(step == 0)
    def _():
        for c in copies(0, 0):
            c.start()

    def sample(t):
        g = step * samples_per_step + t
        slot = t % 2
        row = pl.ds(g, 1)

        @pl.when(g + 1 < n_samples)
        def _():
            for c in copies(g + 1, 1 - slot):
                c.start()

        for c in copies(g, slot):
            c.wait()
        state = _sample_scores(row, sq_ref, skn_ref, kbuf.at[slot], sbias_ref, sbnew_ref)
        _sample_output(row, state, svn_ref, vbuf.at[slot], so_ref)

    quarter = SUPER // 4
    for src, dst in ((q_ref, qp), (k_ref, kcur), (v_ref, vcur)):
        for c in range(4):
            tmp[pl.ds(c * quarter, quarter), :] = src[pl.ds(c, quarter, stride=4), :]
        for c in range(4):
            for b in range(4):
                dst[pl.ds((4 * b + c) * DIL_STEPS, DIL_STEPS), :] = tmp[pl.ds(c * quarter + b, DIL_STEPS, stride=4), :]

    @pl.when(sb == 0)
    def _():
        kprev[...] = jnp.zeros(kprev.shape, F32)
        vprev[...] = jnp.zeros(vprev.shape, F32)

    head0 = lax.broadcasted_iota(jnp.int32, (DIL_STEPS, LANES), 1) < HEAD_DIM

    def block(di, starts, run, prev_ref_k, prev_ref_v, prev_starts, first, mode, out_rows=None):
        qb = _gather_rows(qp, starts, run)
        kb = jnp.concatenate([_gather_rows(prev_ref_k, prev_starts, run),
                              _gather_rows(kcur, starts, run)], axis=0).astype(BF16)
        vb = jnp.concatenate([_gather_rows(prev_ref_v, prev_starts, run),
                              _gather_rows(vcur, starts, run)], axis=0).astype(BF16)
        q2 = jnp.concatenate([jnp.where(head0, qb, 0.0), jnp.where(head0, 0.0, qb)], axis=0).astype(BF16)
        s = lax.dot_general(q2, kb, (((1,), (1,)), ((), ())), preferred_element_type=F32)
        s = s + jnp.concatenate([bias_ref[h, di, first] for h in range(HEADS_PER_GROUP)], axis=0)
        m2 = jnp.max(s, axis=1, keepdims=True)
        p = jnp.exp(s - m2)
        l2 = jnp.sum(p, axis=1, keepdims=True)
        o2 = jnp.dot(p.astype(BF16), vb, preferred_element_type=F32)
        m_c = jnp.where(head0, m2[:DIL_STEPS], m2[DIL_STEPS:])
        l_c = jnp.where(head0, l2[:DIL_STEPS], l2[DIL_STEPS:])
        o_c = jnp.where(head0, o2[:DIL_STEPS], o2[DIL_STEPS:])
        if mode == "init":
            _scatter_rows(m_sc, starts, run, m_c)
            _scatter_rows(l_sc, starts, run, l_c)
            _scatter_rows(acc, starts, run, o_c)
            return
        m_o = _gather_rows(m_sc, starts, run)
        m_n = jnp.maximum(m_o, m_c)
        a_o = jnp.exp(m_o - m_n)
        a_c = jnp.exp(m_c - m_n)
        l_n = _gather_rows(l_sc, starts, run) * a_o + l_c * a_c
        o_n = _gather_rows(acc, starts, run) * a_o + o_c * a_c
        if mode == "merge":
            _scatter_rows(m_sc, starts, run, m_n)
            _scatter_rows(l_sc, starts, run, l_n)
            _scatter_rows(acc, starts, run, o_n)
        else:
            o_ref[out_rows, :] = o_n / l_n

    def run_dilation(di, mode):
        d = DILATIONS[di]
        runs = N_RES // d
        run = DIL_STEPS // runs
        nblk = SUPER // d // DIL_STEPS

        def body(it, carry):
            c = it // nblk
            n = it % nblk
            starts = [pl.multiple_of((d * b + c) * DIL_STEPS + run * n, SUBLANES) for b in range(runs)]
            pn = jnp.where(n > 0, n - 1, nblk - 1)
            prev_starts = [pl.multiple_of((d * b + c) * DIL_STEPS + run * pn, SUBLANES)
                           for b in range(runs)]
            pslot_n = jnp.where(n > 0, slot, pslot)
            first = jnp.logical_and(sb == 0, n == 0).astype(jnp.int32)
            out_rows = pl.ds(it, DIL_STEPS, stride=N_RES) if mode == "final" else None
            block(di, starts, run, kp.at[pslot_n], vp.at[pslot_n], prev_starts, first, mode, out_rows)
            return carry

        lax.fori_loop(0, d * nblk, body, 0, unroll=ATTN_UNROLL)

    phases = ("init", "merge", "final")
    for t in range(max(samples_per_step, len(phases))):
        if t < samples_per_step:
            sample(t)
        if t < len(phases):
            run_dilation(t, phases[t])


def _attn(qk, vhbc, bias, batch, seq, sqk, svhbc, cache_k, cache_v, sbias, sbias_new):
    nsb = seq // SUPER
    m = batch * seq
    bd, lb = cache_k.shape[0], cache_k.shape[1]
    da = D_ATTN
    n_steps = N_GROUPS * batch * nsb
    assert bd % (2 * n_steps) == 0 and lb % SAMPLE_POS_CHUNK == 0
    assert N_RES == 16, "the residue-major copy is written as two stride-4 passes"

    def feature_major(c):
        return jnp.transpose(c, (0, 2, 3, 1)).reshape(bd, da, lb)

    def plane(p):
        return pl.BlockSpec((None, None, SUPER, LANES), lambda g, b, s: (p, g, b * nsb + s, 0))

    def splane(p):
        return pl.BlockSpec((None, N_GROUPS, bd, LANES), lambda g, b, s: (p, 0, 0, 0))

    fixed2 = lambda g, b, s: (0, 0)
    bias_spec = pl.BlockSpec((HEADS_PER_GROUP,) + bias.shape[1:], lambda g, b, s: (g, 0, 0, 0, 0),
                             pipeline_mode=pl.Buffered(1))
    hbm = pl.BlockSpec(memory_space=pl.ANY)
    blk_bytes = _nbytes((SUPER, LANES), F32)
    return pl.pallas_call(
        functools.partial(_attn_kernel, samples_per_step=bd // n_steps),
        out_shape=(jax.ShapeDtypeStruct((N_GROUPS, m, LANES), F32),
                   jax.ShapeDtypeStruct((N_GROUPS, bd, LANES), F32)),
        grid=(N_GROUPS, batch, nsb),
        in_specs=[plane(0), plane(1), plane(0), bias_spec,
                  splane(0), splane(1), splane(0), hbm, hbm,
                  pl.BlockSpec(sbias.shape, fixed2), pl.BlockSpec(sbias_new.shape, fixed2)],
        out_specs=(pl.BlockSpec((None, SUPER, LANES), lambda g, b, s: (g, b * nsb + s, 0)),
                   pl.BlockSpec((N_GROUPS, bd, LANES), lambda g, b, s: (0, 0, 0))),
        scratch_shapes=[pltpu.VMEM((SUPER, LANES), F32),
                        pltpu.VMEM((2, SUPER, LANES), F32), pltpu.VMEM((2, SUPER, LANES), F32),
                        pltpu.VMEM((SUPER, LANES), F32), pltpu.VMEM((SUPER, LANES), F32),
                        pltpu.VMEM((SUPER, LANES), F32), pltpu.VMEM((SUPER, LANES), F32),
                        pltpu.VMEM((2, da, lb), F32), pltpu.VMEM((2, da, lb), F32),
                        pltpu.SemaphoreType.DMA((2, 2))],
        compiler_params=_cparams(("arbitrary", "arbitrary", "arbitrary"),
                                 (2 * 4 + 9) * blk_bytes
                                 + _nbytes((HEADS_PER_GROUP,) + bias.shape[1:], F32)
                                 + 4 * _nbytes((da, lb), F32) + 2 * 4 * _nbytes((bd, da), F32)
                                 + 2 * _nbytes(sbias.shape, F32)),
        name="attn",
    )(qk, qk, vhbc, bias, sqk, sqk, svhbc, feature_major(cache_k), feature_major(cache_v), sbias, sbias_new)


def _mix_tail(x, a, conv, ga_ref, gc_ref, w_ref):
    cat = jnp.concatenate([_rms(a, ga_ref[...]), _rms(conv, gc_ref[...])], axis=1).astype(BF16)
    return x + jnp.dot(cat, w_ref[...], preferred_element_type=F32)


def _mix_prompt_kernel(x_ref, a_ref, h_ref, b_ref, c_ref, hh_ref, ch_ref, cw_ref, ga_ref, gc_ref, gf_ref, w_ref,
                       o_ref, n_ref, ut_ref, *, tiles_per_seq):
    tm = x_ref.shape[0]
    seq_start = pl.program_id(0) % tiles_per_seq == 0
    all8 = pl.ds(0, SUBLANES)
    halo = jnp.where(seq_start, 0.0, _cat_groups(hh_ref, all8) * _cat_groups(ch_ref, all8))
    for rows in _sub_tiles(tm, MIX_SUB_ROWS):
        u = _cat_groups(h_ref, rows) * _cat_groups(c_ref, rows)
        rid = lax.broadcasted_iota(jnp.int32, u.shape, 0)
        u1 = jnp.where(rid == 0, halo[7:8], pltpu.roll(u, 1, axis=0))
        u2 = jnp.where(rid == 0, halo[6:7], jnp.where(rid == 1, halo[7:8], pltpu.roll(u, 2, axis=0)))
        cy = cw_ref[0:1] * u2 + cw_ref[1:2] * u1 + cw_ref[2:3] * u
        conv = _cat_groups(b_ref, rows) * cy
        h1 = _mix_tail(x_ref[rows, :], _cat_groups(a_ref, rows), conv, ga_ref, gc_ref, w_ref)
        o_ref[rows, :] = h1
        n_ref[rows, :] = _rms(h1, gf_ref[...]).astype(BF16)
        halo = u[u.shape[0] - SUBLANES:]
    ut_ref[...] = halo


def _mix_sample_kernel(x_ref, a_ref, h_ref, b_ref, c_ref, buf_ref, cw_ref, ga_ref, gc_ref, gf_ref, w_ref,
                       o_ref, n_ref, u_ref):
    every = pl.ds(0, x_ref.shape[0])
    u = _cat_groups(h_ref, every) * _cat_groups(c_ref, every)
    dc = u.shape[1]
    cy = cw_ref[0:1] * buf_ref[:, :dc] + cw_ref[1:2] * buf_ref[:, dc:] + cw_ref[2:3] * u
    conv = _cat_groups(b_ref, every) * cy
    h1 = _mix_tail(x_ref[...], _cat_groups(a_ref, every), conv, ga_ref, gc_ref, w_ref)
    o_ref[...] = h1
    n_ref[...] = _rms(h1, gf_ref[...]).astype(BF16)
    u_ref[...] = u


def _mix_out_prompt(x, attn, vhbc, conv_w, ga, gc, gf, w_out, seq, tm):
    m, dm = x.shape
    dc = conv_w.shape[1]
    fixed = lambda i: (0, 0)
    halo_blocks = tm // SUBLANES

    def plane(p):
        return pl.BlockSpec((None, N_GROUPS, tm, LANES), lambda i: (p, 0, i, 0))

    def halo(p):
        return pl.BlockSpec((None, N_GROUPS, SUBLANES, LANES),
                            lambda i: (p, 0, jnp.maximum(i * halo_blocks - 1, 0), 0))

    tile = _nbytes((tm, dc), F32)
    return pl.pallas_call(
        functools.partial(_mix_prompt_kernel, tiles_per_seq=seq // tm),
        out_shape=(jax.ShapeDtypeStruct((m, dm), F32), jax.ShapeDtypeStruct((m, dm), BF16),
                   jax.ShapeDtypeStruct((m // tm, SUBLANES, dc), F32)),
        grid=(m // tm,),
        in_specs=[pl.BlockSpec((tm, dm), lambda i: (i, 0)),
                  pl.BlockSpec((N_GROUPS, tm, LANES), lambda i: (0, i, 0)),
                  plane(1), plane(2), plane(3), halo(1), halo(3),
                  pl.BlockSpec((CONV_W, dc), fixed), pl.BlockSpec((1, D_ATTN), fixed),
                  pl.BlockSpec((1, dc), fixed), pl.BlockSpec((1, dm), fixed), _resident(w_out.shape, fixed)],
        out_specs=(pl.BlockSpec((tm, dm), lambda i: (i, 0)), pl.BlockSpec((tm, dm), lambda i: (i, 0)),
                   pl.BlockSpec((None, SUBLANES, dc), lambda i: (i, 0, 0))),
        compiler_params=_cparams(("arbitrary",), 2 * 9 * tile + _nbytes(w_out.shape, BF16)
                                 + 4 * _nbytes((min(tm, MIX_SUB_ROWS), dm), F32)),
        name="mix_out_prompt",
    )(x, attn, vhbc, vhbc, vhbc, vhbc, vhbc, conv_w, ga, gc, gf, w_out)


def _mix_out_sample(x, attn, vhbc, buf, conv_w, ga, gc, gf, w_out):
    m, dm = x.shape
    dc = conv_w.shape[1]
    full = lambda a: pl.BlockSpec(a.shape, lambda i: (0,) * a.ndim)

    def plane(p):
        return pl.BlockSpec((None, N_GROUPS, m, LANES), lambda i: (p, 0, 0, 0))

    return pl.pallas_call(
        _mix_sample_kernel,
        out_shape=(jax.ShapeDtypeStruct((m, dm), F32), jax.ShapeDtypeStruct((m, dm), BF16),
                   jax.ShapeDtypeStruct((m, dc), F32)),
        grid=(1,),
        in_specs=[full(x), full(attn), plane(1), plane(2), plane(3), full(buf), full(conv_w), full(ga),
                  full(gc), full(gf), full(w_out)],
        out_specs=(pl.BlockSpec((m, dm), lambda i: (0, 0)), pl.BlockSpec((m, dm), lambda i: (0, 0)),
                   pl.BlockSpec((m, dc), lambda i: (0, 0))),
        compiler_params=_cparams(("arbitrary",), 2 * (11 * _nbytes((m, dc), F32) + _nbytes(w_out.shape, BF16))),
        name="mix_out_sample",
    )(x, attn, vhbc, vhbc, vhbc, buf, conv_w, ga, gc, gf, w_out)


def _ffn_kernel(n_ref, wg_ref, wu_ref, wd_ref, o_ref):
    @pl.when(pl.program_id(1) == 0)
    def _():
        o_ref[...] = jnp.zeros(o_ref.shape, F32)

    n = n_ref[...]
    acc = None
    for c in range(0, wg_ref.shape[1], FFN_SUB_COLS):
        cols = pl.ds(c, FFN_SUB_COLS)
        gate = jnp.dot(n, wg_ref[:, cols], preferred_element_type=F32)
        up = jnp.dot(n, wu_ref[:, cols], preferred_element_type=F32)
        act = (gate / (1.0 + jnp.exp(-gate)) * up).astype(BF16)
        part = jnp.dot(act, wd_ref[cols, :], preferred_element_type=F32)
        acc = part if acc is None else acc + part
    o_ref[...] += acc


def _ffn(n, wg, wu, wd, tm, tf):
    m, dm = n.shape
    dff = wg.shape[1]
    row = lambda i, f: (i, 0)
    return pl.pallas_call(
        _ffn_kernel,
        out_shape=jax.ShapeDtypeStruct((m, dm), F32),
        grid=(m // tm, dff // tf),
        in_specs=[pl.BlockSpec((tm, dm), row),
                  pl.BlockSpec((dm, tf), lambda i, f: (0, f)), pl.BlockSpec((dm, tf), lambda i, f: (0, f)),
                  pl.BlockSpec((tf, dm), lambda i, f: (f, 0))],
        out_specs=pl.BlockSpec((tm, dm), row),
        compiler_params=_cparams(("arbitrary", "arbitrary"),
                                 2 * (_nbytes((tm, dm), F32) + _nbytes((tm, dm), BF16))
                                 + 2 * 3 * _nbytes((dm, tf), BF16) + 4 * _nbytes((tm, tf), F32)),
        name="ffn",
    )(n, wg, wu, wd)


def _ple_kernel(h_ref, d_ref, p_ref, g_ref, wg_ref, wp_ref, o_ref):
    for rows in _sub_tiles(h_ref.shape[0], PLE_SUB_ROWS):
        x = h_ref[rows, :] + d_ref[rows, :]
        z = jnp.dot(_rms(x, g_ref[...]).astype(BF16), wg_ref[...], preferred_element_type=F32)
        e = jnp.dot(p_ref[rows, :].astype(BF16), wp_ref[...], preferred_element_type=F32)
        o_ref[rows, :] = x + e / (1.0 + jnp.exp(-z))


def _ple(h, d, p, g, wg, wp, tm):
    m, dm = h.shape
    dp = p.shape[1]
    row = lambda i: (i, 0)
    fixed = lambda i: (0, 0)
    return pl.pallas_call(
        _ple_kernel,
        out_shape=jax.ShapeDtypeStruct((m, dm), F32),
        grid=(m // tm,),
        in_specs=[pl.BlockSpec((tm, dm), row), pl.BlockSpec((tm, dm), row), pl.BlockSpec((tm, dp), row),
                  pl.BlockSpec((1, dm), fixed), _resident(wg.shape, fixed), _resident(wp.shape, fixed)],
        out_specs=pl.BlockSpec((tm, dm), row),
        compiler_params=_cparams(("arbitrary",),
                                 2 * (3 * _nbytes((tm, dm), F32) + _nbytes((tm, dp), F32))
                                 + _nbytes(wg.shape, BF16) + _nbytes(wp.shape, BF16)
                                 + 6 * _nbytes((min(tm, PLE_SUB_ROWS), dm), F32)),
        name="ple",
    )(h, d, p, g, wg, wp)


def _tiles(m):
    return dict(proj=min(m, 1024), proj_n=min(m, 2048), mix=min(m, 512), ffn=min(m, 1024), ffn_cols=512,
                ple=min(m, 512))


def _window_rows(t, batch, kw):
    return jnp.transpose(t.reshape(batch, N_HEADS, HEAD_DIM, kw), (0, 3, 1, 2))


def kernel(x_prompt, x_sample, p_prompt, p_sample, cache_k, cache_v, state_conv, rel_bias, g_mix, w_in,
           q_norm, k_norm, conv_w, g_attn_out, g_conv_out, w_out, g_ffn, w_gate, w_up, w_down, g_ple,
           w_ple_gate, w_ple_proj):
    depth = g_mix.shape[0]
    batch, seq, dm = x_prompt.shape
    bd, dec_seq, _ = x_sample.shape
    dc = conv_w.shape[2]
    assert depth == 1 and dec_seq == 1, "single layer, one new position per sample"
    assert seq % SUPER == 0 and dm == D_ATTN + dc and dc == D_ATTN
    assert w_gate.shape[2] % 512 == 0

    bias_p, bias_s, bias_s_new = _bias_tables(rel_bias, cache_k.shape[2])
    gmat = jnp.asarray(np.kron(np.eye(HEADS_PER_GROUP, dtype=np.float32),
                               np.full((HEAD_DIM, HEAD_DIM), 1.0 / HEAD_DIM, np.float32)), BF16)

    i = 0
    row2 = lambda a: a.reshape(1, -1).astype(F32)
    g_mix_i, g_ffn_i, g_ple_i = row2(g_mix[i]), row2(g_ffn[i]), row2(g_ple[i])
    qk_gains = jnp.stack([row2(jnp.tile(q_norm[i], N_HEADS) * SCALE), row2(jnp.tile(k_norm[i], N_HEADS))])
    ga, gc = row2(g_attn_out[i]), row2(g_conv_out[i])
    cw = conv_w[i].astype(F32)
    wpp_i = w_ple_proj[i].astype(BF16)

    mp = batch * seq
    tp, ts = _tiles(mp), _tiles(bd)
    kw = min(SUPER, seq)
    xp = x_prompt.reshape(mp, dm)
    xs = x_sample.reshape(bd, dm)
    qks, kts, w_qk, nrm_s = _proj(xs, w_in[i], 0, 2, 1, bd, bd, ts["proj"], g_mix_i, qk_gains, gmat,
                                  emit_bf16_weights=True)
    vhbcs, vts, w_vhbc = _proj(nrm_s, w_in[i], 2, 4, 0, bd, bd, ts["proj_n"], emit_bf16_weights=True)
    qk, kt, nrm = _proj(xp, w_qk, 0, 2, 1, seq, kw, tp["proj"], g_mix_i, qk_gains, gmat)
    vhbc, vt, wg_i, wu_i, wd_i, w_out_i, wpg_i = _proj(
        nrm, w_vhbc, 0, 4, 0, seq, kw, tp["proj_n"],
        side_casts=(w_gate[i], w_up[i], w_down[i], w_out[i], w_ple_gate[i]))
    attn, attn_s = _attn(qk, vhbc, bias_p, batch, seq, qks, vhbcs, cache_k[i], cache_v[i], bias_s, bias_s_new)

    buf = state_conv[i].astype(F32)
    hs, ns, us = _mix_out_sample(xs, attn_s, vhbcs, buf.reshape(bd, (CONV_W - 1) * dc), cw, ga, gc, g_ffn_i,
                                 w_out_i)
    ds = _ffn(ns, wg_i, wu_i, wd_i, ts["ffn"], ts["ffn_cols"])
    hs = _ple(hs, ds, p_sample[i].reshape(bd, -1), g_ple_i, wpg_i, wpp_i, ts["ple"])

    h, n, u_tail = _mix_out_prompt(xp, attn, vhbc, cw, ga, gc, g_ffn_i, w_out_i, seq, tp["mix"])
    d = _ffn(n, wg_i, wu_i, wd_i, tp["ffn"], tp["ffn_cols"])
    h = _ple(h, d, p_prompt[i].reshape(mp, -1), g_ple_i, wpg_i, wpp_i, tp["ple"])
    y_prompt = h.reshape(batch, seq, dm)
    k_prompt = _window_rows(kt, batch, kw)[None]
    v_prompt = _window_rows(vt, batch, kw)[None]
    tiles_per_seq = seq // tp["mix"]
    conv_prompt = u_tail.reshape(batch, tiles_per_seq, SUBLANES, dc)[None, :, -1, SUBLANES - (CONV_W - 1):]

    y_sample = hs.reshape(bd, dec_seq, dm)
    k_sample = _window_rows(kts, 1, bd).reshape(1, bd, dec_seq, N_HEADS, HEAD_DIM)
    v_sample = _window_rows(vts, 1, bd).reshape(1, bd, dec_seq, N_HEADS, HEAD_DIM)
    conv_sample = jnp.concatenate([buf[:, 1:], us[:, None, :]], axis=1)[None]

    return (y_prompt, y_sample, k_prompt, v_prompt, conv_prompt, k_sample, v_sample, conv_sample)
```

```python
import functools
import math

import numpy as np
import jax
import jax.numpy as jnp
from jax import lax
from jax.experimental import pallas as pl
from jax.experimental.pallas import tpu as pltpu

HEAD_DIM = 64
N_HEADS = 16
D_ATTN = N_HEADS * HEAD_DIM
CONV_W = 3
DIL_STEPS = 128
DILATIONS = (1, 4, 16)
N_BUCKETS = 32
MAX_EXACT = N_BUCKETS // 2
MAX_DIST = DIL_STEPS * max(DILATIONS)
EPS = 1e-6
SCALE = HEAD_DIM ** -0.5

LANES = 128
SUBLANES = 8
BF16_ROWS = 16
HEADS_PER_GROUP = LANES // HEAD_DIM
N_GROUPS = N_HEADS // HEADS_PER_GROUP
SUPER = DIL_STEPS * max(DILATIONS)
N_RES = max(DILATIONS)
MASK_VALUE = -1e30
MASK_BUCKET = N_BUCKETS
TABLE_ROWS = -(-(N_BUCKETS + 1) // BF16_ROWS) * BF16_ROWS
TABLE_CHUNK = 2048
TABLE_UNROLL = 4
SAMPLE_POS_CHUNK = 512
ATTN_UNROLL = 16
PROJ_COLS = 512
PROJ_SPLIT = D_ATTN // PROJ_COLS
MIX_SUB_ROWS = 256
PLE_SUB_ROWS = 512
FFN_SUB_COLS = 256
VMEM_SLACK_BYTES = 8 * 1024 * 1024

F32 = jnp.float32
BF16 = jnp.bfloat16


def _cparams(sem, vmem_bytes):
    return pltpu.CompilerParams(dimension_semantics=sem,
                                vmem_limit_bytes=int(vmem_bytes + VMEM_SLACK_BYTES))


def _nbytes(shape, dtype):
    return int(np.prod(shape)) * jnp.dtype(dtype).itemsize


def _resident(shape, index_map):
    return pl.BlockSpec(shape, index_map, pipeline_mode=pl.Buffered(1))


def _t5_bucket(dist):
    n = np.asarray(dist, np.int32)
    nf = np.maximum(n, 1).astype(np.float32)
    large = MAX_EXACT + (np.log(nf / MAX_EXACT) / np.float32(math.log(MAX_DIST / MAX_EXACT))
                         * (N_BUCKETS - MAX_EXACT)).astype(np.int32)
    large = np.minimum(large, N_BUCKETS - 1)
    return np.where(n < MAX_EXACT, n, large).astype(np.int32)


def _stored_to_natural(d):
    runs = N_RES // d
    run = DIL_STEPS // runs
    j = np.arange(DIL_STEPS)
    return (j % run) * runs + j // run


def _prompt_bias_index():
    out = np.empty((len(DILATIONS), 2, DIL_STEPS, 2 * DIL_STEPS), np.int32)
    for di, d in enumerate(DILATIONS):
        nat = _stored_to_natural(d)
        qi = nat[:, None]
        kj = np.concatenate([nat, nat + DIL_STEPS])[None, :]
        steps = DIL_STEPS + qi - kj
        band = (steps >= 0) & (steps <= DIL_STEPS)
        bucket = _t5_bucket(d * np.clip(steps, 0, DIL_STEPS))
        out[di, 0] = np.where(band, bucket, MASK_BUCKET)
        out[di, 1] = np.where(band & (kj >= DIL_STEPS), bucket, MASK_BUCKET)
    return out


def _sample_bias_index(lb):
    back = lb - np.arange(lb)
    cached = np.stack([np.where((back % d == 0) & (back // d <= DIL_STEPS), _t5_bucket(back), MASK_BUCKET)
                       for d in DILATIONS])
    new = np.stack([_t5_bucket(d * np.zeros(1, np.int32)) for d in DILATIONS])
    return cached, new


def _bias_table_kernel(tab_ref, idx_ref, op_ref, os_ref):
    t = tab_ref[...]
    hi = t.astype(BF16)
    r1 = t - hi.astype(F32)
    mid = r1.astype(BF16)
    lo = (r1 - mid.astype(F32)).astype(BF16)
    rows = lax.broadcasted_iota(jnp.int32, (TABLE_ROWS, TABLE_CHUNK), 0)

    def lookup(o_ref, base):
        def chunk(c, carry):
            dst = pl.multiple_of(c * TABLE_CHUNK, TABLE_CHUNK)
            src = pl.ds(pl.multiple_of(base + dst, TABLE_CHUNK), TABLE_CHUNK)
            onehot = jnp.where(rows == idx_ref[:, src], 1.0, 0.0).astype(BF16)
            acc = jnp.dot(hi, onehot, preferred_element_type=F32)
            acc = acc + jnp.dot(mid, onehot, preferred_element_type=F32)
            acc = acc + jnp.dot(lo, onehot, preferred_element_type=F32)
            o_ref[:, pl.ds(dst, TABLE_CHUNK)] = acc
            return carry

        lax.fori_loop(0, o_ref.shape[1] // TABLE_CHUNK, chunk, 0, unroll=TABLE_UNROLL)

    lookup(op_ref, 0)
    lookup(os_ref, op_ref.shape[1])


def _bias_tables(rel_bias, lb):
    pidx = _prompt_bias_index()
    cached, new = _sample_bias_index(lb)
    n_p = pidx.size
    n_s = -(-(cached.size + LANES) // (TABLE_CHUNK * TABLE_UNROLL)) * TABLE_CHUNK * TABLE_UNROLL
    assert n_p % (TABLE_CHUNK * TABLE_UNROLL) == 0
    idx = np.full((1, n_p + n_s), MASK_BUCKET, np.int32)
    flat = np.concatenate([pidx.reshape(-1), cached.reshape(-1), new.reshape(-1)])
    idx[0, :flat.size] = flat
    tab = jnp.concatenate(
        [rel_bias.astype(F32).T,
         jnp.full((N_HEADS, 1), MASK_VALUE, F32),
         jnp.zeros((N_HEADS, TABLE_ROWS - N_BUCKETS - 1), F32)], axis=1)
    whole = lambda a: pl.BlockSpec(a, lambda i: (0, 0))
    prompt, samp = pl.pallas_call(
        _bias_table_kernel,
        out_shape=(jax.ShapeDtypeStruct((N_HEADS, n_p), F32), jax.ShapeDtypeStruct((N_HEADS, n_s), F32)),
        grid=(1,),
        in_specs=[whole((N_HEADS, TABLE_ROWS)), whole((1, n_p + n_s))],
        out_specs=(whole((N_HEADS, n_p)), whole((N_HEADS, n_s))),
        compiler_params=_cparams(("arbitrary",), 2 * (_nbytes((N_HEADS, n_p + n_s), F32)
                                                      + _nbytes((SUBLANES, n_p + n_s), jnp.int32))),
        name="bias_table",
    )(tab, jnp.asarray(idx))
    prompt = prompt.reshape(N_HEADS, len(DILATIONS), 2, DIL_STEPS, 2 * DIL_STEPS)
    return prompt, samp[:, :cached.size], samp[:, cached.size:cached.size + LANES]


def _rms(x, g):
    return x * lax.rsqrt(jnp.mean(x * x, axis=-1, keepdims=True) + EPS) * g


def _head_rms(p, g, gmat):
    sq = (p * p).astype(BF16)
    ms = jnp.concatenate([jnp.dot(sq[:, c:c + LANES], gmat, preferred_element_type=F32)
                          for c in range(0, p.shape[1], LANES)], axis=1)
    return p * lax.rsqrt(ms + EPS) * g


def _cat_groups(ref, rows):
    return jnp.concatenate([ref[g, rows, :] for g in range(N_GROUPS)], axis=1)


def _sub_tiles(tm, sub):
    sub = min(tm, sub)
    return [pl.ds(r, sub) for r in range(0, tm, sub)]


def _proj_kernel(*refs, normed, emit_w, n_cast, t_plane, tiles_per_seq, first_kept):
    i, j = pl.program_id(0), pl.program_id(1)
    if normed:
        x_ref, g_ref, w_ref, gain_ref, gmat_ref, o_ref, t_ref, *rest = refs
        n_ref = rest[-1]

        @pl.when(j == 0)
        def _():
            n_ref[...] = _rms(x_ref[...], g_ref[...]).astype(BF16)
    else:
        n_ref, w_ref, *rest = refs
        cast_in, (o_ref, t_ref, *rest) = rest[:n_cast], rest[n_cast:]
        for src, dst in zip(cast_in, rest[len(rest) - n_cast:]):
            dst[...] = src[...].astype(BF16)

    w = w_ref[...].astype(BF16)
    if emit_w:
        rest[0][...] = w
    y = jnp.dot(n_ref[...], w, preferred_element_type=F32)
    if normed:
        y = _head_rms(y, gain_ref[...], gmat_ref[...])
    for g in range(PROJ_COLS // LANES):
        o_ref[g] = y[:, g * LANES:(g + 1) * LANES]

    @pl.when(jnp.logical_and(j // PROJ_SPLIT == t_plane, i % tiles_per_seq >= first_kept))
    def _():
        t_ref[...] = y.T


def _proj(x, w_in, plane0, n_planes, t_plane, seq, kw, tm, g_mix=None, gains=None, gmat=None,
          emit_bf16_weights=False, side_casts=()):
    m, dm = x.shape
    da = D_ATTN
    assert seq % tm == 0 and kw % tm == 0 and m % seq == 0
    assert not emit_bf16_weights or m == tm, "weight copies are written once: one row tile"
    tps, first_kept, batch = seq // tm, (seq - kw) // tm, m // seq
    normed = gains is not None
    assert not (normed and side_casts)
    grp = PROJ_COLS // LANES
    n_cols = n_planes * PROJ_SPLIT
    n_steps = (m // tm) * n_cols
    w_spec = pl.BlockSpec((dm, PROJ_COLS), lambda i, j: (0, plane0 * PROJ_SPLIT + j))
    if normed:
        in_specs = [pl.BlockSpec((tm, dm), lambda i, j: (i, 0)), pl.BlockSpec((1, dm), lambda i, j: (0, 0)), w_spec,
                    pl.BlockSpec((None, 1, PROJ_COLS), lambda i, j: (j, 0, 0)),
                    _resident((LANES, LANES), lambda i, j: (0, 0))]
        args = [x, g_mix, w_in, gains.reshape(n_planes * PROJ_SPLIT, 1, PROJ_COLS), gmat]
    else:
        in_specs = [pl.BlockSpec((tm, dm), lambda i, j: (i, 0)), w_spec]
        args = [x, w_in]
    slab_specs = []
    for a in side_casts:
        assert a.shape[0] % (n_steps * BF16_ROWS) == 0, "whole packed bf16 row tiles per step"
        slab_specs.append(pl.BlockSpec((a.shape[0] // n_steps, a.shape[1]), lambda i, j: (i * n_cols + j, 0)))
    in_specs += slab_specs
    args += list(side_casts)

    def t_index(i, j):
        kept = i % tps >= first_kept
        half = jnp.where(kept, jnp.clip(j - t_plane * PROJ_SPLIT, 0, PROJ_SPLIT - 1), 0)
        return (i // tps, half, jnp.maximum(i % tps - first_kept, 0))

    out_specs = [pl.BlockSpec((None, grp, tm, LANES), lambda i, j: (j // PROJ_SPLIT, j % PROJ_SPLIT, i, 0)),
                 pl.BlockSpec((None, PROJ_COLS, tm), t_index)]
    out_shape = [jax.ShapeDtypeStruct((n_planes, N_GROUPS, m, LANES), F32),
                 jax.ShapeDtypeStruct((batch, da, kw), F32)]
    if emit_bf16_weights:
        out_specs.append(pl.BlockSpec((dm, PROJ_COLS), lambda i, j: (0, j)))
        out_shape.append(jax.ShapeDtypeStruct((dm, n_planes * da), BF16))
    if normed:
        out_specs.append(pl.BlockSpec((tm, dm), lambda i, j: (i, 0)))
        out_shape.append(jax.ShapeDtypeStruct((m, dm), BF16))
    out_specs += slab_specs
    out_shape += [jax.ShapeDtypeStruct(a.shape, BF16) for a in side_casts]
    vmem = (2 * (_nbytes((tm, dm), x.dtype) + _nbytes((dm, PROJ_COLS), w_in.dtype) + _nbytes((dm, PROJ_COLS), BF16)
                 + 2 * _nbytes((tm, PROJ_COLS), F32) + _nbytes((tm, dm), BF16))
            + 6 * _nbytes((tm, PROJ_COLS), F32)
            + 2 * sum(_nbytes(a.shape, F32) + _nbytes(a.shape, BF16) for a in side_casts) // n_steps)
    return pl.pallas_call(
        functools.partial(_proj_kernel, normed=normed, emit_w=emit_bf16_weights, n_cast=len(side_casts),
                          t_plane=t_plane, tiles_per_seq=tps, first_kept=first_kept),
        out_shape=tuple(out_shape),
        grid=(m // tm, n_cols),
        in_specs=in_specs,
        out_specs=tuple(out_specs),
        compiler_params=_cparams(("arbitrary", "arbitrary"), vmem),
        name="in_proj_qk" if normed else "in_proj_vhbc",
    )(*args)


def _gather_rows(ref, starts, run):
    parts = [ref[pl.ds(s, run), :] for s in starts]
    return parts[0] if len(parts) == 1 else jnp.concatenate(parts, axis=0)


def _scatter_rows(ref, starts, run, val):
    for i, s in enumerate(starts):
        ref[pl.ds(s, run), :] = val[i * run:(i + 1) * run]


def _sample_scores(row, q_ref, kn_ref, kt_ref, bias_ref, bnew_ref):
    lb = kt_ref.shape[1]
    nd = len(DILATIONS)
    seg = (lax.broadcasted_iota(jnp.int32, (N_HEADS, D_ATTN), 1) // HEAD_DIM
           == lax.broadcasted_iota(jnp.int32, (N_HEADS, D_ATTN), 0))
    qe = jnp.where(seg, _cat_groups(q_ref, row), 0.0).astype(BF16)
    kn = _cat_groups(kn_ref, row).astype(BF16).astype(F32)
    s_new = jnp.sum(qe.astype(F32) * kn, axis=1, keepdims=True)
    sn = [s_new + bnew_ref[:, di:di + 1] for di in range(nd)]
    starts = range(0, lb, SAMPLE_POS_CHUNK)
    chunks = [pl.ds(c, SAMPLE_POS_CHUNK) for c in starts]
    sc = []
    for c, ch in zip(starts, chunks):
        s = jnp.dot(qe, kt_ref[:, ch].astype(BF16), preferred_element_type=F32)
        sc.append([s + bias_ref[:, pl.ds(di * lb + c, SAMPLE_POS_CHUNK)] for di in range(nd)])
    m = functools.reduce(jnp.maximum, [jnp.max(x, axis=1, keepdims=True) for row_ in sc for x in row_] + sn)
    pn = functools.reduce(jnp.add, [jnp.exp(x - m) for x in sn])
    ps = [functools.reduce(jnp.add, [jnp.exp(x - m) for x in sd]) for sd in sc]
    return seg, chunks, ps, pn


def _sample_output(row, state, vn_ref, vt_ref, o_ref):
    seg, chunks, ps, pn = state
    den = pn
    o = pn * _cat_groups(vn_ref, row)
    for ch, p in zip(chunks, ps):
        den = den + jnp.sum(p, axis=1, keepdims=True)
        o = o + lax.dot_general(p.astype(BF16), vt_ref[:, ch].astype(BF16),
                                (((1,), (1,)), ((), ())), preferred_element_type=F32)
    out = jnp.sum(jnp.where(seg, o / den, 0.0), axis=0, keepdims=True)
    for g in range(N_GROUPS):
        o_ref[g, row, :] = out[:, g * LANES:(g + 1) * LANES]


def _attn_kernel(q_ref, k_ref, v_ref, bias_ref, sq_ref, skn_ref, svn_ref, ck_hbm, cv_hbm, sbias_ref,
                 sbnew_ref, o_ref, so_ref, qp, kp, vp, acc, m_sc, l_sc, tmp, kbuf, vbuf, sem, *, samples_per_step):
    sb = pl.program_id(2)
    step = (pl.program_id(0) * pl.num_programs(1) + pl.program_id(1)) * pl.num_programs(2) + sb
    n_samples = ck_hbm.shape[0]
    slot = sb % 2
    pslot = 1 - slot
    kcur, vcur = kp.at[slot], vp.at[slot]
    kprev, vprev = kp.at[pslot], vp.at[pslot]

    def copies(g, slot):
        return (pltpu.make_async_copy(ck_hbm.at[g], kbuf.at[slot], sem.at[0, slot]),
                pltpu.make_async_copy(cv_hbm.at[g], vbuf.at[slot], sem.at[1, slot]))

    def start_copies(g, slot):
        for priority, c in enumerate(copies(g, slot)):
            c.start(priority=priority)

    @pl.when(step == 0)
    def _():
        start_copies(0, 0)

    def sample(t):
        g = step * samples_per_step + t
        slot = t % 2
        row = pl.ds(g, 1)

        @pl.when(g + 1 < n_samples)
        def _():
            start_copies(g + 1, 1 - slot)

        for c in copies(g, slot):
            c.wait()
        state = _sample_scores(row, sq_ref, skn_ref, kbuf.at[slot], sbias_ref, sbnew_ref)
        _sample_output(row, state, svn_ref, vbuf.at[slot], so_ref)

    quarter = SUPER // 4
    for src, dst in ((q_ref, qp), (k_ref, kcur), (v_ref, vcur)):
        for c in range(4):
            tmp[pl.ds(c * quarter, quarter), :] = src[pl.ds(c, quarter, stride=4), :]
        for c in range(4):
            for b in range(4):
                dst[pl.ds((4 * b + c) * DIL_STEPS, DIL_STEPS), :] = tmp[pl.ds(c * quarter + b, DIL_STEPS, stride=4), :]

    @pl.when(sb == 0)
    def _():
        kprev[...] = jnp.zeros(kprev.shape, F32)
        vprev[...] = jnp.zeros(vprev.shape, F32)

    head0 = lax.broadcasted_iota(jnp.int32, (DIL_STEPS, LANES), 1) < HEAD_DIM

    def block(di, starts, run, prev_ref_k, prev_ref_v, prev_starts, first, mode, out_rows=None):
        qb = _gather_rows(qp, starts, run)
        kb = jnp.concatenate([_gather_rows(prev_ref_k, prev_starts, run),
                              _gather_rows(kcur, starts, run)], axis=0).astype(BF16)
        vb = jnp.concatenate([_gather_rows(prev_ref_v, prev_starts, run),
                              _gather_rows(vcur, starts, run)], axis=0).astype(BF16)
        q2 = jnp.concatenate([jnp.where(head0, qb, 0.0), jnp.where(head0, 0.0, qb)], axis=0).astype(BF16)
        s = lax.dot_general(q2, kb, (((1,), (1,)), ((), ())), preferred_element_type=F32)
        s = s + jnp.concatenate([bias_ref[h, di, first] for h in range(HEADS_PER_GROUP)], axis=0)
        m2 = jnp.max(s, axis=1, keepdims=True)
        p = jnp.exp(s - m2)
        l2 = jnp.sum(p, axis=1, keepdims=True)
        o2 = jnp.dot(p.astype(BF16), vb, preferred_element_type=F32)
        m_c = jnp.where(head0, m2[:DIL_STEPS], m2[DIL_STEPS:])
        l_c = jnp.where(head0, l2[:DIL_STEPS], l2[DIL_STEPS:])
        o_c = jnp.where(head0, o2[:DIL_STEPS], o2[DIL_STEPS:])
        if mode == "init":
            _scatter_rows(m_sc, starts, run, m_c)
            _scatter_rows(l_sc, starts, run, l_c)
            _scatter_rows(acc, starts, run, o_c)
            return
        m_o = _gather_rows(m_sc, starts, run)
        m_n = jnp.maximum(m_o, m_c)
        a_o = jnp.exp(m_o - m_n)
        a_c = jnp.exp(m_c - m_n)
        l_n = _gather_rows(l_sc, starts, run) * a_o + l_c * a_c
        o_n = _gather_rows(acc, starts, run) * a_o + o_c * a_c
        if mode == "merge":
            _scatter_rows(m_sc, starts, run, m_n)
            _scatter_rows(l_sc, starts, run, l_n)
            _scatter_rows(acc, starts, run, o_n)
        else:
            o_ref[out_rows, :] = o_n / l_n

    def run_dilation(di, mode):
        d = DILATIONS[di]
        runs = N_RES // d
        run = DIL_STEPS // runs
        nblk = SUPER // d // DIL_STEPS

        def body(it, carry):
            c = it // nblk
            n = it % nblk
            starts = [pl.multiple_of((d * b + c) * DIL_STEPS + run * n, SUBLANES) for b in range(runs)]
            pn = jnp.where(n > 0, n - 1, nblk - 1)
            prev_starts = [pl.multiple_of((d * b + c) * DIL_STEPS + run * pn, SUBLANES)
                           for b in range(runs)]
            pslot_n = jnp.where(n > 0, slot, pslot)
            first = jnp.logical_and(sb == 0, n == 0).astype(jnp.int32)
            out_rows = pl.ds(it, DIL_STEPS, stride=N_RES) if mode == "final" else None
            block(di, starts, run, kp.at[pslot_n], vp.at[pslot_n], prev_starts, first, mode, out_rows)
            return carry

        lax.fori_loop(0, d * nblk, body, 0, unroll=ATTN_UNROLL)

    phases = ("init", "merge", "final")
    for t in range(max(samples_per_step, len(phases))):
        if t < samples_per_step:
            sample(t)
        if t < len(phases):
            run_dilation(t, phases[t])


def _attn(qk, vhbc, bias, batch, seq, sqk, svhbc, cache_k, cache_v, sbias, sbias_new):
    nsb = seq // SUPER
    m = batch * seq
    bd, lb = cache_k.shape[0], cache_k.shape[1]
    da = D_ATTN
    n_steps = N_GROUPS * batch * nsb
    assert bd % (2 * n_steps) == 0 and lb % SAMPLE_POS_CHUNK == 0
    assert N_RES == 16, "the residue-major copy is written as two stride-4 passes"

    def feature_major(c):
        return jnp.transpose(c, (0, 2, 3, 1)).reshape(bd, da, lb)

    def plane(p):
        return pl.BlockSpec((None, None, SUPER, LANES), lambda g, b, s: (p, g, b * nsb + s, 0))

    def splane(p):
        return pl.BlockSpec((None, N_GROUPS, bd, LANES), lambda g, b, s: (p, 0, 0, 0))

    fixed2 = lambda g, b, s: (0, 0)
    bias_spec = pl.BlockSpec((HEADS_PER_GROUP,) + bias.shape[1:], lambda g, b, s: (g, 0, 0, 0, 0),
                             pipeline_mode=pl.Buffered(1))
    hbm = pl.BlockSpec(memory_space=pl.ANY)
    blk_bytes = _nbytes((SUPER, LANES), F32)
    return pl.pallas_call(
        functools.partial(_attn_kernel, samples_per_step=bd // n_steps),
        out_shape=(jax.ShapeDtypeStruct((N_GROUPS, m, LANES), F32),
                   jax.ShapeDtypeStruct((N_GROUPS, bd, LANES), F32)),
        grid=(N_GROUPS, batch, nsb),
        in_specs=[plane(0), plane(1), plane(0), bias_spec,
                  splane(0), splane(1), splane(0), hbm, hbm,
                  pl.BlockSpec(sbias.shape, fixed2), pl.BlockSpec(sbias_new.shape, fixed2)],
        out_specs=(pl.BlockSpec((None, SUPER, LANES), lambda g, b, s: (g, b * nsb + s, 0)),
                   pl.BlockSpec((N_GROUPS, bd, LANES), lambda g, b, s: (0, 0, 0))),
        scratch_shapes=[pltpu.VMEM((SUPER, LANES), F32),
                        pltpu.VMEM((2, SUPER, LANES), F32), pltpu.VMEM((2, SUPER, LANES), F32),
                        pltpu.VMEM((SUPER, LANES), F32), pltpu.VMEM((SUPER, LANES), F32),
                        pltpu.VMEM((SUPER, LANES), F32), pltpu.VMEM((SUPER, LANES), F32),
                        pltpu.VMEM((2, da, lb), F32), pltpu.VMEM((2, da, lb), F32),
                        pltpu.SemaphoreType.DMA((2, 2))],
        compiler_params=_cparams(("arbitrary", "arbitrary", "arbitrary"),
                                 (2 * 4 + 9) * blk_bytes
                                 + _nbytes((HEADS_PER_GROUP,) + bias.shape[1:], F32)
                                 + 4 * _nbytes((da, lb), F32) + 2 * 4 * _nbytes((bd, da), F32)
                                 + 2 * _nbytes(sbias.shape, F32)),
        name="attn",
    )(qk, qk, vhbc, bias, sqk, sqk, svhbc, feature_major(cache_k), feature_major(cache_v), sbias, sbias_new)


def _mix_tail(x, a, conv, ga_ref, gc_ref, w_ref):
    cat = jnp.concatenate([_rms(a, ga_ref[...]), _rms(conv, gc_ref[...])], axis=1).astype(BF16)
    return x + jnp.dot(cat, w_ref[...], preferred_element_type=F32)


def _mix_prompt_kernel(x_ref, a_ref, h_ref, b_ref, c_ref, hh_ref, ch_ref, cw_ref, ga_ref, gc_ref, gf_ref, w_ref,
                       o_ref, n_ref, ut_ref, *, tiles_per_seq):
    tm = x_ref.shape[0]
    seq_start = pl.program_id(0) % tiles_per_seq == 0
    all8 = pl.ds(0, SUBLANES)
    halo = jnp.where(seq_start, 0.0, _cat_groups(hh_ref, all8) * _cat_groups(ch_ref, all8))
    for rows in _sub_tiles(tm, MIX_SUB_ROWS):
        u = _cat_groups(h_ref, rows) * _cat_groups(c_ref, rows)
        rid = lax.broadcasted_iota(jnp.int32, u.shape, 0)
        u1 = jnp.where(rid == 0, halo[7:8], pltpu.roll(u, 1, axis=0))
        u2 = jnp.where(rid == 0, halo[6:7], jnp.where(rid == 1, halo[7:8], pltpu.roll(u, 2, axis=0)))
        cy = cw_ref[0:1] * u2 + cw_ref[1:2] * u1 + cw_ref[2:3] * u
        conv = _cat_groups(b_ref, rows) * cy
        h1 = _mix_tail(x_ref[rows, :], _cat_groups(a_ref, rows), conv, ga_ref, gc_ref, w_ref)
        o_ref[rows, :] = h1
        n_ref[rows, :] = _rms(h1, gf_ref[...]).astype(BF16)
        halo = u[u.shape[0] - SUBLANES:]
    ut_ref[...] = halo


def _mix_sample_kernel(x_ref, a_ref, h_ref, b_ref, c_ref, buf_ref, cw_ref, ga_ref, gc_ref, gf_ref, w_ref,
                       o_ref, n_ref, u_ref):
    every = pl.ds(0, x_ref.shape[0])
    u = _cat_groups(h_ref, every) * _cat_groups(c_ref, every)
    dc = u.shape[1]
    cy = cw_ref[0:1] * buf_ref[:, :dc] + cw_ref[1:2] * buf_ref[:, dc:] + cw_ref[2:3] * u
    conv = _cat_groups(b_ref, every) * cy
    h1 = _mix_tail(x_ref[...], _cat_groups(a_ref, every), conv, ga_ref, gc_ref, w_ref)
    o_ref[...] = h1
    n_ref[...] = _rms(h1, gf_ref[...]).astype(BF16)
    u_ref[...] = u


def _mix_out_prompt(x, attn, vhbc, conv_w, ga, gc, gf, w_out, seq, tm):
    m, dm = x.shape
    dc = conv_w.shape[1]
    fixed = lambda i: (0, 0)
    halo_blocks = tm // SUBLANES

    def plane(p):
        return pl.BlockSpec((None, N_GROUPS, tm, LANES), lambda i: (p, 0, i, 0))

    def halo(p):
        return pl.BlockSpec((None, N_GROUPS, SUBLANES, LANES),
                            lambda i: (p, 0, jnp.maximum(i * halo_blocks - 1, 0), 0))

    tile = _nbytes((tm, dc), F32)
    return pl.pallas_call(
        functools.partial(_mix_prompt_kernel, tiles_per_seq=seq // tm),
        out_shape=(jax.ShapeDtypeStruct((m, dm), F32), jax.ShapeDtypeStruct((m, dm), BF16),
                   jax.ShapeDtypeStruct((m // tm, SUBLANES, dc), F32)),
        grid=(m // tm,),
        in_specs=[pl.BlockSpec((tm, dm), lambda i: (i, 0)),
                  pl.BlockSpec((N_GROUPS, tm, LANES), lambda i: (0, i, 0)),
                  plane(1), plane(2), plane(3), halo(1), halo(3),
                  pl.BlockSpec((CONV_W, dc), fixed), pl.BlockSpec((1, D_ATTN), fixed),
                  pl.BlockSpec((1, dc), fixed), pl.BlockSpec((1, dm), fixed), _resident(w_out.shape, fixed)],
        out_specs=(pl.BlockSpec((tm, dm), lambda i: (i, 0)), pl.BlockSpec((tm, dm), lambda i: (i, 0)),
                   pl.BlockSpec((None, SUBLANES, dc), lambda i: (i, 0, 0))),
        compiler_params=_cparams(("arbitrary",), 2 * 9 * tile + _nbytes(w_out.shape, BF16)
                                 + 4 * _nbytes((min(tm, MIX_SUB_ROWS), dm), F32)),
        name="mix_out_prompt",
    )(x, attn, vhbc, vhbc, vhbc, vhbc, vhbc, conv_w, ga, gc, gf, w_out)


def _mix_out_sample(x, attn, vhbc, buf, conv_w, ga, gc, gf, w_out):
    m, dm = x.shape
    dc = conv_w.shape[1]
    full = lambda a: pl.BlockSpec(a.shape, lambda i: (0,) * a.ndim)

    def plane(p):
        return pl.BlockSpec((None, N_GROUPS, m, LANES), lambda i: (p, 0, 0, 0))

    return pl.pallas_call(
        _mix_sample_kernel,
        out_shape=(jax.ShapeDtypeStruct((m, dm), F32), jax.ShapeDtypeStruct((m, dm), BF16),
                   jax.ShapeDtypeStruct((m, dc), F32)),
        grid=(1,),
        in_specs=[full(x), full(attn), plane(1), plane(2), plane(3), full(buf), full(conv_w), full(ga),
                  full(gc), full(gf), full(w_out)],
        out_specs=(pl.BlockSpec((m, dm), lambda i: (0, 0)), pl.BlockSpec((m, dm), lambda i: (0, 0)),
                   pl.BlockSpec((m, dc), lambda i: (0, 0))),
        compiler_params=_cparams(("arbitrary",), 2 * (11 * _nbytes((m, dc), F32) + _nbytes(w_out.shape, BF16))),
        name="mix_out_sample",
    )(x, attn, vhbc, vhbc, vhbc, buf, conv_w, ga, gc, gf, w_out)


def _ffn_kernel(n_ref, wg_ref, wu_ref, wd_ref, o_ref):
    @pl.when(pl.program_id(1) == 0)
    def _():
        o_ref[...] = jnp.zeros(o_ref.shape, F32)

    n = n_ref[...]
    acc = None
    for c in range(0, wg_ref.shape[1], FFN_SUB_COLS):
        cols = pl.ds(c, FFN_SUB_COLS)
        gate = jnp.dot(n, wg_ref[:, cols], preferred_element_type=F32)
        up = jnp.dot(n, wu_ref[:, cols], preferred_element_type=F32)
        act = (gate / (1.0 + jnp.exp(-gate)) * up).astype(BF16)
        part = jnp.dot(act, wd_ref[cols, :], preferred_element_type=F32)
        acc = part if acc is None else acc + part
    o_ref[...] += acc


def _ffn(n, wg, wu, wd, tm, tf):
    m, dm = n.shape
    dff = wg.shape[1]
    row = lambda i, f: (i, 0)
    return pl.pallas_call(
        _ffn_kernel,
        out_shape=jax.ShapeDtypeStruct((m, dm), F32),
        grid=(m // tm, dff // tf),
        in_specs=[pl.BlockSpec((tm, dm), row),
                  pl.BlockSpec((dm, tf), lambda i, f: (0, f)), pl.BlockSpec((dm, tf), lambda i, f: (0, f)),
                  pl.BlockSpec((tf, dm), lambda i, f: (f, 0))],
        out_specs=pl.BlockSpec((tm, dm), row),
        compiler_params=_cparams(("arbitrary", "arbitrary"),
                                 2 * (_nbytes((tm, dm), F32) + _nbytes((tm, dm), BF16))
                                 + 2 * 3 * _nbytes((dm, tf), BF16) + 4 * _nbytes((tm, tf), F32)),
        name="ffn",
    )(n, wg, wu, wd)


def _ple_kernel(h_ref, d_ref, p_ref, g_ref, wg_ref, wp_ref, o_ref):
    for rows in _sub_tiles(h_ref.shape[0], PLE_SUB_ROWS):
        x = h_ref[rows, :] + d_ref[rows, :]
        z = jnp.dot(_rms(x, g_ref[...]).astype(BF16), wg_ref[...], preferred_element_type=F32)
        e = jnp.dot(p_ref[rows, :].astype(BF16), wp_ref[...], preferred_element_type=F32)
        o_ref[rows, :] = x + e / (1.0 + jnp.exp(-z))


def _ple(h, d, p, g, wg, wp, tm):
    m, dm = h.shape
    dp = p.shape[1]
    row = lambda i: (i, 0)
    fixed = lambda i: (0, 0)
    return pl.pallas_call(
        _ple_kernel,
        out_shape=jax.ShapeDtypeStruct((m, dm), F32),
        grid=(m // tm,),
        in_specs=[pl.BlockSpec((tm, dm), row), pl.BlockSpec((tm, dm), row), pl.BlockSpec((tm, dp), row),
                  pl.BlockSpec((1, dm), fixed), _resident(wg.shape, fixed), _resident(wp.shape, fixed)],
        out_specs=pl.BlockSpec((tm, dm), row),
        compiler_params=_cparams(("arbitrary",),
                                 2 * (3 * _nbytes((tm, dm), F32) + _nbytes((tm, dp), F32))
                                 + _nbytes(wg.shape, BF16) + _nbytes(wp.shape, BF16)
                                 + 6 * _nbytes((min(tm, PLE_SUB_ROWS), dm), F32)),
        name="ple",
    )(h, d, p, g, wg, wp)


def _tiles(m):
    return dict(proj=min(m, 1024), proj_n=min(m, 2048), mix=min(m, 512), ffn=min(m, 1024), ffn_cols=512,
                ple=min(m, 512))


def _window_rows(t, batch, kw):
    return jnp.transpose(t.reshape(batch, N_HEADS, HEAD_DIM, kw), (0, 3, 1, 2))


def kernel(x_prompt, x_sample, p_prompt, p_sample, cache_k, cache_v, state_conv, rel_bias, g_mix, w_in,
           q_norm, k_norm, conv_w, g_attn_out, g_conv_out, w_out, g_ffn, w_gate, w_up, w_down, g_ple,
           w_ple_gate, w_ple_proj):
    depth = g_mix.shape[0]
    batch, seq, dm = x_prompt.shape
    bd, dec_seq, _ = x_sample.shape
    dc = conv_w.shape[2]
    assert depth == 1 and dec_seq == 1, "single layer, one new position per sample"
    assert seq % SUPER == 0 and dm == D_ATTN + dc and dc == D_ATTN
    assert w_gate.shape[2] % 512 == 0

    bias_p, bias_s, bias_s_new = _bias_tables(rel_bias, cache_k.shape[2])
    gmat = jnp.asarray(np.kron(np.eye(HEADS_PER_GROUP, dtype=np.float32),
                               np.full((HEAD_DIM, HEAD_DIM), 1.0 / HEAD_DIM, np.float32)), BF16)

    i = 0
    row2 = lambda a: a.reshape(1, -1).astype(F32)
    g_mix_i, g_ffn_i, g_ple_i = row2(g_mix[i]), row2(g_ffn[i]), row2(g_ple[i])
    qk_gains = jnp.stack([row2(jnp.tile(q_norm[i], N_HEADS) * SCALE), row2(jnp.tile(k_norm[i], N_HEADS))])
    ga, gc = row2(g_attn_out[i]), row2(g_conv_out[i])
    cw = conv_w[i].astype(F32)
    wpp_i = w_ple_proj[i].astype(BF16)

    mp = batch * seq
    tp, ts = _tiles(mp), _tiles(bd)
    kw = min(SUPER, seq)
    xp = x_prompt.reshape(mp, dm)
    xs = x_sample.reshape(bd, dm)
    qks, kts, w_qk, nrm_s = _proj(xs, w_in[i], 0, 2, 1, bd, bd, ts["proj"], g_mix_i, qk_gains, gmat,
                                  emit_bf16_weights=True)
    vhbcs, vts, w_vhbc = _proj(nrm_s, w_in[i], 2, 4, 0, bd, bd, ts["proj_n"], emit_bf16_weights=True)
    qk, kt, nrm = _proj(xp, w_qk, 0, 2, 1, seq, kw, tp["proj"], g_mix_i, qk_gains, gmat)
    vhbc, vt, wg_i, wu_i, wd_i, w_out_i, wpg_i = _proj(
        nrm, w_vhbc, 0, 4, 0, seq, kw, tp["proj_n"],
        side_casts=(w_gate[i], w_up[i], w_down[i], w_out[i], w_ple_gate[i]))
    attn, attn_s = _attn(qk, vhbc, bias_p, batch, seq, qks, vhbcs, cache_k[i], cache_v[i], bias_s, bias_s_new)

    buf = state_conv[i].astype(F32)
    hs, ns, us = _mix_out_sample(xs, attn_s, vhbcs, buf.reshape(bd, (CONV_W - 1) * dc), cw, ga, gc, g_ffn_i,
                                 w_out_i)
    ds = _ffn(ns, wg_i, wu_i, wd_i, ts["ffn"], ts["ffn_cols"])
    hs = _ple(hs, ds, p_sample[i].reshape(bd, -1), g_ple_i, wpg_i, wpp_i, ts["ple"])

    h, n, u_tail = _mix_out_prompt(xp, attn, vhbc, cw, ga, gc, g_ffn_i, w_out_i, seq, tp["mix"])
    d = _ffn(n, wg_i, wu_i, wd_i, tp["ffn"], tp["ffn_cols"])
    h = _ple(h, d, p_prompt[i].reshape(mp, -1), g_ple_i, wpg_i, wpp_i, tp["ple"])
    y_prompt = h.reshape(batch, seq, dm)
    k_prompt = _window_rows(kt, batch, kw)[None]
    v_prompt = _window_rows(vt, batch, kw)[None]
    tiles_per_seq = seq // tp["mix"]
    conv_prompt = u_tail.reshape(batch, tiles_per_seq, SUBLANES, dc)[None, :, -1, SUBLANES - (CONV_W - 1):]

    y_sample = hs.reshape(bd, dec_seq, dm)
    k_sample = _window_rows(kts, 1, bd).reshape(1, bd, dec_seq, N_HEADS, HEAD_DIM)
    v_sample = _window_rows(vts, 1, bd).reshape(1, bd, dec_seq, N_HEADS, HEAD_DIM)
    conv_sample = jnp.concatenate([buf[:, 1:], us[:, None, :]], axis=1)[None]

    return (y_prompt, y_sample, k_prompt, v_prompt, conv_prompt, k_sample, v_sample, conv_sample)
```
